```python
import jax, jax.numpy as jnp
from jax import lax
import numpy as np

D_MODEL = 2048
BATCH = 2
SEQ = 16384
DEPTH = 1

HEAD_DIM = 64
N_Q_HEADS = 16
N_KV_HEADS = 2
GQA_GROUP = N_Q_HEADS // N_KV_HEADS
ATTN_WIDTH = N_Q_HEADS * HEAD_DIM
KV_WIDTH = N_KV_HEADS * HEAD_DIM
WINDOW = 128
BLOCK = 128
ROPE_THETA = 10000.0
HG_HEAD_DIM = 128
HG_HEADS = 8
HG_WIDTH = HG_HEADS * HG_HEAD_DIM
CHUNK = 64
D_FF = 5632
N_BRANCH = 2
EPS = 1e-6
IN_SPLITS = (ATTN_WIDTH, KV_WIDTH, KV_WIDTH, HG_WIDTH, HG_WIDTH, HG_WIDTH, HG_WIDTH, N_BRANCH * D_MODEL)
IN_COLS = sum(IN_SPLITS)

kernel_name = 'hybrid_swa_sink_hgrn2_macaron_layer'


def rmsnorm(x, g):
    xf = x.astype(jnp.float32)
    y = xf * lax.rsqrt(jnp.mean(xf * xf, axis=-1, keepdims=True) + EPS)
    return (y * g.astype(jnp.float32)).astype(x.dtype)


def swiglu(h, w_gu, w_down):
    gate, up = jnp.split(h @ w_gu, 2, axis=-1)
    return (jax.nn.silu(gate) * up) @ w_down


def rope(t, positions):
    half = HEAD_DIM // 2
    inv_freq = ROPE_THETA ** (-jnp.arange(half, dtype=jnp.float32) * 2.0 / HEAD_DIM)
    ang = positions.astype(jnp.float32)[..., None] * inv_freq
    cos = jnp.cos(ang)[:, :, None, :]
    sin = jnp.sin(ang)[:, :, None, :]
    t1, t2 = t[..., :half], t[..., half:]
    return jnp.concatenate([t1 * cos - t2 * sin, t2 * cos + t1 * sin], axis=-1)


def sliding_window_attention(q, k, v, sinks):
    B, S = q.shape[0], q.shape[1]
    nb = S // BLOCK
    qb = q.reshape(B, nb, BLOCK, N_KV_HEADS, GQA_GROUP, HEAD_DIM)
    kb = k.reshape(B, nb, BLOCK, N_KV_HEADS, HEAD_DIM)
    vb = v.reshape(B, nb, BLOCK, N_KV_HEADS, HEAD_DIM)

    def with_prev(t):
        prev = jnp.pad(t, ((0, 0), (1, 0), (0, 0), (0, 0), (0, 0)))[:, :-1]
        return jnp.concatenate([prev, t], axis=2)

    kw, vw = with_prev(kb), with_prev(vb)
    scores = jnp.einsum('bnqkgd,bnskd->bnkgqs', qb, kw) * (HEAD_DIM ** -0.5)
    i = jnp.arange(BLOCK)[:, None]
    j = jnp.arange(2 * BLOCK)[None, :]
    band = (j <= i + BLOCK) & (j > i + BLOCK - WINDOW)
    blk = jnp.arange(nb)[:, None, None]
    valid = band[None] & (blk * BLOCK + j[None] - BLOCK >= 0)
    scores = jnp.where(valid[None, :, None, None], scores, -jnp.inf)
    sink = sinks.astype(jnp.float32).reshape(N_KV_HEADS, GQA_GROUP)[None, None, :, :, None, None]
    m = jnp.maximum(jnp.max(scores, axis=-1, keepdims=True), sink)
    p = jnp.exp(scores - m)
    denom = jnp.sum(p, axis=-1, keepdims=True) + jnp.exp(sink - m)
    out = jnp.einsum('bnkgqs,bnskd->bnqkgd', p / denom, vw)
    return out.reshape(B, S, ATTN_WIDTH)


def hgrn2(q, f_logit, inp, lb):
    B, S = q.shape[0], q.shape[1]
    nc = S // CHUNK

    def heads(t):
        return t.astype(jnp.float32).reshape(B, nc, CHUNK, HG_HEADS, HG_HEAD_DIM).transpose(0, 3, 1, 2, 4)

    f = lb + (1.0 - lb) * jax.nn.sigmoid(f_logit.astype(jnp.float32))
    qh = heads(jax.nn.silu(q.astype(jnp.float32)))
    kh = heads(1.0 - f)
    vh = heads(inp)
    b = jnp.cumsum(heads(jnp.log(f)), axis=3)
    b_ref = b[:, :, :, CHUNK // 2:CHUNK // 2 + 1]
    attn = jnp.einsum('bhncd,bhnsd->bhncs', qh * jnp.exp(b - b_ref), kh * jnp.exp(b_ref - b))
    causal = jnp.tril(jnp.ones((CHUNK, CHUNK), dtype=bool))
    o_intra = jnp.einsum('bhncs,bhnse->bhnce', jnp.where(causal, attn, 0.0), vh)
    b_last = b[:, :, :, -1:]
    upd = jnp.einsum('bhnsd,bhnse->bhnde', kh * jnp.exp(b_last - b), vh)
    decay = jnp.exp(b_last[:, :, :, 0])

    def step(state, xs):
        u_c, d_c = xs
        return d_c[..., None] * state + u_c, state

    s0 = jnp.zeros((B, HG_HEADS, HG_HEAD_DIM, HG_HEAD_DIM), jnp.float32)
    _, s_prev = lax.scan(step, s0, (jnp.moveaxis(upd, 2, 0), jnp.moveaxis(decay, 2, 0)))
    s_prev = jnp.moveaxis(s_prev, 0, 2)
    o_inter = jnp.einsum('bhncd,bhnde->bhnce', qh * jnp.exp(b), s_prev)
    o = o_intra + o_inter
    return o.transpose(0, 2, 3, 1, 4).reshape(B, S, HG_HEADS, HG_HEAD_DIM)


def setup_inputs(seed: int = 0) -> dict:
    key = jax.random.key(seed)
    ks = jax.random.split(key, 20)

    def w(k, shape, fan_in):
        return jax.random.normal(k, shape, jnp.float32) * (fan_in ** -0.5)

    def gain(k, shape):
        return 1.0 + 0.05 * jax.random.normal(k, shape, jnp.float32)

    return {
        'x': jax.random.normal(ks[0], (BATCH, SEQ, D_MODEL), jnp.float32),
        'positions': jnp.tile(jnp.arange(SEQ, dtype=jnp.int32)[None, :], (BATCH, 1)),
        'lb_table': 0.1 * jax.random.normal(ks[1], (DEPTH + 1, HG_WIDTH), jnp.float32),
        'ffn1_norm': gain(ks[2], (DEPTH, D_MODEL)),
        'ffn1_w_gu': w(ks[3], (DEPTH, D_MODEL, 2 * D_FF), D_MODEL),
        'ffn1_w_down': w(ks[4], (DEPTH, D_FF, D_MODEL), D_FF),
        'mix_norm': gain(ks[5], (DEPTH, D_MODEL)),
        'w_in': w(ks[6], (DEPTH, D_MODEL, IN_COLS), D_MODEL),
        'q_norm': gain(ks[7], (DEPTH, HEAD_DIM)),
        'k_norm': gain(ks[8], (DEPTH, HEAD_DIM)),
        'sinks': 0.5 * jax.random.normal(ks[9], (DEPTH, N_Q_HEADS), jnp.float32),
        'hg_out_norm': gain(ks[10], (DEPTH, HG_HEAD_DIM)),
        'w_attn_branch': w(ks[11], (DEPTH, ATTN_WIDTH, D_MODEL), ATTN_WIDTH),
        'w_hg_branch': w(ks[12], (DEPTH, HG_WIDTH, D_MODEL), HG_WIDTH),
        'w_out': w(ks[13], (DEPTH, D_MODEL, D_MODEL), D_MODEL),
        'ffn2_norm': gain(ks[14], (DEPTH, D_MODEL)),
        'ffn2_w_gu': w(ks[15], (DEPTH, D_MODEL, 2 * D_FF), D_MODEL),
        'ffn2_w_down': w(ks[16], (DEPTH, D_FF, D_MODEL), D_FF),
    }


def reference(x, positions, lb_table, ffn1_norm, ffn1_w_gu, ffn1_w_down, mix_norm, w_in, q_norm, k_norm,
              sinks, hg_out_norm, w_attn_branch, w_hg_branch, w_out, ffn2_norm, ffn2_w_gu, ffn2_w_down):
    B, S = x.shape[0], x.shape[1]
    lbs = jnp.cumsum(jax.nn.softmax(lb_table.astype(jnp.float32), axis=0), axis=0)
    split_idx = np.cumsum(IN_SPLITS)[:-1].tolist()
    for l in range(DEPTH):
        x = x + 0.5 * swiglu(rmsnorm(x, ffn1_norm[l]), ffn1_w_gu[l], ffn1_w_down[l])
        h = rmsnorm(x, mix_norm[l])
        z = h @ w_in[l]
        a_q, a_k, a_v, g_q, g_f, g_i, g_o, br = jnp.split(z, split_idx, axis=-1)
        qa = rmsnorm(a_q.astype(jnp.float32).reshape(B, S, N_Q_HEADS, HEAD_DIM), q_norm[l])
        ka = rmsnorm(a_k.astype(jnp.float32).reshape(B, S, N_KV_HEADS, HEAD_DIM), k_norm[l])
        va = a_v.astype(jnp.float32).reshape(B, S, N_KV_HEADS, HEAD_DIM)
        y_attn = sliding_window_attention(rope(qa, positions), rope(ka, positions), va, sinks[l]).astype(x.dtype)
        o_hg = hgrn2(g_q, g_f, g_i, lbs[l])
        gate_hg = jax.nn.silu(g_o.astype(jnp.float32)).reshape(B, S, HG_HEADS, HG_HEAD_DIM)
        y_hg = (rmsnorm(o_hg, hg_out_norm[l]) * gate_hg).reshape(B, S, HG_WIDTH).astype(x.dtype)
        gates = jax.nn.sigmoid(br).reshape(B, S, N_BRANCH, D_MODEL)
        merged = gates[:, :, 0] * (y_attn @ w_attn_branch[l]) + gates[:, :, 1] * (y_hg @ w_hg_branch[l])
        x = x + merged @ w_out[l]
        x = x + 0.5 * swiglu(rmsnorm(x, ffn2_norm[l]), ffn2_w_gu[l], ffn2_w_down[l])
    return x
```

```python
import functools

import jax
import jax.numpy as jnp
from jax import lax
from jax.experimental import pallas as pl
from jax.experimental.pallas import tpu as pltpu

D_MODEL = 2048
HEAD_DIM = 64
N_Q_HEADS = 16
N_KV_HEADS = 2
GQA_GROUP = N_Q_HEADS // N_KV_HEADS
ATTN_WIDTH = N_Q_HEADS * HEAD_DIM
KV_WIDTH = N_KV_HEADS * HEAD_DIM
WINDOW = 128
BLOCK = 128
ROPE_THETA = 10000.0
HG_HEAD_DIM = 128
HG_HEADS = 8
HG_WIDTH = HG_HEADS * HG_HEAD_DIM
CHUNK = 64
D_FF = 5632
EPS = 1e-6
IN_COLS = ATTN_WIDTH + 2 * KV_WIDTH + 4 * HG_WIDTH + 2 * D_MODEL

LANES = 128
PAIRS_PER_GROUP = GQA_GROUP // 2
VMEM_LIMIT = 56 * 1024 * 1024

Z_GQ, Z_GF, Z_GI, Z_GO = 0, 1, 2, 3
Z_BR = 1
Z_AQ = (4 * HG_WIDTH + 2 * D_MODEL) // ATTN_WIDTH
Z_AK = (4 * HG_WIDTH + 2 * D_MODEL + ATTN_WIDTH) // KV_WIDTH
Z_AV = Z_AK + 1

F32 = jnp.float32
BF16 = jnp.bfloat16


def _dot(a, b):
    return jnp.dot(a, b, preferred_element_type=F32)


def _dot_nt(a, b):
    return lax.dot_general(a, b, (((1,), (1,)), ((), ())), preferred_element_type=F32)


def _dot_tn(a, b):
    return lax.dot_general(a, b, (((0,), (0,)), ((), ())), preferred_element_type=F32)


def _rms_rows(x, gain):
    ms = jnp.mean(x * x, axis=-1, keepdims=True)
    return x * lax.rsqrt(ms + EPS) * gain


def _split3(v):
    hi = v.astype(BF16)
    r1 = v - hi.astype(F32)
    mid = r1.astype(BF16)
    lo = (r1 - mid.astype(F32)).astype(BF16)
    return hi, mid, lo


def _ffn_body(x_ref, g_ref, wg_ref, wu_ref, wd_ref, o_ref, h_ref):
    j = pl.program_id(1)

    @pl.when(j == 0)
    def _():
        x = x_ref[...]
        h_ref[...] = _rms_rows(x, g_ref[...]).astype(BF16)
        o_ref[...] = x

    h = h_ref[...]
    gate = _dot(h, wg_ref[...])
    up = _dot(h, wu_ref[...])
    act = (gate * jax.nn.sigmoid(gate)) * (up * 0.5)
    o_ref[...] += _dot(act.astype(BF16), wd_ref[...])


def _ffn(x, gain, w_gu, w_down, *, tm, tf):
    n = x.shape[0]
    nf = D_FF // tf
    return pl.pallas_call(
        _ffn_body,
        grid=(n // tm, nf),
        in_specs=[
            pl.BlockSpec((tm, D_MODEL), lambda i, j: (i, 0)),
            pl.BlockSpec((1, D_MODEL), lambda i, j: (0, 0)),
            pl.BlockSpec((D_MODEL, tf), lambda i, j: (0, j)),
            pl.BlockSpec((D_MODEL, tf), lambda i, j: (0, j + nf)),
            pl.BlockSpec((tf, D_MODEL), lambda i, j: (j, 0)),
        ],
        out_specs=pl.BlockSpec((tm, D_MODEL), lambda i, j: (i, 0)),
        out_shape=jax.ShapeDtypeStruct((n, D_MODEL), F32),
        scratch_shapes=[pltpu.VMEM((tm, D_MODEL), BF16)],
        compiler_params=pltpu.CompilerParams(
            dimension_semantics=("parallel", "arbitrary"), vmem_limit_bytes=VMEM_LIMIT),
        name="ffn",
    )(x, gain, w_gu, w_gu, w_down)


def _inproj_body(x_ref, g_ref, w_ref, z_ref):
    h = _rms_rows(x_ref[...], g_ref[...]).astype(BF16)
    z_ref[...] = _dot(h, w_ref[...]).astype(BF16)


def _inproj(x, gain, w_in, *, tm):
    n = x.shape[0]
    tn = IN_COLS // 2
    return pl.pallas_call(
        _inproj_body,
        grid=(2, n // tm),
        in_specs=[
            pl.BlockSpec((tm, D_MODEL), lambda j, i: (i, 0)),
            pl.BlockSpec((1, D_MODEL), lambda j, i: (0, 0)),
            pl.BlockSpec((D_MODEL, tn), lambda j, i: (0, j)),
        ],
        out_specs=pl.BlockSpec((tm, tn), lambda j, i: (i, j)),
        out_shape=jax.ShapeDtypeStruct((n, IN_COLS), BF16),
        compiler_params=pltpu.CompilerParams(
            dimension_semantics=("arbitrary", "arbitrary"), vmem_limit_bytes=VMEM_LIMIT),
        name="inproj",
    )(x, gain, w_in)


def _attn_body(sinks_ref, zq_ref, zk_ref, zkp_ref, zv_ref, zvp_ref, pos_ref, posp_ref, invf_ref, qg_ref, kg_ref,
               o_ref, qs_ref, ks_ref, vs_ref, *, tq):
    t = pl.program_id(1)
    nb = tq // BLOCK
    lane = lax.broadcasted_iota(jnp.int32, (1, LANES), 1)
    first_half = (lane % HEAD_DIM) < (HEAD_DIM // 2)
    low_head = lane < HEAD_DIM
    gi = lax.broadcasted_iota(jnp.int32, (LANES, LANES), 0) // HEAD_DIM
    gj = lax.broadcasted_iota(jnp.int32, (LANES, LANES), 1) // HEAD_DIM
    gsum = jnp.where(gi == gj, 1.0, 0.0).astype(BF16)

    def head_norm(v, gain):
        hi, mid, lo = _split3(v * v)
        ss = _dot(hi, gsum) + _dot(mid, gsum) + _dot(lo, gsum)
        return v * lax.rsqrt(ss * (1.0 / HEAD_DIM) + EPS) * gain

    def rope(v, pos):
        ang = pos * invf_ref[...]
        cosf = jnp.cos(ang)
        sinf = jnp.sin(ang)
        rot = jnp.where(first_half, pltpu.roll(v, LANES - HEAD_DIM // 2, 1), pltpu.roll(v, HEAD_DIM // 2, 1))
        return v * cosf + rot * jnp.where(first_half, -sinf, sinf)

    pos = pos_ref[...]
    for c in range(ATTN_WIDTH // LANES):
        sl = slice(c * LANES, (c + 1) * LANES)
        q = rope(head_norm(zq_ref[:, sl].astype(F32), qg_ref[...]), pos)
        qs_ref[:, sl] = (q * (HEAD_DIM ** -0.5)).astype(BF16)

    def put_kv(dst_ref, rows, v):
        swapped = pltpu.roll(v, HEAD_DIM, 1)
        dst_ref[0, rows, :] = jnp.where(low_head, v, 0.0).astype(BF16)
        dst_ref[1, rows, :] = jnp.where(low_head, 0.0, swapped).astype(BF16)
        dst_ref[2, rows, :] = jnp.where(low_head, swapped, 0.0).astype(BF16)
        dst_ref[3, rows, :] = jnp.where(low_head, 0.0, v).astype(BF16)

    put_kv(ks_ref, slice(0, BLOCK), rope(head_norm(zkp_ref[...].astype(F32), kg_ref[...]), posp_ref[...]))
    put_kv(ks_ref, slice(BLOCK, BLOCK + tq), rope(head_norm(zk_ref[...].astype(F32), kg_ref[...]), pos))
    put_kv(vs_ref, slice(0, BLOCK), zvp_ref[...].astype(F32))
    put_kv(vs_ref, slice(BLOCK, BLOCK + tq), zv_ref[...].astype(F32))

    rows_q = PAIRS_PER_GROUP * BLOCK
    qi = lax.broadcasted_iota(jnp.int32, (rows_q, 2 * BLOCK), 0) % BLOCK
    kj = lax.broadcasted_iota(jnp.int32, (rows_q, 2 * BLOCK), 1)
    band = (kj <= qi + BLOCK) & (kj > qi + BLOCK - WINDOW)

    for n in range(nb):
        valid = band & ((kj >= BLOCK) | (t * nb + n > 0))
        qrows = slice(n * BLOCK, (n + 1) * BLOCK)
        krows = slice(n * BLOCK, n * BLOCK + 2 * BLOCK)
        for g in range(N_KV_HEADS):
            qst = jnp.concatenate(
                [qs_ref[qrows, (g * PAIRS_PER_GROUP + p) * LANES:(g * PAIRS_PER_GROUP + p + 1) * LANES]
                 for p in range(PAIRS_PER_GROUP)], axis=0)
            outs = None
            inv_den = []
            for half in range(2):
                sink = jnp.concatenate(
                    [jnp.full((BLOCK, 1), sinks_ref[g * GQA_GROUP + 2 * p + half], F32)
                     for p in range(PAIRS_PER_GROUP)], axis=0)
                s = _dot_nt(qst, ks_ref[2 * g + half, krows, :])
                s = jnp.where(valid, s, -jnp.inf)
                m = jnp.maximum(jnp.max(s, axis=-1, keepdims=True), sink)
                p_ = jnp.exp(s - m)
                den = jnp.sum(p_, axis=-1, keepdims=True) + jnp.exp(sink - m)
                inv_den.append(1.0 / den)
                o = _dot(p_.astype(BF16), vs_ref[2 * g + half, krows, :])
                outs = o if outs is None else outs + o
            outs = outs * jnp.where(low_head, inv_den[0], inv_den[1])
            for p in range(PAIRS_PER_GROUP):
                c = g * PAIRS_PER_GROUP + p
                o_ref[qrows, c * LANES:(c + 1) * LANES] = outs[p * BLOCK:(p + 1) * BLOCK].astype(BF16)


def _attn(z, pos_f, inv_freq, q_gain, k_gain, sinks, *, batch, seq, tq):
    n = batch * seq
    nt = seq // tq
    nb = tq // BLOCK
    bps = seq // BLOCK

    def cur(col):
        return lambda b, t: (b * nt + t, col)

    def prev(col):
        return lambda b, t: (b * bps + jnp.maximum(t * nb - 1, 0), col)

    return pl.pallas_call(
        functools.partial(_attn_body, tq=tq),
        grid=(batch, nt),
        in_specs=[
            pl.BlockSpec(memory_space=pltpu.SMEM),
            pl.BlockSpec((tq, ATTN_WIDTH), cur(Z_AQ)),
            pl.BlockSpec((tq, KV_WIDTH), cur(Z_AK)),
            pl.BlockSpec((BLOCK, KV_WIDTH), prev(Z_AK)),
            pl.BlockSpec((tq, KV_WIDTH), cur(Z_AV)),
            pl.BlockSpec((BLOCK, KV_WIDTH), prev(Z_AV)),
            pl.BlockSpec((tq, 1), cur(0)),
            pl.BlockSpec((BLOCK, 1), prev(0)),
            pl.BlockSpec((1, LANES), lambda b, t: (0, 0)),
            pl.BlockSpec((1, LANES), lambda b, t: (0, 0)),
            pl.BlockSpec((1, LANES), lambda b, t: (0, 0)),
        ],
        out_specs=pl.BlockSpec((tq, ATTN_WIDTH), cur(0)),
        out_shape=jax.ShapeDtypeStruct((n, ATTN_WIDTH), BF16),
        scratch_shapes=[
            pltpu.VMEM((tq, ATTN_WIDTH), BF16),
            pltpu.VMEM((4, tq + BLOCK, LANES), BF16),
            pltpu.VMEM((4, tq + BLOCK, LANES), BF16),
        ],
        compiler_params=pltpu.CompilerParams(
            dimension_semantics=("parallel", "arbitrary"), vmem_limit_bytes=VMEM_LIMIT),
        name="attn",
    )(sinks, z, z, z, z, z, pos_f, pos_f, inv_freq, q_gain, k_gain)


def _hgrn_body(zq_ref, zf_ref, zi_ref, zo_ref, lbt_ref, gain_ref, o_ref, state_ref, *, layer, tile):
    @pl.when(pl.program_id(1) == 0)
    def _():
        state_ref[...] = jnp.zeros_like(state_ref)

    lbt = lbt_ref[...]
    e = jnp.exp(lbt - jnp.max(lbt, axis=0, keepdims=True))
    sm = e / jnp.sum(e, axis=0, keepdims=True)
    lb = jnp.sum(sm[:layer + 1], axis=0, keepdims=True)

    ri = lax.broadcasted_iota(jnp.int32, (CHUNK, CHUNK), 0)
    ci = lax.broadcasted_iota(jnp.int32, (CHUNK, CHUNK), 1)
    causal = ri >= ci
    tri = jnp.where(causal, 1.0, 0.0).astype(BF16)
    gain = gain_ref[...]

    def chunk(c, carry):
        rows = pl.ds(pl.multiple_of(c * CHUNK, CHUNK), CHUNK)
        f = lb + (1.0 - lb) * jax.nn.sigmoid(zf_ref[rows, :].astype(F32))
        kk = 1.0 - f
        hi, mid, lo = _split3(jnp.log(f))
        b = _dot(tri, hi) + _dot(tri, mid) + _dot(tri, lo)
        for h in range(HG_HEADS):
            sl = slice(h * HG_HEAD_DIM, (h + 1) * HG_HEAD_DIM)
            bh = b[:, sl]
            kh = kk[:, sl]
            b_mid = bh[CHUNK // 2:CHUNK // 2 + 1]
            b_last = bh[CHUNK - 1:CHUNK]
            qh = jax.nn.silu(zq_ref[rows, sl].astype(F32))
            vh = zi_ref[rows, sl]
            att = _dot_nt((qh * jnp.exp(bh - b_mid)).astype(BF16), (kh * jnp.exp(b_mid - bh)).astype(BF16))
            o = _dot(jnp.where(causal, att, 0.0).astype(BF16), vh)
            st = state_ref[h]
            o = o + _dot_nt((qh * jnp.exp(bh)).astype(BF16), st.astype(BF16))
            upd = _dot_tn(vh, (kh * jnp.exp(b_last - bh)).astype(BF16))
            state_ref[h] = st * jnp.exp(b_last) + upd
            y = _rms_rows(o, gain) * jax.nn.silu(zo_ref[rows, sl].astype(F32))
            o_ref[rows, sl] = y.astype(BF16)
        return carry

    lax.fori_loop(0, tile // CHUNK, chunk, 0)


def _hgrn(z, lb_table, gain, *, layer, batch, seq, tile):
    n = batch * seq
    nt = seq // tile

    def cur(col):
        return lambda b, t: (b * nt + t, col)

    return pl.pallas_call(
        functools.partial(_hgrn_body, layer=layer, tile=tile),
        grid=(batch, nt),
        in_specs=[
            pl.BlockSpec((tile, HG_WIDTH), cur(Z_GQ)),
            pl.BlockSpec((tile, HG_WIDTH), cur(Z_GF)),
            pl.BlockSpec((tile, HG_WIDTH), cur(Z_GI)),
            pl.BlockSpec((tile, HG_WIDTH), cur(Z_GO)),
            pl.BlockSpec(lb_table.shape, lambda b, t: (0, 0)),
            pl.BlockSpec((1, HG_HEAD_DIM), lambda b, t: (0, 0)),
        ],
        out_specs=pl.BlockSpec((tile, HG_WIDTH), cur(0)),
        out_shape=jax.ShapeDtypeStruct((n, HG_WIDTH), BF16),
        scratch_shapes=[pltpu.VMEM((HG_HEADS, HG_HEAD_DIM, HG_HEAD_DIM), F32)],
        compiler_params=pltpu.CompilerParams(
            dimension_semantics=("parallel", "arbitrary"), vmem_limit_bytes=VMEM_LIMIT),
        name="hgrn",
    )(z, z, z, z, lb_table, gain)


def _merge_body(x_ref, ya_ref, yh_ref, br_ref, wa_ref, wr_ref, wo_ref, o_ref):
    a = _dot(ya_ref[...], wa_ref[...])
    r = _dot(yh_ref[...], wr_ref[...])
    ga = jax.nn.sigmoid(br_ref[:, :D_MODEL].astype(F32))
    gr = jax.nn.sigmoid(br_ref[:, D_MODEL:].astype(F32))
    merged = (ga * a + gr * r).astype(BF16)
    o_ref[...] = x_ref[...] + _dot(merged, wo_ref[...])


def _merge(x, ya, yh, z, w_a, w_r, w_o, *, tm):
    n = x.shape[0]
    whole = lambda i: (0, 0)
    return pl.pallas_call(
        _merge_body,
        grid=(n // tm,),
        in_specs=[
            pl.BlockSpec((tm, D_MODEL), lambda i: (i, 0)),
            pl.BlockSpec((tm, ATTN_WIDTH), lambda i: (i, 0)),
            pl.BlockSpec((tm, HG_WIDTH), lambda i: (i, 0)),
            pl.BlockSpec((tm, 2 * D_MODEL), lambda i: (i, Z_BR)),
            pl.BlockSpec((ATTN_WIDTH, D_MODEL), whole),
            pl.BlockSpec((HG_WIDTH, D_MODEL), whole),
            pl.BlockSpec((D_MODEL, D_MODEL), whole),
        ],
        out_specs=pl.BlockSpec((tm, D_MODEL), lambda i: (i, 0)),
        out_shape=jax.ShapeDtypeStruct((n, D_MODEL), F32),
        compiler_params=pltpu.CompilerParams(
            dimension_semantics=("parallel",), vmem_limit_bytes=VMEM_LIMIT),
        name="merge",
    )(x, ya, yh, z, w_a, w_r, w_o)


def _tile(n, want):
    t = min(want, n)
    assert n % t == 0, (n, t)
    return t


def kernel(x, positions, lb_table, ffn1_norm, ffn1_w_gu, ffn1_w_down, mix_norm, w_in, q_norm, k_norm, sinks,
           hg_out_norm, w_attn_branch, w_hg_branch, w_out, ffn2_norm, ffn2_w_gu, ffn2_w_down):
    batch, seq = x.shape[0], x.shape[1]
    n = batch * seq
    depth = w_in.shape[0]
    assert seq % BLOCK == 0 and seq % CHUNK == 0

    half = HEAD_DIM // 2
    inv_freq = ROPE_THETA ** (-jnp.arange(half, dtype=F32) * 2.0 / HEAD_DIM)
    inv_freq = jnp.tile(inv_freq, LANES // half)[None, :]
    pos_f = positions.astype(F32).reshape(n, 1)
    n_attn = ATTN_WIDTH + 2 * KV_WIDTH

    xf = x.reshape(n, D_MODEL)
    for l in range(depth):
        w_in_l = jnp.concatenate([w_in[l][:, n_attn:], w_in[l][:, :n_attn]], axis=1).astype(BF16)
        xf = _ffn(xf, ffn1_norm[l][None, :], ffn1_w_gu[l].astype(BF16), ffn1_w_down[l].astype(BF16),
                  tm=_tile(n, 512), tf=512)
        z = _inproj(xf, mix_norm[l][None, :], w_in_l, tm=_tile(n, 256))
        ya = _attn(z, pos_f, inv_freq, jnp.tile(q_norm[l], 2)[None, :], jnp.tile(k_norm[l], 2)[None, :], sinks[l],
                   batch=batch, seq=seq, tq=_tile(seq, 512))
        yh = _hgrn(z, lb_table, hg_out_norm[l][None, :], layer=l, batch=batch, seq=seq, tile=_tile(seq, 512))
        xf = _merge(xf, ya, yh, z, w_attn_branch[l].astype(BF16), w_hg_branch[l].astype(BF16),
                    w_out[l].astype(BF16), tm=_tile(n, 256))
        xf = _ffn(xf, ffn2_norm[l][None, :], ffn2_w_gu[l].astype(BF16), ffn2_w_down[l].astype(BF16),
                  tm=_tile(n, 512), tf=512)
    return xf.reshape(batch, seq, D_MODEL)
```

```python
import functools

import jax
import jax.numpy as jnp
from jax import lax
from jax.experimental import pallas as pl
from jax.experimental.pallas import tpu as pltpu

D_MODEL = 2048
HEAD_DIM = 64
N_Q_HEADS = 16
N_KV_HEADS = 2
GQA_GROUP = N_Q_HEADS // N_KV_HEADS
ATTN_WIDTH = N_Q_HEADS * HEAD_DIM
KV_WIDTH = N_KV_HEADS * HEAD_DIM
WINDOW = 128
BLOCK = 128
ROPE_THETA = 10000.0
HG_HEAD_DIM = 128
HG_HEADS = 8
HG_WIDTH = HG_HEADS * HG_HEAD_DIM
CHUNK = 64
D_FF = 5632
EPS = 1e-6
IN_COLS = ATTN_WIDTH + 2 * KV_WIDTH + 4 * HG_WIDTH + 2 * D_MODEL

LANES = 128
PAIRS_PER_GROUP = GQA_GROUP // 2
VMEM_LIMIT = 56 * 1024 * 1024
LOG2E = 1.4426950408889634

Z_GQ, Z_GF, Z_GI, Z_GO = 0, 1, 2, 3
Z_BR = 1
Z_AQ = (4 * HG_WIDTH + 2 * D_MODEL) // ATTN_WIDTH
Z_AK = (4 * HG_WIDTH + 2 * D_MODEL + ATTN_WIDTH) // KV_WIDTH
Z_AV = Z_AK + 1

F32 = jnp.float32
BF16 = jnp.bfloat16


def _dot(a, b):
    return jnp.dot(a, b, preferred_element_type=F32)


def _dot_nt(a, b):
    return lax.dot_general(a, b, (((1,), (1,)), ((), ())), preferred_element_type=F32)


def _dot_tn(a, b):
    return lax.dot_general(a, b, (((0,), (0,)), ((), ())), preferred_element_type=F32)


def _rms_rows(x, gain):
    ms = jnp.mean(x * x, axis=-1, keepdims=True)
    return x * lax.rsqrt(ms + EPS) * gain


def _split3(v):
    hi = v.astype(BF16)
    r1 = v - hi.astype(F32)
    mid = r1.astype(BF16)
    lo = (r1 - mid.astype(F32)).astype(BF16)
    return hi, mid, lo


def _ffn_body(x_ref, g_ref, wg_ref, wu_ref, wd_ref, o_ref, h_ref):
    j = pl.program_id(1)

    @pl.when(j == 0)
    def _():
        x = x_ref[...]
        h_ref[...] = _rms_rows(x, g_ref[...]).astype(BF16)
        o_ref[...] = x

    h = h_ref[...]
    gate = _dot(h, wg_ref[...])
    up = _dot(h, wu_ref[...])
    act = (gate * jax.nn.sigmoid(gate)) * (up * 0.5)
    o_ref[...] += _dot(act.astype(BF16), wd_ref[...])


def _ffn(x, gain, w_gu, w_down, *, tm, tf):
    n = x.shape[0]
    nf = D_FF // tf
    return pl.pallas_call(
        _ffn_body,
        grid=(n // tm, nf),
        in_specs=[
            pl.BlockSpec((tm, D_MODEL), lambda i, j: (i, 0)),
            pl.BlockSpec((1, D_MODEL), lambda i, j: (0, 0)),
            pl.BlockSpec((D_MODEL, tf), lambda i, j: (0, j)),
            pl.BlockSpec((D_MODEL, tf), lambda i, j: (0, j + nf)),
            pl.BlockSpec((tf, D_MODEL), lambda i, j: (j, 0)),
        ],
        out_specs=pl.BlockSpec((tm, D_MODEL), lambda i, j: (i, 0)),
        out_shape=jax.ShapeDtypeStruct((n, D_MODEL), F32),
        scratch_shapes=[pltpu.VMEM((tm, D_MODEL), BF16)],
        compiler_params=pltpu.CompilerParams(
            dimension_semantics=("parallel", "arbitrary"), vmem_limit_bytes=VMEM_LIMIT),
        name="ffn",
    )(x, gain, w_gu, w_gu, w_down)


def _inproj_body(x_ref, g_ref, w_ref, z_ref):
    h = _rms_rows(x_ref[...], g_ref[...]).astype(BF16)
    z_ref[...] = _dot(h, w_ref[...]).astype(BF16)


def _inproj(x, gain, w_in, *, tm):
    n = x.shape[0]
    tn = IN_COLS // 2
    return pl.pallas_call(
        _inproj_body,
        grid=(2, n // tm),
        in_specs=[
            pl.BlockSpec((tm, D_MODEL), lambda j, i: (i, 0)),
            pl.BlockSpec((1, D_MODEL), lambda j, i: (0, 0)),
            pl.BlockSpec((D_MODEL, tn), lambda j, i: (0, j), pipeline_mode=pl.Buffered(1)),
        ],
        out_specs=pl.BlockSpec((tm, tn), lambda j, i: (i, j)),
        out_shape=jax.ShapeDtypeStruct((n, IN_COLS), BF16),
        compiler_params=pltpu.CompilerParams(
            dimension_semantics=("arbitrary", "arbitrary"), vmem_limit_bytes=VMEM_LIMIT),
        name="inproj",
    )(x, gain, w_in)


def _attn_body(sinks_ref, zq_ref, zk_ref, zkp_ref, zv_ref, zvp_ref, pos_ref, posp_ref, invf_ref, qg_ref, kg_ref,
               o_ref, qs_ref, ks_ref, vs_ref, *, tq):
    t = pl.program_id(1)
    nb = tq // BLOCK
    lane = lax.broadcasted_iota(jnp.int32, (1, LANES), 1)
    first_half = (lane % HEAD_DIM) < (HEAD_DIM // 2)
    low_head = lane < HEAD_DIM
    gi = lax.broadcasted_iota(jnp.int32, (LANES, LANES), 0) // HEAD_DIM
    gj = lax.broadcasted_iota(jnp.int32, (LANES, LANES), 1) // HEAD_DIM
    gsum = jnp.where(gi == gj, 1.0, 0.0).astype(BF16)

    def head_norm(v, gain):
        sq = v * v
        hi = sq.astype(BF16)
        lo = (sq - hi.astype(F32)).astype(BF16)
        ss = _dot(hi, gsum) + _dot(lo, gsum)
        return v * lax.rsqrt(ss * (1.0 / HEAD_DIM) + EPS) * gain

    def rope_tables(pos):
        ang = pos * invf_ref[...]
        sinf = jnp.sin(ang)
        return jnp.cos(ang), jnp.where(first_half, -sinf, sinf)

    def rope(v, tables):
        cosf, sin_signed = tables
        rot = jnp.where(first_half, pltpu.roll(v, LANES - HEAD_DIM // 2, 1), pltpu.roll(v, HEAD_DIM // 2, 1))
        return v * cosf + rot * sin_signed

    tab = rope_tables(pos_ref[...])
    tab_prev = rope_tables(posp_ref[...])

    for c in range(ATTN_WIDTH // LANES):
        sl = slice(c * LANES, (c + 1) * LANES)
        q = rope(head_norm(zq_ref[:, sl].astype(F32), qg_ref[...]), tab)
        qs_ref[:, sl] = (q * (HEAD_DIM ** -0.5 * LOG2E)).astype(BF16)

    def put_kv(dst_ref, rows, v, pad):
        swapped = pltpu.roll(v, HEAD_DIM, 1)
        dst_ref[0, rows, :] = jnp.where(low_head, v, pad).astype(BF16)
        dst_ref[1, rows, :] = jnp.where(low_head, pad, swapped).astype(BF16)
        dst_ref[2, rows, :] = jnp.where(low_head, swapped, pad).astype(BF16)
        dst_ref[3, rows, :] = jnp.where(low_head, pad, v).astype(BF16)

    put_kv(ks_ref, slice(0, BLOCK), rope(head_norm(zkp_ref[...].astype(F32), kg_ref[...]), tab_prev), 0.0)
    put_kv(ks_ref, slice(BLOCK, BLOCK + tq), rope(head_norm(zk_ref[...].astype(F32), kg_ref[...]), tab), 0.0)
    put_kv(vs_ref, slice(0, BLOCK), zvp_ref[...].astype(F32), 1.0)
    put_kv(vs_ref, slice(BLOCK, BLOCK + tq), zv_ref[...].astype(F32), 1.0)

    rows_q = PAIRS_PER_GROUP * BLOCK
    qi = lax.broadcasted_iota(jnp.int32, (rows_q, BLOCK), 0) % BLOCK
    kj = lax.broadcasted_iota(jnp.int32, (rows_q, BLOCK), 1)
    band_prev = kj > qi + BLOCK - WINDOW
    band_cur = kj <= qi
    sink_fill = {}
    for g in range(N_KV_HEADS):
        for half in range(2):
            sink_rows = jnp.concatenate(
                [jnp.full((BLOCK, LANES), sinks_ref[g * GQA_GROUP + 2 * p + half] * LOG2E, F32)
                 for p in range(PAIRS_PER_GROUP)], axis=0)
            sink_fill[g, half] = jnp.where(kj == 0, sink_rows, -jnp.inf)
    first_key = lax.broadcasted_iota(jnp.int32, (2 * BLOCK, LANES), 0) == 0
    drop_value = (first_key & low_head, first_key & jnp.logical_not(low_head))

    for n in range(nb):
        valid_prev = band_prev & (t * nb + n > 0)
        qrows = slice(n * BLOCK, (n + 1) * BLOCK)
        krows = slice(n * BLOCK, n * BLOCK + 2 * BLOCK)
        for g in range(N_KV_HEADS):
            qst = jnp.concatenate(
                [qs_ref[qrows, (g * PAIRS_PER_GROUP + p) * LANES:(g * PAIRS_PER_GROUP + p + 1) * LANES]
                 for p in range(PAIRS_PER_GROUP)], axis=0)
            o = []
            for half in range(2):
                s = _dot_nt(qst, ks_ref[2 * g + half, krows, :])
                s_prev = jnp.where(valid_prev, s[:, :BLOCK], sink_fill[g, half])
                s_cur = jnp.where(band_cur, s[:, BLOCK:], -jnp.inf)
                m = jnp.max(jnp.maximum(s_prev, s_cur), axis=-1, keepdims=True)
                p_ = jnp.concatenate([jnp.exp2(s_prev - m), jnp.exp2(s_cur - m)], axis=1).astype(BF16)
                vwin = vs_ref[2 * g + half, krows, :]
                vwin = jnp.where(drop_value[half], jnp.zeros_like(vwin), vwin)
                o.append(_dot(p_, vwin))
            num = jnp.where(low_head, o[0], o[1])
            den = pltpu.roll(jnp.where(low_head, o[1], o[0]), HEAD_DIM, 1)
            outs = num / den
            for p in range(PAIRS_PER_GROUP):
                c = g * PAIRS_PER_GROUP + p
                o_ref[qrows, c * LANES:(c + 1) * LANES] = outs[p * BLOCK:(p + 1) * BLOCK].astype(BF16)


def _attn(z, pos_f, inv_freq, q_gain, k_gain, sinks, *, batch, seq, tq):
    n = batch * seq
    nt = seq // tq
    nb = tq // BLOCK
    bps = seq // BLOCK

    def cur(col):
        return lambda b, t: (b * nt + t, col)

    def prev(col):
        return lambda b, t: (b * bps + jnp.maximum(t * nb - 1, 0), col)

    return pl.pallas_call(
        functools.partial(_attn_body, tq=tq),
        grid=(batch, nt),
        in_specs=[
            pl.BlockSpec(memory_space=pltpu.SMEM),
            pl.BlockSpec((tq, ATTN_WIDTH), cur(Z_AQ)),
            pl.BlockSpec((tq, KV_WIDTH), cur(Z_AK)),
            pl.BlockSpec((BLOCK, KV_WIDTH), prev(Z_AK)),
            pl.BlockSpec((tq, KV_WIDTH), cur(Z_AV)),
            pl.BlockSpec((BLOCK, KV_WIDTH), prev(Z_AV)),
            pl.BlockSpec((tq, 1), cur(0)),
            pl.BlockSpec((BLOCK, 1), prev(0)),
            pl.BlockSpec((1, LANES), lambda b, t: (0, 0)),
            pl.BlockSpec((1, LANES), lambda b, t: (0, 0)),
            pl.BlockSpec((1, LANES), lambda b, t: (0, 0)),
        ],
        out_specs=pl.BlockSpec((tq, ATTN_WIDTH), cur(0)),
        out_shape=jax.ShapeDtypeStruct((n, ATTN_WIDTH), BF16),
        scratch_shapes=[
            pltpu.VMEM((tq, ATTN_WIDTH), BF16),
            pltpu.VMEM((4, tq + BLOCK, LANES), BF16),
            pltpu.VMEM((4, tq + BLOCK, LANES), BF16),
        ],
        compiler_params=pltpu.CompilerParams(
            dimension_semantics=("parallel", "arbitrary"), vmem_limit_bytes=VMEM_LIMIT),
        name="attn",
    )(sinks, z, z, z, z, z, pos_f, pos_f, inv_freq, q_gain, k_gain)


def _hgrn_body(zq_ref, zf_ref, zi_ref, zo_ref, lbt_ref, gain_ref, o_ref, state_ref, *, layer, tile):
    @pl.when(pl.program_id(1) == 0)
    def _():
        state_ref[...] = jnp.zeros_like(state_ref)

    lbt = lbt_ref[...]
    e = jnp.exp(lbt - jnp.max(lbt, axis=0, keepdims=True))
    sm = e / jnp.sum(e, axis=0, keepdims=True)
    lb = jnp.sum(sm[:layer + 1], axis=0, keepdims=True)

    ri = lax.broadcasted_iota(jnp.int32, (CHUNK, CHUNK), 0)
    ci = lax.broadcasted_iota(jnp.int32, (CHUNK, CHUNK), 1)
    causal = ri >= ci
    tri = jnp.where(causal, 1.0, 0.0).astype(BF16)
    gain = gain_ref[...]

    def chunk(c, carry):
        rows = pl.ds(pl.multiple_of(c * CHUNK, CHUNK), CHUNK)
        f = lb + (1.0 - lb) * jax.nn.sigmoid(zf_ref[rows, :].astype(F32))
        kk = 1.0 - f
        hi, mid, lo = _split3(jnp.log(f))
        b = _dot(tri, hi) + _dot(tri, mid) + _dot(tri, lo)
        b_mid = b[CHUNK // 2:CHUNK // 2 + 1]
        b_last = b[CHUNK - 1:CHUNK]
        qh = jax.nn.silu(zq_ref[rows, :].astype(F32))
        q_mid = (qh * jnp.exp(b - b_mid)).astype(BF16)
        k_mid = (kk * jnp.exp(b_mid - b)).astype(BF16)
        q_in = (qh * jnp.exp(b)).astype(BF16)
        k_out = (kk * jnp.exp(b_last - b)).astype(BF16)
        decay = jnp.exp(b_last)
        heads = [slice(h * HG_HEAD_DIM, (h + 1) * HG_HEAD_DIM) for h in range(HG_HEADS)]
        att = [_dot_nt(q_mid[:, sl], k_mid[:, sl]) for sl in heads]
        st = [state_ref[h] for h in range(HG_HEADS)]
        o_inter = [_dot_nt(q_in[:, sl], st[h].astype(BF16)) for h, sl in enumerate(heads)]
        upd = [_dot_tn(zi_ref[rows, sl], k_out[:, sl]) for sl in heads]
        o = [o_inter[h] + _dot(jnp.where(causal, att[h], 0.0).astype(BF16), zi_ref[rows, sl])
             for h, sl in enumerate(heads)]
        for h, sl in enumerate(heads):
            state_ref[h] = st[h] * decay[:, sl] + upd[h]
            y = _rms_rows(o[h], gain) * jax.nn.silu(zo_ref[rows, sl].astype(F32))
            o_ref[rows, sl] = y.astype(BF16)
        return carry

    lax.fori_loop(0, tile // CHUNK, chunk, 0, unroll=True)


def _hgrn(z, lb_table, gain, *, layer, batch, seq, tile):
    n = batch * seq
    nt = seq // tile

    def cur(col):
        return lambda b, t: (b * nt + t, col)

    return pl.pallas_call(
        functools.partial(_hgrn_body, layer=layer, tile=tile),
        grid=(batch, nt),
        in_specs=[
            pl.BlockSpec((tile, HG_WIDTH), cur(Z_GQ)),
            pl.BlockSpec((tile, HG_WIDTH), cur(Z_GF)),
            pl.BlockSpec((tile, HG_WIDTH), cur(Z_GI)),
            pl.BlockSpec((tile, HG_WIDTH), cur(Z_GO)),
            pl.BlockSpec(lb_table.shape, lambda b, t: (0, 0)),
            pl.BlockSpec((1, HG_HEAD_DIM), lambda b, t: (0, 0)),
        ],
        out_specs=pl.BlockSpec((tile, HG_WIDTH), cur(0)),
        out_shape=jax.ShapeDtypeStruct((n, HG_WIDTH), BF16),
        scratch_shapes=[pltpu.VMEM((HG_HEADS, HG_HEAD_DIM, HG_HEAD_DIM), F32)],
        compiler_params=pltpu.CompilerParams(
            dimension_semantics=("parallel", "arbitrary"), vmem_limit_bytes=VMEM_LIMIT),
        name="hgrn",
    )(z, z, z, z, lb_table, gain)


def _merge_body(x_ref, ya_ref, yh_ref, br_ref, wa_ref, wr_ref, wo_ref, o_ref):
    a = _dot(ya_ref[...], wa_ref[...])
    r = _dot(yh_ref[...], wr_ref[...])
    ga = jax.nn.sigmoid(br_ref[:, :D_MODEL].astype(F32))
    gr = jax.nn.sigmoid(br_ref[:, D_MODEL:].astype(F32))
    merged = (ga * a + gr * r).astype(BF16)
    o_ref[...] = x_ref[...] + _dot(merged, wo_ref[...])


def _merge(x, ya, yh, z, w_a, w_r, w_o, *, tm):
    n = x.shape[0]
    whole = lambda i: (0, 0)
    return pl.pallas_call(
        _merge_body,
        grid=(n // tm,),
        in_specs=[
            pl.BlockSpec((tm, D_MODEL), lambda i: (i, 0)),
            pl.BlockSpec((tm, ATTN_WIDTH), lambda i: (i, 0)),
            pl.BlockSpec((tm, HG_WIDTH), lambda i: (i, 0)),
            pl.BlockSpec((tm, 2 * D_MODEL), lambda i: (i, Z_BR)),
            pl.BlockSpec((ATTN_WIDTH, D_MODEL), whole),
            pl.BlockSpec((HG_WIDTH, D_MODEL), whole),
            pl.BlockSpec((D_MODEL, D_MODEL), whole),
        ],
        out_specs=pl.BlockSpec((tm, D_MODEL), lambda i: (i, 0)),
        out_shape=jax.ShapeDtypeStruct((n, D_MODEL), F32),
        compiler_params=pltpu.CompilerParams(
            dimension_semantics=("parallel",), vmem_limit_bytes=VMEM_LIMIT),
        name="merge",
    )(x, ya, yh, z, w_a, w_r, w_o)


def _tile(n, want):
    t = min(want, n)
    assert n % t == 0, (n, t)
    return t


def kernel(x, positions, lb_table, ffn1_norm, ffn1_w_gu, ffn1_w_down, mix_norm, w_in, q_norm, k_norm, sinks,
           hg_out_norm, w_attn_branch, w_hg_branch, w_out, ffn2_norm, ffn2_w_gu, ffn2_w_down):
    batch, seq = x.shape[0], x.shape[1]
    n = batch * seq
    depth = w_in.shape[0]
    assert seq % BLOCK == 0 and seq % CHUNK == 0

    half = HEAD_DIM // 2
    inv_freq = ROPE_THETA ** (-jnp.arange(half, dtype=F32) * 2.0 / HEAD_DIM)
    inv_freq = jnp.tile(inv_freq, LANES // half)[None, :]
    pos_f = positions.astype(F32).reshape(n, 1)
    n_attn = ATTN_WIDTH + 2 * KV_WIDTH

    xf = x.reshape(n, D_MODEL)
    for l in range(depth):
        w_in_l = jnp.concatenate([w_in[l][:, n_attn:], w_in[l][:, :n_attn]], axis=1).astype(BF16)
        xf = _ffn(xf, ffn1_norm[l][None, :], ffn1_w_gu[l].astype(BF16), ffn1_w_down[l].astype(BF16),
                  tm=_tile(n, 512), tf=512)
        z = _inproj(xf, mix_norm[l][None, :], w_in_l, tm=_tile(n, 512))
        ya = _attn(z, pos_f, inv_freq, jnp.tile(q_norm[l], 2)[None, :], jnp.tile(k_norm[l], 2)[None, :], sinks[l],
                   batch=batch, seq=seq, tq=_tile(seq, 512))
        yh = _hgrn(z, lb_table, hg_out_norm[l][None, :], layer=l, batch=batch, seq=seq, tile=_tile(seq, 512))
        xf = _merge(xf, ya, yh, z, w_attn_branch[l].astype(BF16), w_hg_branch[l].astype(BF16),
                    w_out[l].astype(BF16), tm=_tile(n, 256))
        xf = _ffn(xf, ffn2_norm[l][None, :], ffn2_w_gu[l].astype(BF16), ffn2_w_down[l].astype(BF16),
                  tm=_tile(n, 512), tf=512)
    return xf.reshape(batch, seq, D_MODEL)
```

```python
import functools

import jax
import jax.numpy as jnp
from jax import lax
from jax.experimental import pallas as pl
from jax.experimental.pallas import tpu as pltpu

D_MODEL = 2048
HEAD_DIM = 64
N_Q_HEADS = 16
N_KV_HEADS = 2
GQA_GROUP = N_Q_HEADS // N_KV_HEADS
ATTN_WIDTH = N_Q_HEADS * HEAD_DIM
KV_WIDTH = N_KV_HEADS * HEAD_DIM
WINDOW = 128
BLOCK = 128
ROPE_THETA = 10000.0
HG_HEAD_DIM = 128
HG_HEADS = 8
HG_WIDTH = HG_HEADS * HG_HEAD_DIM
CHUNK = 64
D_FF = 5632
EPS = 1e-6
IN_COLS = ATTN_WIDTH + 2 * KV_WIDTH + 4 * HG_WIDTH + 2 * D_MODEL

LANES = 128
PAIRS_PER_GROUP = GQA_GROUP // 2
VMEM_LIMIT = 56 * 1024 * 1024
LOG2E = 1.4426950408889634

Z_GQ, Z_GF, Z_GI, Z_GO = 0, 1, 2, 3
Z_BR = 1
Z_AQ = (4 * HG_WIDTH + 2 * D_MODEL) // ATTN_WIDTH
Z_AK = (4 * HG_WIDTH + 2 * D_MODEL + ATTN_WIDTH) // KV_WIDTH
Z_AV = Z_AK + 1

F32 = jnp.float32
BF16 = jnp.bfloat16


def _dot(a, b):
    return jnp.dot(a, b, preferred_element_type=F32)


def _dot_nt(a, b):
    return lax.dot_general(a, b, (((1,), (1,)), ((), ())), preferred_element_type=F32)


def _dot_tn(a, b):
    return lax.dot_general(a, b, (((0,), (0,)), ((), ())), preferred_element_type=F32)


def _rms_rows(x, gain):
    ms = jnp.mean(x * x, axis=-1, keepdims=True)
    return x * lax.rsqrt(ms + EPS) * gain


def _split3(v):
    hi = v.astype(BF16)
    r1 = v - hi.astype(F32)
    mid = r1.astype(BF16)
    lo = (r1 - mid.astype(F32)).astype(BF16)
    return hi, mid, lo


def _ffn_body(x_hbm, g_ref, wg_ref, wu_ref, wd_ref, o_ref, h_ref, xbuf_ref, sem, *, tm):
    i = pl.program_id(0)
    j = pl.program_id(1)

    def x_copy(tile):
        rows = pl.ds(pl.multiple_of(tile * tm, tm), tm)
        return pltpu.make_async_copy(x_hbm.at[rows], xbuf_ref, sem)

    @pl.when((i == 0) & (j == 0))
    def _():
        x_copy(0).start()

    @pl.when(j == 0)
    def _():
        x_copy(i).wait()
        x = xbuf_ref[...]
        h_ref[...] = _rms_rows(x, g_ref[...]).astype(BF16)
        o_ref[...] = x

    @pl.when((j == 1) & (i + 1 < pl.num_programs(0)))
    def _():
        x_copy(i + 1).start()

    h = h_ref[...]
    gate = _dot(h, wg_ref[...])
    up = _dot(h, wu_ref[...])
    act = (gate * jax.nn.sigmoid(gate)) * (up * 0.5)
    o_ref[...] += _dot(act.astype(BF16), wd_ref[...])


def _ffn(x, gain, w_gu, w_down, *, tm, tf):
    n = x.shape[0]
    nf = D_FF // tf
    assert nf >= 2
    return pl.pallas_call(
        functools.partial(_ffn_body, tm=tm),
        grid=(n // tm, nf),
        in_specs=[
            pl.BlockSpec(memory_space=pl.ANY),
            pl.BlockSpec((1, D_MODEL), lambda i, j: (0, 0)),
            pl.BlockSpec((D_MODEL, tf), lambda i, j: (0, j)),
            pl.BlockSpec((D_MODEL, tf), lambda i, j: (0, j + nf)),
            pl.BlockSpec((tf, D_MODEL), lambda i, j: (j, 0)),
        ],
        out_specs=pl.BlockSpec((tm, D_MODEL), lambda i, j: (i, 0)),
        out_shape=jax.ShapeDtypeStruct((n, D_MODEL), F32),
        scratch_shapes=[pltpu.VMEM((tm, D_MODEL), BF16), pltpu.VMEM((tm, D_MODEL), F32),
                        pltpu.SemaphoreType.DMA(())],
        compiler_params=pltpu.CompilerParams(
            dimension_semantics=("arbitrary", "arbitrary"), vmem_limit_bytes=VMEM_LIMIT),
        name="ffn",
    )(x, gain, w_gu, w_gu, w_down)


def _inproj_body(x_ref, g_ref, w_ref, z_ref):
    h = _rms_rows(x_ref[...], g_ref[...]).astype(BF16)
    z_ref[...] = _dot(h, w_ref[...]).astype(BF16)


def _inproj(x, gain, w_in, *, tm):
    n = x.shape[0]
    tn = IN_COLS // 2
    return pl.pallas_call(
        _inproj_body,
        grid=(2, n // tm),
        in_specs=[
            pl.BlockSpec((tm, D_MODEL), lambda j, i: (i, 0)),
            pl.BlockSpec((1, D_MODEL), lambda j, i: (0, 0)),
            pl.BlockSpec((D_MODEL, tn), lambda j, i: (0, j), pipeline_mode=pl.Buffered(1)),
        ],
        out_specs=pl.BlockSpec((tm, tn), lambda j, i: (i, j)),
        out_shape=jax.ShapeDtypeStruct((n, IN_COLS), BF16),
        compiler_params=pltpu.CompilerParams(
            dimension_semantics=("arbitrary", "arbitrary"), vmem_limit_bytes=VMEM_LIMIT),
        name="inproj",
    )(x, gain, w_in)


def _attn_body(sinks_ref, zq_ref, zk_ref, zkp_ref, zv_ref, zvp_ref, pos_ref, posp_ref, invf_ref, qg_ref, kg_ref,
               o_ref, qs_ref, ks_ref, vs_ref, *, tq):
    t = pl.program_id(1)
    nb = tq // BLOCK
    lane = lax.broadcasted_iota(jnp.int32, (1, LANES), 1)
    first_half = (lane % HEAD_DIM) < (HEAD_DIM // 2)
    low_head = lane < HEAD_DIM
    gi = lax.broadcasted_iota(jnp.int32, (LANES, LANES), 0) // HEAD_DIM
    gj = lax.broadcasted_iota(jnp.int32, (LANES, LANES), 1) // HEAD_DIM
    gsum = jnp.where(gi == gj, 1.0, 0.0).astype(BF16)

    def head_norm(v, gain):
        sq = v * v
        hi = sq.astype(BF16)
        lo = (sq - hi.astype(F32)).astype(BF16)
        ss = _dot(hi, gsum) + _dot(lo, gsum)
        return v * lax.rsqrt(ss * (1.0 / HEAD_DIM) + EPS) * gain

    def rope_tables(pos):
        ang = pos * invf_ref[...]
        sinf = jnp.sin(ang)
        return jnp.cos(ang), jnp.where(first_half, -sinf, sinf)

    def rope(v, tables):
        cosf, sin_signed = tables
        rot = jnp.where(first_half, pltpu.roll(v, LANES - HEAD_DIM // 2, 1), pltpu.roll(v, HEAD_DIM // 2, 1))
        return v * cosf + rot * sin_signed

    tab = rope_tables(pos_ref[...])
    tab_prev = rope_tables(posp_ref[...])

    for c in range(ATTN_WIDTH // LANES):
        sl = slice(c * LANES, (c + 1) * LANES)
        q = rope(head_norm(zq_ref[:, sl].astype(F32), qg_ref[...]), tab)
        qs_ref[:, sl] = (q * (HEAD_DIM ** -0.5 * LOG2E)).astype(BF16)

    def put_kv(dst_ref, rows, v, pad):
        swapped = pltpu.roll(v, HEAD_DIM, 1)
        dst_ref[0, rows, :] = jnp.where(low_head, v, pad).astype(BF16)
        dst_ref[1, rows, :] = jnp.where(low_head, pad, swapped).astype(BF16)
        dst_ref[2, rows, :] = jnp.where(low_head, swapped, pad).astype(BF16)
        dst_ref[3, rows, :] = jnp.where(low_head, pad, v).astype(BF16)

    put_kv(ks_ref, slice(0, BLOCK), rope(head_norm(zkp_ref[...].astype(F32), kg_ref[...]), tab_prev), 0.0)
    put_kv(ks_ref, slice(BLOCK, BLOCK + tq), rope(head_norm(zk_ref[...].astype(F32), kg_ref[...]), tab), 0.0)
    put_kv(vs_ref, slice(0, BLOCK), zvp_ref[...].astype(F32), 1.0)
    put_kv(vs_ref, slice(BLOCK, BLOCK + tq), zv_ref[...].astype(F32), 1.0)

    rows_q = PAIRS_PER_GROUP * BLOCK
    qi = lax.broadcasted_iota(jnp.int32, (rows_q, BLOCK), 0) % BLOCK
    kj = lax.broadcasted_iota(jnp.int32, (rows_q, BLOCK), 1)
    band_prev = kj > qi + BLOCK - WINDOW
    band_cur = kj <= qi
    sink_fill = {}
    for g in range(N_KV_HEADS):
        for half in range(2):
            sink_rows = jnp.concatenate(
                [jnp.full((BLOCK, LANES), sinks_ref[g * GQA_GROUP + 2 * p + half] * LOG2E, F32)
                 for p in range(PAIRS_PER_GROUP)], axis=0)
            sink_fill[g, half] = jnp.where(kj == 0, sink_rows, -jnp.inf)
    first_key = lax.broadcasted_iota(jnp.int32, (2 * BLOCK, LANES), 0) == 0
    drop_value = (first_key & low_head, first_key & jnp.logical_not(low_head))

    def scores(n, g, half):
        qrows = slice(n * BLOCK, (n + 1) * BLOCK)
        qst = jnp.concatenate(
            [qs_ref[qrows, (g * PAIRS_PER_GROUP + p) * LANES:(g * PAIRS_PER_GROUP + p + 1) * LANES]
             for p in range(PAIRS_PER_GROUP)], axis=0)
        return _dot_nt(qst, ks_ref[2 * g + half, n * BLOCK:(n + 2) * BLOCK, :])

    units = [(n, g, half) for n in range(nb) for g in range(N_KV_HEADS) for half in range(2)]
    s_next = scores(*units[0])
    o = []
    for idx, (n, g, half) in enumerate(units):
        s = s_next
        if idx + 1 < len(units):
            s_next = scores(*units[idx + 1])
        valid_prev = band_prev & (t * nb + n > 0)
        s_prev = jnp.where(valid_prev, s[:, :BLOCK], sink_fill[g, half])
        s_cur = jnp.where(band_cur, s[:, BLOCK:], -jnp.inf)
        m = jnp.max(jnp.maximum(s_prev, s_cur), axis=-1, keepdims=True)
        p_ = jnp.concatenate([jnp.exp2(s_prev - m), jnp.exp2(s_cur - m)], axis=1).astype(BF16)
        vwin = vs_ref[2 * g + half, n * BLOCK:(n + 2) * BLOCK, :]
        vwin = jnp.where(drop_value[half], jnp.zeros_like(vwin), vwin)
        o.append(_dot(p_, vwin))
        if half == 1:
            num = jnp.where(low_head, o[0], o[1])
            den = pltpu.roll(jnp.where(low_head, o[1], o[0]), HEAD_DIM, 1)
            outs = num / den
            o = []
            for p in range(PAIRS_PER_GROUP):
                c = g * PAIRS_PER_GROUP + p
                o_ref[n * BLOCK:(n + 1) * BLOCK, c * LANES:(c + 1) * LANES] = (
                    outs[p * BLOCK:(p + 1) * BLOCK].astype(BF16))


def _attn(z, pos_f, inv_freq, q_gain, k_gain, sinks, *, batch, seq, tq):
    n = batch * seq
    nt = seq // tq
    nb = tq // BLOCK
    bps = seq // BLOCK

    def cur(col):
        return lambda b, t: (b * nt + t, col)

    def prev(col):
        return lambda b, t: (b * bps + jnp.maximum(t * nb - 1, 0), col)

    return pl.pallas_call(
        functools.partial(_attn_body, tq=tq),
        grid=(batch, nt),
        in_specs=[
            pl.BlockSpec(memory_space=pltpu.SMEM),
            pl.BlockSpec((tq, ATTN_WIDTH), cur(Z_AQ)),
            pl.BlockSpec((tq, KV_WIDTH), cur(Z_AK)),
            pl.BlockSpec((BLOCK, KV_WIDTH), prev(Z_AK)),
            pl.BlockSpec((tq, KV_WIDTH), cur(Z_AV)),
            pl.BlockSpec((BLOCK, KV_WIDTH), prev(Z_AV)),
            pl.BlockSpec((tq, 1), cur(0)),
            pl.BlockSpec((BLOCK, 1), prev(0)),
            pl.BlockSpec((1, LANES), lambda b, t: (0, 0)),
            pl.BlockSpec((1, LANES), lambda b, t: (0, 0)),
            pl.BlockSpec((1, LANES), lambda b, t: (0, 0)),
        ],
        out_specs=pl.BlockSpec((tq, ATTN_WIDTH), cur(0)),
        out_shape=jax.ShapeDtypeStruct((n, ATTN_WIDTH), BF16),
        scratch_shapes=[
            pltpu.VMEM((tq, ATTN_WIDTH), BF16),
            pltpu.VMEM((4, tq + BLOCK, LANES), BF16),
            pltpu.VMEM((4, tq + BLOCK, LANES), BF16),
        ],
        compiler_params=pltpu.CompilerParams(
            dimension_semantics=("parallel", "arbitrary"), vmem_limit_bytes=VMEM_LIMIT),
        name="attn",
    )(sinks, z, z, z, z, z, pos_f, pos_f, inv_freq, q_gain, k_gain)


def _hgrn_body(zq_ref, zf_ref, zi_ref, zo_ref, lbt_ref, gain_ref, o_ref, state_ref, *, layer, tile):
    @pl.when(pl.program_id(1) == 0)
    def _():
        state_ref[...] = jnp.zeros_like(state_ref)

    lbt = lbt_ref[...]
    e = jnp.exp(lbt - jnp.max(lbt, axis=0, keepdims=True))
    sm = e / jnp.sum(e, axis=0, keepdims=True)
    lb = jnp.sum(sm[:layer + 1], axis=0, keepdims=True)

    ri = lax.broadcasted_iota(jnp.int32, (CHUNK, CHUNK), 0)
    ci = lax.broadcasted_iota(jnp.int32, (CHUNK, CHUNK), 1)
    causal = ri >= ci
    tri = jnp.where(causal, 1.0, 0.0).astype(BF16)
    gain = gain_ref[...]

    def chunk(c, carry):
        rows = pl.ds(pl.multiple_of(c * CHUNK, CHUNK), CHUNK)
        f = lb + (1.0 - lb) * jax.nn.sigmoid(zf_ref[rows, :].astype(F32))
        kk = 1.0 - f
        hi, mid, lo = _split3(jnp.log(f))
        b = _dot(tri, hi) + _dot(tri, mid) + _dot(tri, lo)
        b_mid = b[CHUNK // 2:CHUNK // 2 + 1]
        b_last = b[CHUNK - 1:CHUNK]
        qh = jax.nn.silu(zq_ref[rows, :].astype(F32))
        q_mid = (qh * jnp.exp(b - b_mid)).astype(BF16)
        k_mid = (kk * jnp.exp(b_mid - b)).astype(BF16)
        q_in = (qh * jnp.exp(b)).astype(BF16)
        k_out = (kk * jnp.exp(b_last - b)).astype(BF16)
        decay = jnp.exp(b_last)
        heads = [slice(h * HG_HEAD_DIM, (h + 1) * HG_HEAD_DIM) for h in range(HG_HEADS)]
        att = [_dot_nt(q_mid[:, sl], k_mid[:, sl]) for sl in heads]
        st = [state_ref[h] for h in range(HG_HEADS)]
        o_inter = [_dot_nt(q_in[:, sl], st[h].astype(BF16)) for h, sl in enumerate(heads)]
        upd = [_dot_tn(zi_ref[rows, sl], k_out[:, sl]) for sl in heads]
        o = [o_inter[h] + _dot(jnp.where(causal, att[h], 0.0).astype(BF16), zi_ref[rows, sl])
             for h, sl in enumerate(heads)]
        for h, sl in enumerate(heads):
            state_ref[h] = st[h] * decay[:, sl] + upd[h]
            y = _rms_rows(o[h], gain) * jax.nn.silu(zo_ref[rows, sl].astype(F32))
            o_ref[rows, sl] = y.astype(BF16)
        return carry

    lax.fori_loop(0, tile // CHUNK, chunk, 0, unroll=True)


def _hgrn(z, lb_table, gain, *, layer, batch, seq, tile):
    n = batch * seq
    nt = seq // tile

    def cur(col):
        return lambda b, t: (b * nt + t, col)

    return pl.pallas_call(
        functools.partial(_hgrn_body, layer=layer, tile=tile),
        grid=(batch, nt),
        in_specs=[
            pl.BlockSpec((tile, HG_WIDTH), cur(Z_GQ)),
            pl.BlockSpec((tile, HG_WIDTH), cur(Z_GF)),
            pl.BlockSpec((tile, HG_WIDTH), cur(Z_GI)),
            pl.BlockSpec((tile, HG_WIDTH), cur(Z_GO)),
            pl.BlockSpec(lb_table.shape, lambda b, t: (0, 0)),
            pl.BlockSpec((1, HG_HEAD_DIM), lambda b, t: (0, 0)),
        ],
        out_specs=pl.BlockSpec((tile, HG_WIDTH), cur(0)),
        out_shape=jax.ShapeDtypeStruct((n, HG_WIDTH), BF16),
        scratch_shapes=[pltpu.VMEM((HG_HEADS, HG_HEAD_DIM, HG_HEAD_DIM), F32)],
        compiler_params=pltpu.CompilerParams(
            dimension_semantics=("parallel", "arbitrary"), vmem_limit_bytes=VMEM_LIMIT),
        name="hgrn",
    )(z, z, z, z, lb_table, gain)


def _merge_body(x_ref, ya_ref, yh_ref, br_ref, wa_ref, wr_ref, wo_ref, o_ref):
    a = _dot(ya_ref[...], wa_ref[...])
    r = _dot(yh_ref[...], wr_ref[...])
    ga = jax.nn.sigmoid(br_ref[:, :D_MODEL].astype(F32))
    gr = jax.nn.sigmoid(br_ref[:, D_MODEL:].astype(F32))
    merged = (ga * a + gr * r).astype(BF16)
    o_ref[...] = x_ref[...] + _dot(merged, wo_ref[...])


def _merge(x, ya, yh, z, w_a, w_r, w_o, *, tm):
    n = x.shape[0]
    whole = lambda i: (0, 0)
    return pl.pallas_call(
        _merge_body,
        grid=(n // tm,),
        in_specs=[
            pl.BlockSpec((tm, D_MODEL), lambda i: (i, 0)),
            pl.BlockSpec((tm, ATTN_WIDTH), lambda i: (i, 0)),
            pl.BlockSpec((tm, HG_WIDTH), lambda i: (i, 0)),
            pl.BlockSpec((tm, 2 * D_MODEL), lambda i: (i, Z_BR)),
            pl.BlockSpec((ATTN_WIDTH, D_MODEL), whole),
            pl.BlockSpec((HG_WIDTH, D_MODEL), whole),
            pl.BlockSpec((D_MODEL, D_MODEL), whole),
        ],
        out_specs=pl.BlockSpec((tm, D_MODEL), lambda i: (i, 0)),
        out_shape=jax.ShapeDtypeStruct((n, D_MODEL), F32),
        compiler_params=pltpu.CompilerParams(
            dimension_semantics=("parallel",), vmem_limit_bytes=VMEM_LIMIT),
        name="merge",
    )(x, ya, yh, z, w_a, w_r, w_o)


def _tile(n, want):
    t = min(want, n)
    assert n % t == 0, (n, t)
    return t


def kernel(x, positions, lb_table, ffn1_norm, ffn1_w_gu, ffn1_w_down, mix_norm, w_in, q_norm, k_norm, sinks,
           hg_out_norm, w_attn_branch, w_hg_branch, w_out, ffn2_norm, ffn2_w_gu, ffn2_w_down):
    batch, seq = x.shape[0], x.shape[1]
    n = batch * seq
    depth = w_in.shape[0]
    assert seq % BLOCK == 0 and seq % CHUNK == 0

    half = HEAD_DIM // 2
    inv_freq = ROPE_THETA ** (-jnp.arange(half, dtype=F32) * 2.0 / HEAD_DIM)
    inv_freq = jnp.tile(inv_freq, LANES // half)[None, :]
    pos_f = positions.astype(F32).reshape(n, 1)
    n_attn = ATTN_WIDTH + 2 * KV_WIDTH

    xf = x.reshape(n, D_MODEL)
    for l in range(depth):
        w_in_l = jnp.concatenate([w_in[l][:, n_attn:], w_in[l][:, :n_attn]], axis=1).astype(BF16)
        xf = _ffn(xf, ffn1_norm[l][None, :], ffn1_w_gu[l].astype(BF16), ffn1_w_down[l].astype(BF16),
                  tm=_tile(n, 1024), tf=512)
        z = _inproj(xf, mix_norm[l][None, :], w_in_l, tm=_tile(n, 512))
        ya = _attn(z, pos_f, inv_freq, jnp.tile(q_norm[l], 2)[None, :], jnp.tile(k_norm[l], 2)[None, :], sinks[l],
                   batch=batch, seq=seq, tq=_tile(seq, 512))
        yh = _hgrn(z, lb_table, hg_out_norm[l][None, :], layer=l, batch=batch, seq=seq, tile=_tile(seq, 512))
        xf = _merge(xf, ya, yh, z, w_attn_branch[l].astype(BF16), w_hg_branch[l].astype(BF16),
                    w_out[l].astype(BF16), tm=_tile(n, 256))
        xf = _ffn(xf, ffn2_norm[l][None, :], ffn2_w_gu[l].astype(BF16), ffn2_w_down[l].astype(BF16),
                  tm=_tile(n, 1024), tf=512)
    return xf.reshape(batch, seq, D_MODEL)
```

```python
import functools

import jax
import jax.numpy as jnp
from jax import lax
from jax.experimental import pallas as pl
from jax.experimental.pallas import tpu as pltpu

D_MODEL = 2048
HEAD_DIM = 64
N_Q_HEADS = 16
N_KV_HEADS = 2
GQA_GROUP = N_Q_HEADS // N_KV_HEADS
ATTN_WIDTH = N_Q_HEADS * HEAD_DIM
KV_WIDTH = N_KV_HEADS * HEAD_DIM
WINDOW = 128
BLOCK = 128
ROPE_THETA = 10000.0
HG_HEAD_DIM = 128
HG_HEADS = 8
HG_WIDTH = HG_HEADS * HG_HEAD_DIM
CHUNK = 64
D_FF = 5632
EPS = 1e-6

LANES = 128
PAIRS_PER_GROUP = GQA_GROUP // 2
VMEM_LIMIT = 56 * 1024 * 1024
LOG2E = 1.4426950408889634

N_ATTN = ATTN_WIDTH + 2 * KV_WIDTH
COLS_A = N_ATTN + 3 * HG_WIDTH
COLS_B = HG_WIDTH + 2 * D_MODEL

F32 = jnp.float32
BF16 = jnp.bfloat16


def _dot(a, b):
    return jnp.dot(a, b, preferred_element_type=F32)


def _dot_nt(a, b):
    return lax.dot_general(a, b, (((1,), (1,)), ((), ())), preferred_element_type=F32)


def _dot_tn(a, b):
    return lax.dot_general(a, b, (((0,), (0,)), ((), ())), preferred_element_type=F32)


def _rms_rows(x, gain):
    ms = jnp.mean(x * x, axis=-1, keepdims=True)
    return x * lax.rsqrt(ms + EPS) * gain


def _split3(v):
    hi = v.astype(BF16)
    r1 = v - hi.astype(F32)
    mid = r1.astype(BF16)
    lo = (r1 - mid.astype(F32)).astype(BF16)
    return hi, mid, lo


def _ffn_body(x_hbm, g_ref, wg_ref, wu_ref, wd_ref, o_ref, h_ref, xbuf_ref, sem, *, tm):
    i = pl.program_id(0)
    j = pl.program_id(1)

    def x_copy(tile):
        rows = pl.ds(pl.multiple_of(tile * tm, tm), tm)
        return pltpu.make_async_copy(x_hbm.at[rows], xbuf_ref, sem)

    @pl.when((i == 0) & (j == 0))
    def _():
        x_copy(0).start()

    @pl.when(j == 0)
    def _():
        x_copy(i).wait()
        x = xbuf_ref[...]
        h_ref[...] = _rms_rows(x, g_ref[...]).astype(BF16)
        o_ref[...] = x

    @pl.when((j == 1) & (i + 1 < pl.num_programs(0)))
    def _():
        x_copy(i + 1).start()

    h = h_ref[...]
    gate = _dot(h, wg_ref[...])
    up = _dot(h, wu_ref[...])
    act = (gate * jax.nn.sigmoid(gate)) * (up * 0.5)
    o_ref[...] += _dot(act.astype(BF16), wd_ref[...])


def _ffn(x, gain, w_gu, w_down, *, tm, tf):
    n = x.shape[0]
    nf = D_FF // tf
    assert nf >= 2
    return pl.pallas_call(
        functools.partial(_ffn_body, tm=tm),
        grid=(n // tm, nf),
        in_specs=[
            pl.BlockSpec(memory_space=pl.ANY),
            pl.BlockSpec((1, D_MODEL), lambda i, j: (0, 0)),
            pl.BlockSpec((D_MODEL, tf), lambda i, j: (0, j)),
            pl.BlockSpec((D_MODEL, tf), lambda i, j: (0, j + nf)),
            pl.BlockSpec((tf, D_MODEL), lambda i, j: (j, 0)),
        ],
        out_specs=pl.BlockSpec((tm, D_MODEL), lambda i, j: (i, 0)),
        out_shape=jax.ShapeDtypeStruct((n, D_MODEL), F32),
        scratch_shapes=[pltpu.VMEM((tm, D_MODEL), BF16), pltpu.VMEM((tm, D_MODEL), F32),
                        pltpu.SemaphoreType.DMA(())],
        compiler_params=pltpu.CompilerParams(
            dimension_semantics=("arbitrary", "arbitrary"), vmem_limit_bytes=VMEM_LIMIT),
        name="ffn",
    )(x, gain, w_gu, w_gu, w_down)


def _proj_a_body(x_ref, g_ref, w_ref, pos_ref, invf_ref, qg_ref, kg_ref, lbt_ref,
                 q_ref, k_ref, v_ref, hq_ref, hk_ref, hl_ref, hi_ref, *, layer):
    h = _rms_rows(x_ref[...], g_ref[...]).astype(BF16)

    def cols(start, width):
        return _dot(h, w_ref[:, start:start + width])

    lane = lax.broadcasted_iota(jnp.int32, (1, LANES), 1)
    first_half = (lane % HEAD_DIM) < (HEAD_DIM // 2)
    gi = lax.broadcasted_iota(jnp.int32, (LANES, LANES), 0) // HEAD_DIM
    gj = lax.broadcasted_iota(jnp.int32, (LANES, LANES), 1) // HEAD_DIM
    gsum = jnp.where(gi == gj, 1.0, 0.0).astype(BF16)

    def head_norm(v, gain):
        sq = v * v
        hi = sq.astype(BF16)
        lo = (sq - hi.astype(F32)).astype(BF16)
        ss = _dot(hi, gsum) + _dot(lo, gsum)
        return v * lax.rsqrt(ss * (1.0 / HEAD_DIM) + EPS) * gain

    ang = pos_ref[...] * invf_ref[...]
    cosf = jnp.cos(ang)
    sinf = jnp.sin(ang)
    sin_signed = jnp.where(first_half, -sinf, sinf)

    def rope(v):
        rot = jnp.where(first_half, pltpu.roll(v, LANES - HEAD_DIM // 2, 1), pltpu.roll(v, HEAD_DIM // 2, 1))
        return v * cosf + rot * sin_signed

    lbt = lbt_ref[...]
    e = jnp.exp(lbt - jnp.max(lbt, axis=0, keepdims=True))
    sm = e / jnp.sum(e, axis=0, keepdims=True)
    lb = jnp.sum(sm[:layer + 1], axis=0, keepdims=True)

    zq = cols(0, ATTN_WIDTH)
    zkv = cols(ATTN_WIDTH, 2 * KV_WIDTH)

    def attn_piece(c):
        if c < ATTN_WIDTH // LANES:
            sl = slice(c * LANES, (c + 1) * LANES)
            q_ref[:, sl] = (rope(head_norm(zq[:, sl], qg_ref[...])) * (HEAD_DIM ** -0.5 * LOG2E)).astype(BF16)
        else:
            k_ref[...] = rope(head_norm(zkv[:, :KV_WIDTH], kg_ref[...])).astype(BF16)
            v_ref[...] = zkv[:, KV_WIDTH:].astype(BF16)

    def hgrn_piece(kind, sl):
        z = cols(N_ATTN + kind * HG_WIDTH + sl.start, sl.stop - sl.start)
        if kind == 0:
            hq_ref[:, sl] = jax.nn.silu(z).astype(BF16)
        elif kind == 1:
            f = lb[:, sl] + (1.0 - lb[:, sl]) * jax.nn.sigmoid(z)
            hk_ref[:, sl] = (1.0 - f).astype(BF16)
            hl_ref[:, sl] = jnp.log(f)
        else:
            hi_ref[:, sl] = z.astype(BF16)

    chunk = 2 * LANES
    pieces = [(kind, slice(c, c + chunk)) for kind in range(3) for c in range(0, HG_WIDTH, chunk)]
    for idx, (kind, sl) in enumerate(pieces):
        hgrn_piece(kind, sl)
        if idx <= ATTN_WIDTH // LANES:
            attn_piece(idx)


def _proj_a(x, gain, w_a, pos_f, inv_freq, q_gain, k_gain, lb_table, *, layer, tm):
    n = x.shape[0]
    row = lambda i: (i, 0)
    whole = lambda i: (0, 0)
    widths = (ATTN_WIDTH, KV_WIDTH, KV_WIDTH, HG_WIDTH, HG_WIDTH, HG_WIDTH, HG_WIDTH)
    dtypes = (BF16, BF16, BF16, BF16, BF16, F32, BF16)
    return pl.pallas_call(
        functools.partial(_proj_a_body, layer=layer),
        grid=(n // tm,),
        in_specs=[
            pl.BlockSpec((tm, D_MODEL), row),
            pl.BlockSpec((1, D_MODEL), whole),
            pl.BlockSpec((D_MODEL, COLS_A), whole, pipeline_mode=pl.Buffered(1)),
            pl.BlockSpec((tm, 1), row),
            pl.BlockSpec((1, LANES), whole),
            pl.BlockSpec((1, LANES), whole),
            pl.BlockSpec((1, LANES), whole),
            pl.BlockSpec(lb_table.shape, whole),
        ],
        out_specs=[pl.BlockSpec((tm, w), row) for w in widths],
        out_shape=[jax.ShapeDtypeStruct((n, w), d) for w, d in zip(widths, dtypes)],
        compiler_params=pltpu.CompilerParams(
            dimension_semantics=("parallel",), vmem_limit_bytes=VMEM_LIMIT),
        name="proj_a",
    )(x, gain, w_a, pos_f, inv_freq, q_gain, k_gain, lb_table)


def _proj_b_body(x_ref, g_ref, w_ref, og_ref, ga_ref, gr_ref):
    h = _rms_rows(x_ref[...], g_ref[...]).astype(BF16)

    def cols(start, width):
        return _dot(h, w_ref[:, start:start + width])

    og_ref[...] = jax.nn.silu(cols(0, HG_WIDTH)).astype(BF16)
    for c in range(D_MODEL // HG_WIDTH):
        sl = slice(c * HG_WIDTH, (c + 1) * HG_WIDTH)
        ga_ref[:, sl] = jax.nn.sigmoid(cols(HG_WIDTH + c * HG_WIDTH, HG_WIDTH)).astype(BF16)
        gr_ref[:, sl] = jax.nn.sigmoid(cols(HG_WIDTH + D_MODEL + c * HG_WIDTH, HG_WIDTH)).astype(BF16)


def _proj_b(x, gain, w_b, *, tm):
    n = x.shape[0]
    row = lambda i: (i, 0)
    whole = lambda i: (0, 0)
    widths = (HG_WIDTH, D_MODEL, D_MODEL)
    return pl.pallas_call(
        _proj_b_body,
        grid=(n // tm,),
        in_specs=[
            pl.BlockSpec((tm, D_MODEL), row),
            pl.BlockSpec((1, D_MODEL), whole),
            pl.BlockSpec((D_MODEL, COLS_B), whole, pipeline_mode=pl.Buffered(1)),
        ],
        out_specs=[pl.BlockSpec((tm, w), row) for w in widths],
        out_shape=[jax.ShapeDtypeStruct((n, w), BF16) for w in widths],
        compiler_params=pltpu.CompilerParams(
            dimension_semantics=("parallel",), vmem_limit_bytes=VMEM_LIMIT),
        name="proj_b",
    )(x, gain, w_b)


def _attn_body(sinks_ref, q_ref, k_ref, kp_ref, v_ref, vp_ref, o_ref, ks_ref, vs_ref, *, tq):
    t = pl.program_id(1)
    nb = tq // BLOCK
    lane = lax.broadcasted_iota(jnp.int32, (1, LANES), 1)
    low_head = lane < HEAD_DIM

    def put_kv(dst_ref, rows, v, pad):
        swapped = pltpu.roll(v, HEAD_DIM, 1)
        dst_ref[0, rows, :] = jnp.where(low_head, v, pad).astype(BF16)
        dst_ref[1, rows, :] = jnp.where(low_head, pad, swapped).astype(BF16)
        dst_ref[2, rows, :] = jnp.where(low_head, swapped, pad).astype(BF16)
        dst_ref[3, rows, :] = jnp.where(low_head, pad, v).astype(BF16)

    put_kv(ks_ref, slice(0, BLOCK), kp_ref[...].astype(F32), 0.0)
    put_kv(ks_ref, slice(BLOCK, BLOCK + tq), k_ref[...].astype(F32), 0.0)
    put_kv(vs_ref, slice(0, BLOCK), vp_ref[...].astype(F32), 1.0)
    put_kv(vs_ref, slice(BLOCK, BLOCK + tq), v_ref[...].astype(F32), 1.0)

    rows_q = PAIRS_PER_GROUP * BLOCK
    qi = lax.broadcasted_iota(jnp.int32, (rows_q, BLOCK), 0) % BLOCK
    kj = lax.broadcasted_iota(jnp.int32, (rows_q, BLOCK), 1)
    band_prev = kj > qi + BLOCK - WINDOW
    band_cur = kj <= qi
    sink_fill = {}
    for g in range(N_KV_HEADS):
        for half in range(2):
            sink_rows = jnp.concatenate(
                [jnp.full((BLOCK, LANES), sinks_ref[g * GQA_GROUP + 2 * p + half] * LOG2E, F32)
                 for p in range(PAIRS_PER_GROUP)], axis=0)
            sink_fill[g, half] = jnp.where(kj == 0, sink_rows, -jnp.inf)
    first_key = lax.broadcasted_iota(jnp.int32, (2 * BLOCK, LANES), 0) == 0
    drop_value = (first_key & low_head, first_key & jnp.logical_not(low_head))

    def block(n, carry):
        qrows = pl.ds(pl.multiple_of(n * BLOCK, BLOCK), BLOCK)
        krows = pl.ds(pl.multiple_of(n * BLOCK, BLOCK), 2 * BLOCK)
        valid_prev = band_prev & (t * nb + n > 0)

        def scores(g, half):
            qst = jnp.concatenate(
                [q_ref[qrows, (g * PAIRS_PER_GROUP + p) * LANES:(g * PAIRS_PER_GROUP + p + 1) * LANES]
                 for p in range(PAIRS_PER_GROUP)], axis=0)
            return _dot_nt(qst, ks_ref[2 * g + half, krows, :])

        units = [(g, half) for g in range(N_KV_HEADS) for half in range(2)]
        s_next = scores(*units[0])
        o = []
        for idx, (g, half) in enumerate(units):
            s = s_next
            if idx + 1 < len(units):
                s_next = scores(*units[idx + 1])
            s_prev = jnp.where(valid_prev, s[:, :BLOCK], sink_fill[g, half])
            s_cur = jnp.where(band_cur, s[:, BLOCK:], -jnp.inf)
            m = jnp.max(jnp.maximum(s_prev, s_cur), axis=-1, keepdims=True)
            p_ = jnp.concatenate([jnp.exp2(s_prev - m), jnp.exp2(s_cur - m)], axis=1).astype(BF16)
            vwin = vs_ref[2 * g + half, krows, :]
            vwin = jnp.where(drop_value[half], jnp.zeros_like(vwin), vwin)
            o.append(_dot(p_, vwin))
            if half == 1:
                num = jnp.where(low_head, o[0], o[1])
                den = pltpu.roll(jnp.where(low_head, o[1], o[0]), HEAD_DIM, 1)
                outs = num / den
                o = []
                for p in range(PAIRS_PER_GROUP):
                    c = g * PAIRS_PER_GROUP + p
                    o_ref[qrows, c * LANES:(c + 1) * LANES] = outs[p * BLOCK:(p + 1) * BLOCK].astype(BF16)
        return carry

    lax.fori_loop(0, nb, block, 0)


def _attn(q, k, v, sinks, *, batch, seq, tq):
    n = batch * seq
    nt = seq // tq
    nb = tq // BLOCK
    bps = seq // BLOCK
    cur = lambda b, t: (b * nt + t, 0)
    prev = lambda b, t: (b * bps + jnp.maximum(t * nb - 1, 0), 0)
    return pl.pallas_call(
        functools.partial(_attn_body, tq=tq),
        grid=(batch, nt),
        in_specs=[
            pl.BlockSpec(memory_space=pltpu.SMEM),
            pl.BlockSpec((tq, ATTN_WIDTH), cur),
            pl.BlockSpec((tq, KV_WIDTH), cur),
            pl.BlockSpec((BLOCK, KV_WIDTH), prev),
            pl.BlockSpec((tq, KV_WIDTH), cur),
            pl.BlockSpec((BLOCK, KV_WIDTH), prev),
        ],
        out_specs=pl.BlockSpec((tq, ATTN_WIDTH), cur),
        out_shape=jax.ShapeDtypeStruct((n, ATTN_WIDTH), BF16),
        scratch_shapes=[
            pltpu.VMEM((4, tq + BLOCK, LANES), BF16),
            pltpu.VMEM((4, tq + BLOCK, LANES), BF16),
        ],
        compiler_params=pltpu.CompilerParams(
            dimension_semantics=("parallel", "parallel"), vmem_limit_bytes=VMEM_LIMIT),
        name="attn",
    )(sinks, q, k, k, v, v)


def _hgrn_body(hq_ref, hk_ref, hl_ref, hi_ref, og_ref, gain_ref, o_ref, state_ref, *, tile):
    @pl.when(pl.program_id(1) == 0)
    def _():
        state_ref[...] = jnp.zeros_like(state_ref)

    ri = lax.broadcasted_iota(jnp.int32, (CHUNK, CHUNK), 0)
    ci = lax.broadcasted_iota(jnp.int32, (CHUNK, CHUNK), 1)
    causal = ri >= ci
    tri = jnp.where(causal, 1.0, 0.0).astype(BF16)
    gain = gain_ref[...]

    def chunk(c, carry):
        rows = pl.ds(pl.multiple_of(c * CHUNK, CHUNK), CHUNK)
        hi, mid, lo = _split3(hl_ref[rows, :])
        b = _dot(tri, hi) + _dot(tri, mid) + _dot(tri, lo)
        b_mid = b[CHUNK // 2:CHUNK // 2 + 1]
        b_last = b[CHUNK - 1:CHUNK]
        qh = hq_ref[rows, :].astype(F32)
        kk = hk_ref[rows, :].astype(F32)
        q_mid = (qh * jnp.exp(b - b_mid)).astype(BF16)
        k_mid = (kk * jnp.exp(b_mid - b)).astype(BF16)
        q_in = (qh * jnp.exp(b)).astype(BF16)
        k_out = (kk * jnp.exp(b_last - b)).astype(BF16)
        decay = jnp.exp(b_last)
        heads = [slice(h * HG_HEAD_DIM, (h + 1) * HG_HEAD_DIM) for h in range(HG_HEADS)]
        att = [_dot_nt(q_mid[:, sl], k_mid[:, sl]) for sl in heads]
        st = [state_ref[h] for h in range(HG_HEADS)]
        o_inter = [_dot_nt(q_in[:, sl], st[h].astype(BF16)) for h, sl in enumerate(heads)]
        upd = [_dot_tn(hi_ref[rows, sl], k_out[:, sl]) for sl in heads]
        o = [o_inter[h] + _dot(jnp.where(causal, att[h], 0.0).astype(BF16), hi_ref[rows, sl])
             for h, sl in enumerate(heads)]
        for h, sl in enumerate(heads):
            state_ref[h] = st[h] * decay[:, sl] + upd[h]
            y = _rms_rows(o[h], gain) * og_ref[rows, sl].astype(F32)
            o_ref[rows, sl] = y.astype(BF16)
        return carry

    lax.fori_loop(0, tile // CHUNK, chunk, 0, unroll=True)


def _hgrn(hq, hk, hl, hi, og, gain, *, batch, seq, tile):
    n = batch * seq
    nt = seq // tile
    cur = lambda b, t: (b * nt + t, 0)
    return pl.pallas_call(
        functools.partial(_hgrn_body, tile=tile),
        grid=(batch, nt),
        in_specs=[pl.BlockSpec((tile, HG_WIDTH), cur)] * 5 + [pl.BlockSpec((1, HG_HEAD_DIM), lambda b, t: (0, 0))],
        out_specs=pl.BlockSpec((tile, HG_WIDTH), cur),
        out_shape=jax.ShapeDtypeStruct((n, HG_WIDTH), BF16),
        scratch_shapes=[pltpu.VMEM((HG_HEADS, HG_HEAD_DIM, HG_HEAD_DIM), F32)],
        compiler_params=pltpu.CompilerParams(
            dimension_semantics=("parallel", "arbitrary"), vmem_limit_bytes=VMEM_LIMIT),
        name="hgrn",
    )(hq, hk, hl, hi, og, gain)


def _merge_body(x_ref, ya_ref, yh_ref, ga_ref, gr_ref, wa_ref, wr_ref, wo_ref, o_ref):
    a = _dot(ya_ref[...], wa_ref[...])
    r = _dot(yh_ref[...], wr_ref[...])
    merged = (ga_ref[...].astype(F32) * a + gr_ref[...].astype(F32) * r).astype(BF16)
    o_ref[...] = x_ref[...] + _dot(merged, wo_ref[...])


def _merge(x, ya, yh, ga, gr, w_a, w_r, w_o, *, tm):
    n = x.shape[0]
    row = lambda i: (i, 0)
    whole = lambda i: (0, 0)
    return pl.pallas_call(
        _merge_body,
        grid=(n // tm,),
        in_specs=[
            pl.BlockSpec((tm, D_MODEL), row),
            pl.BlockSpec((tm, ATTN_WIDTH), row),
            pl.BlockSpec((tm, HG_WIDTH), row),
            pl.BlockSpec((tm, D_MODEL), row),
            pl.BlockSpec((tm, D_MODEL), row),
            pl.BlockSpec((ATTN_WIDTH, D_MODEL), whole, pipeline_mode=pl.Buffered(1)),
            pl.BlockSpec((HG_WIDTH, D_MODEL), whole, pipeline_mode=pl.Buffered(1)),
            pl.BlockSpec((D_MODEL, D_MODEL), whole, pipeline_mode=pl.Buffered(1)),
        ],
        out_specs=pl.BlockSpec((tm, D_MODEL), row),
        out_shape=jax.ShapeDtypeStruct((n, D_MODEL), F32),
        compiler_params=pltpu.CompilerParams(
            dimension_semantics=("parallel",), vmem_limit_bytes=VMEM_LIMIT),
        name="merge",
    )(x, ya, yh, ga, gr, w_a, w_r, w_o)


def _tile(n, want):
    t = min(want, n)
    assert n % t == 0, (n, t)
    return t


def kernel(x, positions, lb_table, ffn1_norm, ffn1_w_gu, ffn1_w_down, mix_norm, w_in, q_norm, k_norm, sinks,
           hg_out_norm, w_attn_branch, w_hg_branch, w_out, ffn2_norm, ffn2_w_gu, ffn2_w_down):
    batch, seq = x.shape[0], x.shape[1]
    n = batch * seq
    depth = w_in.shape[0]
    assert seq % BLOCK == 0 and seq % CHUNK == 0

    half = HEAD_DIM // 2
    inv_freq = ROPE_THETA ** (-jnp.arange(half, dtype=F32) * 2.0 / HEAD_DIM)
    inv_freq = jnp.tile(inv_freq, LANES // half)[None, :]
    pos_f = positions.astype(F32).reshape(n, 1)

    xf = x.reshape(n, D_MODEL)
    for l in range(depth):
        xf = _ffn(xf, ffn1_norm[l][None, :], ffn1_w_gu[l].astype(BF16), ffn1_w_down[l].astype(BF16),
                  tm=_tile(n, 1024), tf=512)
        gain = mix_norm[l][None, :]
        q, k, v, hq, hk, hl, hi = _proj_a(
            xf, gain, w_in[l][:, :COLS_A].astype(BF16), pos_f, inv_freq, jnp.tile(q_norm[l], 2)[None, :],
            jnp.tile(k_norm[l], 2)[None, :], lb_table, layer=l, tm=_tile(n, 512))
        og, ga, gr = _proj_b(xf, gain, w_in[l][:, COLS_A:].astype(BF16), tm=_tile(n, 512))
        ya = _attn(q, k, v, sinks[l], batch=batch, seq=seq, tq=_tile(seq, 512))
        yh = _hgrn(hq, hk, hl, hi, og, hg_out_norm[l][None, :], batch=batch, seq=seq, tile=_tile(seq, 512))
        xf = _merge(xf, ya, yh, ga, gr, w_attn_branch[l].astype(BF16), w_hg_branch[l].astype(BF16),
                    w_out[l].astype(BF16), tm=_tile(n, 256))
        xf = _ffn(xf, ffn2_norm[l][None, :], ffn2_w_gu[l].astype(BF16), ffn2_w_down[l].astype(BF16),
                  tm=_tile(n, 1024), tf=512)
    return xf.reshape(batch, seq, D_MODEL)
```

```python
import functools

import jax
import jax.numpy as jnp
from jax import lax
from jax.experimental import pallas as pl
from jax.experimental.pallas import tpu as pltpu

D_MODEL = 2048
HEAD_DIM = 64
N_Q_HEADS = 16
N_KV_HEADS = 2
GQA_GROUP = N_Q_HEADS // N_KV_HEADS
ATTN_WIDTH = N_Q_HEADS * HEAD_DIM
KV_WIDTH = N_KV_HEADS * HEAD_DIM
WINDOW = 128
BLOCK = 128
ROPE_THETA = 10000.0
HG_HEAD_DIM = 128
HG_HEADS = 8
HG_WIDTH = HG_HEADS * HG_HEAD_DIM
CHUNK = 64
D_FF = 5632
EPS = 1e-6

LANES = 128
PAIRS_PER_GROUP = GQA_GROUP // 2
VMEM_LIMIT = 56 * 1024 * 1024
LOG2E = 1.4426950408889634

N_ATTN = ATTN_WIDTH + 2 * KV_WIDTH
COLS_A = N_ATTN + 3 * HG_WIDTH
COLS_B = HG_WIDTH + 2 * D_MODEL

F32 = jnp.float32
BF16 = jnp.bfloat16


def _dot(a, b):
    return jnp.dot(a, b, preferred_element_type=F32)


def _dot_nt(a, b):
    return lax.dot_general(a, b, (((1,), (1,)), ((), ())), preferred_element_type=F32)


def _dot_tn(a, b):
    return lax.dot_general(a, b, (((0,), (0,)), ((), ())), preferred_element_type=F32)


def _rms_rows(x, gain):
    ms = jnp.mean(x * x, axis=-1, keepdims=True)
    return x * lax.rsqrt(ms + EPS) * gain


def _ffn_body(x_hbm, g_ref, wg_ref, wu_ref, wd_ref, o_ref, h_ref, xbuf_ref, sem, *, tm, tf):
    i = pl.program_id(0)
    j = pl.program_id(1)

    def x_copy(tile):
        rows = pl.ds(pl.multiple_of(tile * tm, tm), tm)
        return pltpu.make_async_copy(x_hbm.at[rows], xbuf_ref, sem)

    @pl.when((i == 0) & (j == 0))
    def _():
        x_copy(0).start()

    @pl.when(j == 0)
    def _():
        x_copy(i).wait()
        x = xbuf_ref[...]
        h_ref[...] = _rms_rows(x, g_ref[...]).astype(BF16)
        o_ref[...] = x

    @pl.when((j == 1) & (i + 1 < pl.num_programs(0)))
    def _():
        x_copy(i + 1).start()

    h = h_ref[...]
    halves = [slice(c * tf // 2, (c + 1) * tf // 2) for c in range(2)]
    gate_up = [(_dot(h, wg_ref[:, sl]), _dot(h, wu_ref[:, sl])) for sl in halves]
    act = [((g * jax.nn.sigmoid(g)) * (u * 0.5)).astype(BF16) for g, u in gate_up]
    o_ref[...] += _dot(act[0], wd_ref[halves[0], :]) + _dot(act[1], wd_ref[halves[1], :])


def _ffn(x, gain, w_gu, w_down, *, tm, tf):
    n = x.shape[0]
    nf = D_FF // tf
    assert nf >= 2
    return pl.pallas_call(
        functools.partial(_ffn_body, tm=tm, tf=tf),
        grid=(n // tm, nf),
        in_specs=[
            pl.BlockSpec(memory_space=pl.ANY),
            pl.BlockSpec((1, D_MODEL), lambda i, j: (0, 0)),
            pl.BlockSpec((D_MODEL, tf), lambda i, j: (0, j)),
            pl.BlockSpec((D_MODEL, tf), lambda i, j: (0, j + nf)),
            pl.BlockSpec((tf, D_MODEL), lambda i, j: (j, 0)),
        ],
        out_specs=pl.BlockSpec((tm, D_MODEL), lambda i, j: (i, 0)),
        out_shape=jax.ShapeDtypeStruct((n, D_MODEL), F32),
        scratch_shapes=[pltpu.VMEM((tm, D_MODEL), BF16), pltpu.VMEM((tm, D_MODEL), F32),
                        pltpu.SemaphoreType.DMA(())],
        compiler_params=pltpu.CompilerParams(
            dimension_semantics=("arbitrary", "arbitrary"), vmem_limit_bytes=VMEM_LIMIT),
        name="ffn",
    )(x, gain, w_gu, w_gu, w_down)


def _proj_a_body(x_ref, g_ref, w_ref, pos_ref, invf_ref, qg_ref, kg_ref, lbt_ref,
                 q_ref, k_ref, v_ref, hq_ref, hk_ref, hl_ref, hi_ref, *, layer):
    h = _rms_rows(x_ref[...], g_ref[...]).astype(BF16)

    def cols(start, width):
        return _dot(h, w_ref[:, start:start + width])

    lane = lax.broadcasted_iota(jnp.int32, (1, LANES), 1)
    first_half = (lane % HEAD_DIM) < (HEAD_DIM // 2)
    gi = lax.broadcasted_iota(jnp.int32, (LANES, LANES), 0) // HEAD_DIM
    gj = lax.broadcasted_iota(jnp.int32, (LANES, LANES), 1) // HEAD_DIM
    gsum = jnp.where(gi == gj, 1.0, 0.0).astype(BF16)

    def head_norm(v, gain):
        sq = v * v
        hi = sq.astype(BF16)
        lo = (sq - hi.astype(F32)).astype(BF16)
        ss = _dot(hi, gsum) + _dot(lo, gsum)
        return v * lax.rsqrt(ss * (1.0 / HEAD_DIM) + EPS) * gain

    ang = invf_ref[...] * pos_ref[...]
    cos_t = jnp.cos(ang)
    sin_t = jnp.sin(ang)
    reps = LANES // HEAD_DIM
    cosf = jnp.concatenate([cos_t, cos_t] * reps, axis=0).T
    sin_signed = jnp.concatenate([-sin_t, sin_t] * reps, axis=0).T

    def rope(v):
        rot = jnp.where(first_half, pltpu.roll(v, LANES - HEAD_DIM // 2, 1), pltpu.roll(v, HEAD_DIM // 2, 1))
        return v * cosf + rot * sin_signed

    lbt = lbt_ref[...]
    e = jnp.exp(lbt - jnp.max(lbt, axis=0, keepdims=True))
    sm = e / jnp.sum(e, axis=0, keepdims=True)
    lb = jnp.sum(sm[:layer + 1], axis=0, keepdims=True)

    zq = cols(0, ATTN_WIDTH)
    zkv = cols(ATTN_WIDTH, 2 * KV_WIDTH)

    def attn_piece(c):
        if c < ATTN_WIDTH // LANES:
            sl = slice(c * LANES, (c + 1) * LANES)
            q_ref[:, sl] = (rope(head_norm(zq[:, sl], qg_ref[...])) * (HEAD_DIM ** -0.5 * LOG2E)).astype(BF16)
        else:
            k_ref[...] = rope(head_norm(zkv[:, :KV_WIDTH], kg_ref[...])).astype(BF16)
            v_ref[...] = zkv[:, KV_WIDTH:].astype(BF16)

    def hgrn_piece(kind, sl):
        z = cols(N_ATTN + kind * HG_WIDTH + sl.start, sl.stop - sl.start)
        if kind == 0:
            hq_ref[:, sl] = jax.nn.silu(z).astype(BF16)
        elif kind == 1:
            f = lb[:, sl] + (1.0 - lb[:, sl]) * jax.nn.sigmoid(z)
            hk_ref[:, sl] = (1.0 - f).astype(BF16)
            hl_ref[:, sl] = jnp.log2(f)
        else:
            hi_ref[:, sl] = z.astype(BF16)

    chunk = 2 * LANES
    pieces = [(kind, slice(c, c + chunk)) for kind in range(3) for c in range(0, HG_WIDTH, chunk)]
    n_attn_pieces = ATTN_WIDTH // LANES + 1
    for idx, (kind, sl) in enumerate(pieces):
        hgrn_piece(kind, sl)
        if idx < n_attn_pieces:
            attn_piece(idx)


def _proj_a(x, gain, w_a, pos_rows, inv_freq, q_gain, k_gain, lb_table, *, layer, tm):
    n = x.shape[0]
    row = lambda i: (i, 0)
    whole = lambda i: (0, 0)
    widths = (ATTN_WIDTH, KV_WIDTH, KV_WIDTH, HG_WIDTH, HG_WIDTH, HG_WIDTH, HG_WIDTH)
    dtypes = (BF16, BF16, BF16, BF16, BF16, F32, BF16)
    return pl.pallas_call(
        functools.partial(_proj_a_body, layer=layer),
        grid=(n // tm,),
        in_specs=[
            pl.BlockSpec((tm, D_MODEL), row),
            pl.BlockSpec((1, D_MODEL), whole),
            pl.BlockSpec((D_MODEL, COLS_A), whole, pipeline_mode=pl.Buffered(1)),
            pl.BlockSpec((None, 1, tm), lambda i: (i, 0, 0)),
            pl.BlockSpec((HEAD_DIM // 2, 1), whole),
            pl.BlockSpec((1, LANES), whole),
            pl.BlockSpec((1, LANES), whole),
            pl.BlockSpec(lb_table.shape, whole),
        ],
        out_specs=[pl.BlockSpec((tm, w), row) for w in widths],
        out_shape=[jax.ShapeDtypeStruct((n, w), d) for w, d in zip(widths, dtypes)],
        compiler_params=pltpu.CompilerParams(
            dimension_semantics=("parallel",), vmem_limit_bytes=VMEM_LIMIT),
        name="proj_a",
    )(x, gain, w_a, pos_rows, inv_freq, q_gain, k_gain, lb_table)


def _proj_b_body(x_ref, g_ref, w_hbm, og_ref, ga_ref, gr_ref, w_ref, sem):
    @pl.when(pl.program_id(0) == 0)
    def _():
        copy = pltpu.make_async_copy(w_hbm.at[:, COLS_A:], w_ref, sem)
        copy.start()
        copy.wait()

    h = _rms_rows(x_ref[...], g_ref[...]).astype(BF16)

    def cols(start, width):
        return _dot(h, w_ref[:, start:start + width])

    og_ref[...] = jax.nn.silu(cols(0, HG_WIDTH)).astype(BF16)
    for c in range(D_MODEL // HG_WIDTH):
        sl = slice(c * HG_WIDTH, (c + 1) * HG_WIDTH)
        ga_ref[:, sl] = jax.nn.sigmoid(cols(HG_WIDTH + c * HG_WIDTH, HG_WIDTH)).astype(BF16)
        gr_ref[:, sl] = jax.nn.sigmoid(cols(HG_WIDTH + D_MODEL + c * HG_WIDTH, HG_WIDTH)).astype(BF16)


def _proj_b(x, gain, w_in, *, tm):
    n = x.shape[0]
    row = lambda i: (i, 0)
    whole = lambda i: (0, 0)
    widths = (HG_WIDTH, D_MODEL, D_MODEL)
    return pl.pallas_call(
        _proj_b_body,
        grid=(n // tm,),
        in_specs=[
            pl.BlockSpec((tm, D_MODEL), row),
            pl.BlockSpec((1, D_MODEL), whole),
            pl.BlockSpec(memory_space=pl.ANY),
        ],
        out_specs=[pl.BlockSpec((tm, w), row) for w in widths],
        out_shape=[jax.ShapeDtypeStruct((n, w), BF16) for w in widths],
        scratch_shapes=[pltpu.VMEM((D_MODEL, COLS_B), BF16), pltpu.SemaphoreType.DMA(())],
        compiler_params=pltpu.CompilerParams(
            dimension_semantics=("arbitrary",), vmem_limit_bytes=VMEM_LIMIT),
        name="proj_b",
    )(x, gain, w_in)


def _attn_body(sinks_ref, q_ref, k_ref, kp_ref, v_ref, vp_ref, o_ref, ks_ref, vs_ref, *, tq):
    t = pl.program_id(1)
    nb = tq // BLOCK
    lane = lax.broadcasted_iota(jnp.int32, (1, LANES), 1)
    low_head = lane < HEAD_DIM

    def put_kv(dst_ref, rows, v, pad):
        swapped = pltpu.roll(v, HEAD_DIM, 1)
        dst_ref[0, rows, :] = jnp.where(low_head, v, pad).astype(BF16)
        dst_ref[1, rows, :] = jnp.where(low_head, pad, swapped).astype(BF16)
        dst_ref[2, rows, :] = jnp.where(low_head, swapped, pad).astype(BF16)
        dst_ref[3, rows, :] = jnp.where(low_head, pad, v).astype(BF16)

    put_kv(ks_ref, slice(0, BLOCK), kp_ref[...].astype(F32), 0.0)
    put_kv(ks_ref, slice(BLOCK, BLOCK + tq), k_ref[...].astype(F32), 0.0)
    put_kv(vs_ref, slice(0, BLOCK), vp_ref[...].astype(F32), 1.0)
    put_kv(vs_ref, slice(BLOCK, BLOCK + tq), v_ref[...].astype(F32), 1.0)

    rows_q = PAIRS_PER_GROUP * BLOCK
    qi = lax.broadcasted_iota(jnp.int32, (rows_q, BLOCK), 0) % BLOCK
    kj = lax.broadcasted_iota(jnp.int32, (rows_q, BLOCK), 1)
    band_prev = kj > qi + BLOCK - WINDOW
    band_cur = kj <= qi
    sink_fill = {}
    for g in range(N_KV_HEADS):
        for half in range(2):
            sink_rows = jnp.concatenate(
                [jnp.full((BLOCK, LANES), sinks_ref[g * GQA_GROUP + 2 * p + half] * LOG2E, F32)
                 for p in range(PAIRS_PER_GROUP)], axis=0)
            sink_fill[g, half] = jnp.where(kj == 0, sink_rows, -jnp.inf)
    first_key = lax.broadcasted_iota(jnp.int32, (2 * BLOCK, LANES), 0) == 0
    drop_value = (first_key & low_head, first_key & jnp.logical_not(low_head))

    def block(n, carry):
        qrows = pl.ds(pl.multiple_of(n * BLOCK, BLOCK), BLOCK)
        krows = pl.ds(pl.multiple_of(n * BLOCK, BLOCK), 2 * BLOCK)
        valid_prev = band_prev & (t * nb + n > 0)

        def scores(g, half):
            qst = jnp.concatenate(
                [q_ref[qrows, (g * PAIRS_PER_GROUP + p) * LANES:(g * PAIRS_PER_GROUP + p + 1) * LANES]
                 for p in range(PAIRS_PER_GROUP)], axis=0)
            return _dot_nt(qst, ks_ref[2 * g + half, krows, :])

        units = [(g, half) for g in range(N_KV_HEADS) for half in range(2)]
        s_next = scores(*units[0])
        o = []
        for idx, (g, half) in enumerate(units):
            s = s_next
            if idx + 1 < len(units):
                s_next = scores(*units[idx + 1])
            s_prev = jnp.where(valid_prev, s[:, :BLOCK], sink_fill[g, half])
            s_cur = jnp.where(band_cur, s[:, BLOCK:], -jnp.inf)
            m = jnp.max(jnp.maximum(s_prev, s_cur), axis=-1, keepdims=True)
            p_ = jnp.concatenate([jnp.exp2(s_prev - m), jnp.exp2(s_cur - m)], axis=1).astype(BF16)
            vwin = vs_ref[2 * g + half, krows, :]
            vwin = jnp.where(drop_value[half], jnp.zeros_like(vwin), vwin)
            o.append(_dot(p_, vwin))
            if half == 1:
                num = jnp.where(low_head, o[0], o[1])
                den = pltpu.roll(jnp.where(low_head, o[1], o[0]), HEAD_DIM, 1)
                outs = num / den
                o = []
                for p in range(PAIRS_PER_GROUP):
                    c = g * PAIRS_PER_GROUP + p
                    o_ref[qrows, c * LANES:(c + 1) * LANES] = outs[p * BLOCK:(p + 1) * BLOCK].astype(BF16)
        return carry

    lax.fori_loop(0, nb, block, 0)


def _attn(q, k, v, sinks, *, batch, seq, tq):
    n = batch * seq
    nt = seq // tq
    nb = tq // BLOCK
    bps = seq // BLOCK
    cur = lambda b, t: (b * nt + t, 0)
    prev = lambda b, t: (b * bps + jnp.maximum(t * nb - 1, 0), 0)
    return pl.pallas_call(
        functools.partial(_attn_body, tq=tq),
        grid=(batch, nt),
        in_specs=[
            pl.BlockSpec(memory_space=pltpu.SMEM),
            pl.BlockSpec((tq, ATTN_WIDTH), cur),
            pl.BlockSpec((tq, KV_WIDTH), cur),
            pl.BlockSpec((BLOCK, KV_WIDTH), prev),
            pl.BlockSpec((tq, KV_WIDTH), cur),
            pl.BlockSpec((BLOCK, KV_WIDTH), prev),
        ],
        out_specs=pl.BlockSpec((tq, ATTN_WIDTH), cur),
        out_shape=jax.ShapeDtypeStruct((n, ATTN_WIDTH), BF16),
        scratch_shapes=[
            pltpu.VMEM((4, tq + BLOCK, LANES), BF16),
            pltpu.VMEM((4, tq + BLOCK, LANES), BF16),
        ],
        compiler_params=pltpu.CompilerParams(
            dimension_semantics=("parallel", "parallel"), vmem_limit_bytes=VMEM_LIMIT),
        name="attn",
    )(sinks, q, k, k, v, v)


def _hgrn_body(hq_ref, hk_ref, hl_ref, hi_ref, og_ref, gain_ref, o_ref, state_ref, *, tile):
    @pl.when(pl.program_id(1) == 0)
    def _():
        state_ref[...] = jnp.zeros_like(state_ref)

    ri = lax.broadcasted_iota(jnp.int32, (CHUNK, CHUNK), 0)
    ci = lax.broadcasted_iota(jnp.int32, (CHUNK, CHUNK), 1)
    causal = ri >= ci
    tri = jnp.where(causal, 1.0, 0.0).astype(BF16)
    gain = gain_ref[...]

    def chunk(c, carry):
        rows = pl.ds(pl.multiple_of(c * CHUNK, CHUNK), CHUNK)
        lf = hl_ref[rows, :]
        hi = lf.astype(BF16)
        lo = (lf - hi.astype(F32)).astype(BF16)
        b = _dot(tri, hi) + _dot(tri, lo)
        b_mid = b[CHUNK // 2:CHUNK // 2 + 1]
        b_last = b[CHUNK - 1:CHUNK]
        qh = hq_ref[rows, :].astype(F32)
        kk = hk_ref[rows, :].astype(F32)
        q_mid = (qh * jnp.exp2(b - b_mid)).astype(BF16)
        k_mid = (kk * jnp.exp2(b_mid - b)).astype(BF16)
        q_in = (qh * jnp.exp2(b)).astype(BF16)
        k_out = (kk * jnp.exp2(b_last - b)).astype(BF16)
        decay = jnp.exp2(b_last)
        heads = [slice(h * HG_HEAD_DIM, (h + 1) * HG_HEAD_DIM) for h in range(HG_HEADS)]
        att = [_dot_nt(q_mid[:, sl], k_mid[:, sl]) for sl in heads]
        st = [state_ref[h] for h in range(HG_HEADS)]
        o_inter = [_dot_nt(q_in[:, sl], st[h].astype(BF16)) for h, sl in enumerate(heads)]
        upd = [_dot_tn(hi_ref[rows, sl], k_out[:, sl]) for sl in heads]
        o = [o_inter[h] + _dot(jnp.where(causal, att[h], 0.0).astype(BF16), hi_ref[rows, sl])
             for h, sl in enumerate(heads)]
        for h, sl in enumerate(heads):
            state_ref[h] = st[h] * decay[:, sl] + upd[h]
            y = _rms_rows(o[h], gain) * og_ref[rows, sl].astype(F32)
            o_ref[rows, sl] = y.astype(BF16)
        return carry

    lax.fori_loop(0, tile // CHUNK, chunk, 0, unroll=True)


def _hgrn(hq, hk, hl, hi, og, gain, *, batch, seq, tile):
    n = batch * seq
    nt = seq // tile
    cur = lambda b, t: (b * nt + t, 0)
    return pl.pallas_call(
        functools.partial(_hgrn_body, tile=tile),
        grid=(batch, nt),
        in_specs=[pl.BlockSpec((tile, HG_WIDTH), cur)] * 5 + [pl.BlockSpec((1, HG_HEAD_DIM), lambda b, t: (0, 0))],
        out_specs=pl.BlockSpec((tile, HG_WIDTH), cur),
        out_shape=jax.ShapeDtypeStruct((n, HG_WIDTH), BF16),
        scratch_shapes=[pltpu.VMEM((HG_HEADS, HG_HEAD_DIM, HG_HEAD_DIM), F32)],
        compiler_params=pltpu.CompilerParams(
            dimension_semantics=("parallel", "arbitrary"), vmem_limit_bytes=VMEM_LIMIT),
        name="hgrn",
    )(hq, hk, hl, hi, og, gain)


def _merge_body(x_ref, ya_ref, yh_ref, ga_ref, gr_ref, wa_ref, wr_ref, wo_ref, o_ref):
    a = _dot(ya_ref[...], wa_ref[...])
    r = _dot(yh_ref[...], wr_ref[...])
    merged = (ga_ref[...].astype(F32) * a + gr_ref[...].astype(F32) * r).astype(BF16)
    o_ref[...] = x_ref[...] + _dot(merged, wo_ref[...])


def _merge(x, ya, yh, ga, gr, w_a, w_r, w_o, *, tm):
    n = x.shape[0]
    row = lambda i: (i, 0)
    whole = lambda i: (0, 0)
    return pl.pallas_call(
        _merge_body,
        grid=(n // tm,),
        in_specs=[
            pl.BlockSpec((tm, D_MODEL), row),
            pl.BlockSpec((tm, ATTN_WIDTH), row),
            pl.BlockSpec((tm, HG_WIDTH), row),
            pl.BlockSpec((tm, D_MODEL), row),
            pl.BlockSpec((tm, D_MODEL), row),
            pl.BlockSpec((ATTN_WIDTH, D_MODEL), whole, pipeline_mode=pl.Buffered(1)),
            pl.BlockSpec((HG_WIDTH, D_MODEL), whole, pipeline_mode=pl.Buffered(1)),
            pl.BlockSpec((D_MODEL, D_MODEL), whole, pipeline_mode=pl.Buffered(1)),
        ],
        out_specs=pl.BlockSpec((tm, D_MODEL), row),
        out_shape=jax.ShapeDtypeStruct((n, D_MODEL), F32),
        compiler_params=pltpu.CompilerParams(
            dimension_semantics=("parallel",), vmem_limit_bytes=VMEM_LIMIT),
        name="merge",
    )(x, ya, yh, ga, gr, w_a, w_r, w_o)


def _tile(n, want):
    t = min(want, n)
    assert n % t == 0, (n, t)
    return t


def kernel(x, positions, lb_table, ffn1_norm, ffn1_w_gu, ffn1_w_down, mix_norm, w_in, q_norm, k_norm, sinks,
           hg_out_norm, w_attn_branch, w_hg_branch, w_out, ffn2_norm, ffn2_w_gu, ffn2_w_down):
    batch, seq = x.shape[0], x.shape[1]
    n = batch * seq
    depth = w_in.shape[0]
    assert seq % BLOCK == 0 and seq % CHUNK == 0

    half = HEAD_DIM // 2
    inv_freq = (ROPE_THETA ** (-jnp.arange(half, dtype=F32) * 2.0 / HEAD_DIM))[:, None]
    tm_proj = _tile(n, 512)
    pos_rows = positions.astype(F32).reshape(n // tm_proj, 1, tm_proj)

    xf = x.reshape(n, D_MODEL)
    for l in range(depth):
        xf = _ffn(xf, ffn1_norm[l][None, :], ffn1_w_gu[l].astype(BF16), ffn1_w_down[l].astype(BF16),
                  tm=_tile(n, 1024), tf=512)
        gain = mix_norm[l][None, :]
        w_in_l = w_in[l].astype(BF16)
        q, k, v, hq, hk, hl, hi = _proj_a(
            xf, gain, w_in_l, pos_rows, inv_freq, jnp.tile(q_norm[l], 2)[None, :],
            jnp.tile(k_norm[l], 2)[None, :], lb_table, layer=l, tm=tm_proj)
        og, ga, gr = _proj_b(xf, gain, w_in_l, tm=tm_proj)
        ya = _attn(q, k, v, sinks[l], batch=batch, seq=seq, tq=_tile(seq, 512))
        yh = _hgrn(hq, hk, hl, hi, og, hg_out_norm[l][None, :], batch=batch, seq=seq, tile=_tile(seq, 512))
        xf = _merge(xf, ya, yh, ga, gr, w_attn_branch[l].astype(BF16), w_hg_branch[l].astype(BF16),
                    w_out[l].astype(BF16), tm=_tile(n, 256))
        xf = _ffn(xf, ffn2_norm[l][None, :], ffn2_w_gu[l].astype(BF16), ffn2_w_down[l].astype(BF16),
                  tm=_tile(n, 1024), tf=512)
    return xf.reshape(batch, seq, D_MODEL)
```

```python
import functools

import jax
import jax.numpy as jnp
from jax import lax
from jax.experimental import pallas as pl
from jax.experimental.pallas import tpu as pltpu

D_MODEL = 2048
HEAD_DIM = 64
N_Q_HEADS = 16
N_KV_HEADS = 2
GQA_GROUP = N_Q_HEADS // N_KV_HEADS
ATTN_WIDTH = N_Q_HEADS * HEAD_DIM
KV_WIDTH = N_KV_HEADS * HEAD_DIM
WINDOW = 128
BLOCK = 128
ROPE_THETA = 10000.0
HG_HEAD_DIM = 128
HG_HEADS = 8
HG_WIDTH = HG_HEADS * HG_HEAD_DIM
CHUNK = 64
D_FF = 5632
EPS = 1e-6

LANES = 128
PAIRS_PER_GROUP = GQA_GROUP // 2
VMEM_LIMIT = 56 * 1024 * 1024
LOG2E = 1.4426950408889634

N_ATTN = ATTN_WIDTH + 2 * KV_WIDTH
COLS_A = N_ATTN + 3 * HG_WIDTH
COLS_B = HG_WIDTH + 2 * D_MODEL

F32 = jnp.float32
BF16 = jnp.bfloat16


def _dot(a, b):
    return jnp.dot(a, b, preferred_element_type=F32)


def _dot_nt(a, b):
    return lax.dot_general(a, b, (((1,), (1,)), ((), ())), preferred_element_type=F32)


def _dot_tn(a, b):
    return lax.dot_general(a, b, (((0,), (0,)), ((), ())), preferred_element_type=F32)


def _rms_rows(x, gain):
    ms = jnp.mean(x * x, axis=-1, keepdims=True)
    return x * lax.rsqrt(ms + EPS) * gain


FFN_NORM_ROWS = 128
FFN_NORM_FIRST = 2


def _ffn_body(x_hbm, g_ref, wg_ref, wu_ref, wd_ref, o_ref, h0_ref, h1_ref, xs_ref, sem, *, tm, tf):
    i = pl.program_id(0)
    j = pl.program_id(1)
    n_tiles = pl.num_programs(0)
    hbufs = (h0_ref, h1_ref)

    def x_copy(tile):
        rows = pl.ds(pl.multiple_of(tile * tm, tm), tm)
        return pltpu.make_async_copy(x_hbm.at[rows], xs_ref.at[:tm], sem)

    @pl.when((i == 0) & (j == 0))
    def _():
        xs_ref[tm:, :] = jnp.zeros((FFN_NORM_ROWS, D_MODEL), F32)
        x_copy(0).start()
        x_copy(0).wait()
        h0_ref[:tm, :] = _rms_rows(xs_ref[:tm, :], g_ref[...]).astype(BF16)

    @pl.when((j == FFN_NORM_FIRST) & (i + 1 < n_tiles))
    def _():
        x_copy(i + 1).wait()

    def chunk_step(parity, first):
        h = hbufs[parity][:tm, :]
        halves = [slice(c * tf // 2, (c + 1) * tf // 2) for c in range(2)]
        gate_up = [(_dot(h, wg_ref[:, sl]), _dot(h, wu_ref[:, sl])) for sl in halves]
        act = [((g * jax.nn.sigmoid(g)) * (u * 0.5)).astype(BF16) for g, u in gate_up]
        down = _dot(act[0], wd_ref[halves[0], :]) + _dot(act[1], wd_ref[halves[1], :])
        if first:
            o_ref[...] = xs_ref[:tm, :] + down

            @pl.when(i + 1 < n_tiles)
            def _():
                x_copy(i + 1).start()
        else:
            o_ref[...] += down
            part = j - FFN_NORM_FIRST
            live = (part < tm // FFN_NORM_ROWS) & (part >= 0) & (i + 1 < n_tiles)
            rows = pl.ds(pl.multiple_of(jnp.where(live, part * FFN_NORM_ROWS, tm), FFN_NORM_ROWS), FFN_NORM_ROWS)
            hbufs[1 - parity][rows, :] = _rms_rows(xs_ref[rows, :], g_ref[...]).astype(BF16)

    for parity in range(2):
        for first in (True, False):
            pl.when((i % 2 == parity) & ((j == 0) == first))(functools.partial(chunk_step, parity, first))


def _ffn(x, gain, w_gu, w_down, *, tm, tf):
    n = x.shape[0]
    nf = D_FF // tf
    assert tm % FFN_NORM_ROWS == 0 and nf >= FFN_NORM_FIRST + tm // FFN_NORM_ROWS
    return pl.pallas_call(
        functools.partial(_ffn_body, tm=tm, tf=tf),
        grid=(n // tm, nf),
        in_specs=[
            pl.BlockSpec(memory_space=pl.ANY),
            pl.BlockSpec((1, D_MODEL), lambda i, j: (0, 0)),
            pl.BlockSpec((D_MODEL, tf), lambda i, j: (0, j)),
            pl.BlockSpec((D_MODEL, tf), lambda i, j: (0, j + nf)),
            pl.BlockSpec((tf, D_MODEL), lambda i, j: (j, 0)),
        ],
        out_specs=pl.BlockSpec((tm, D_MODEL), lambda i, j: (i, 0)),
        out_shape=jax.ShapeDtypeStruct((n, D_MODEL), F32),
        scratch_shapes=[pltpu.VMEM((tm + FFN_NORM_ROWS, D_MODEL), BF16), pltpu.VMEM((tm + FFN_NORM_ROWS, D_MODEL), BF16),
                        pltpu.VMEM((tm + FFN_NORM_ROWS, D_MODEL), F32), pltpu.SemaphoreType.DMA(())],
        compiler_params=pltpu.CompilerParams(
            dimension_semantics=("arbitrary", "arbitrary"), vmem_limit_bytes=VMEM_LIMIT),
        name="ffn",
    )(x, gain, w_gu, w_gu, w_down)


def _proj_a_body(x_ref, g_ref, w_ref, pos_ref, invf_ref, qg_ref, kg_ref, lbt_ref,
                 q_ref, k_ref, v_ref, hq_ref, hk_ref, hl_ref, hi_ref, *, layer):
    h = _rms_rows(x_ref[...], g_ref[...]).astype(BF16)

    def cols(start, width):
        return _dot(h, w_ref[:, start:start + width])

    lane = lax.broadcasted_iota(jnp.int32, (1, LANES), 1)
    first_half = (lane % HEAD_DIM) < (HEAD_DIM // 2)
    gi = lax.broadcasted_iota(jnp.int32, (LANES, LANES), 0) // HEAD_DIM
    gj = lax.broadcasted_iota(jnp.int32, (LANES, LANES), 1) // HEAD_DIM
    gsum = jnp.where(gi == gj, 1.0, 0.0).astype(BF16)

    def head_norm(v, gain):
        sq = v * v
        hi = sq.astype(BF16)
        lo = (sq - hi.astype(F32)).astype(BF16)
        ss = _dot(hi, gsum) + _dot(lo, gsum)
        return v * lax.rsqrt(ss * (1.0 / HEAD_DIM) + EPS) * gain

    ang = invf_ref[...] * pos_ref[...]
    cos_t = jnp.cos(ang)
    sin_t = jnp.sin(ang)
    reps = LANES // HEAD_DIM
    cosf = jnp.concatenate([cos_t, cos_t] * reps, axis=0).T
    sin_signed = jnp.concatenate([-sin_t, sin_t] * reps, axis=0).T

    def rope(v):
        rot = jnp.where(first_half, pltpu.roll(v, LANES - HEAD_DIM // 2, 1), pltpu.roll(v, HEAD_DIM // 2, 1))
        return v * cosf + rot * sin_signed

    lbt = lbt_ref[...]
    e = jnp.exp(lbt - jnp.max(lbt, axis=0, keepdims=True))
    sm = e / jnp.sum(e, axis=0, keepdims=True)
    lb = jnp.sum(sm[:layer + 1], axis=0, keepdims=True)

    zq = cols(0, ATTN_WIDTH)
    zkv = cols(ATTN_WIDTH, 2 * KV_WIDTH)

    def attn_piece(c):
        if c < ATTN_WIDTH // LANES:
            sl = slice(c * LANES, (c + 1) * LANES)
            q_ref[:, sl] = (rope(head_norm(zq[:, sl], qg_ref[...])) * (HEAD_DIM ** -0.5 * LOG2E)).astype(BF16)
        else:
            k_ref[...] = rope(head_norm(zkv[:, :KV_WIDTH], kg_ref[...])).astype(BF16)
            v_ref[...] = zkv[:, KV_WIDTH:].astype(BF16)

    def hgrn_piece(kind, sl):
        z = cols(N_ATTN + kind * HG_WIDTH + sl.start, sl.stop - sl.start)
        if kind == 0:
            hq_ref[:, sl] = jax.nn.silu(z).astype(BF16)
        elif kind == 1:
            f = lb[:, sl] + (1.0 - lb[:, sl]) * jax.nn.sigmoid(z)
            hk_ref[:, sl] = (1.0 - f).astype(BF16)
            hl_ref[:, sl] = jnp.log2(f)
        else:
            hi_ref[:, sl] = z.astype(BF16)

    chunk = 2 * LANES
    pieces = [(kind, slice(c, c + chunk)) for kind in range(3) for c in range(0, HG_WIDTH, chunk)]
    n_attn_pieces = ATTN_WIDTH // LANES + 1
    for idx, (kind, sl) in enumerate(pieces):
        hgrn_piece(kind, sl)
        if idx < n_attn_pieces:
            attn_piece(idx)


def _proj_a(x, gain, w_a, pos_rows, inv_freq, q_gain, k_gain, lb_table, *, layer, tm):
    n = x.shape[0]
    row = lambda i: (i, 0)
    whole = lambda i: (0, 0)
    widths = (ATTN_WIDTH, KV_WIDTH, KV_WIDTH, HG_WIDTH, HG_WIDTH, HG_WIDTH, HG_WIDTH)
    dtypes = (BF16, BF16, BF16, BF16, BF16, F32, BF16)
    return pl.pallas_call(
        functools.partial(_proj_a_body, layer=layer),
        grid=(n // tm,),
        in_specs=[
            pl.BlockSpec((tm, D_MODEL), row),
            pl.BlockSpec((1, D_MODEL), whole),
            pl.BlockSpec((D_MODEL, COLS_A), whole, pipeline_mode=pl.Buffered(1)),
            pl.BlockSpec((None, 1, tm), lambda i: (i, 0, 0)),
            pl.BlockSpec((HEAD_DIM // 2, 1), whole),
            pl.BlockSpec((1, LANES), whole),
            pl.BlockSpec((1, LANES), whole),
            pl.BlockSpec(lb_table.shape, whole),
        ],
        out_specs=[pl.BlockSpec((tm, w), row) for w in widths],
        out_shape=[jax.ShapeDtypeStruct((n, w), d) for w, d in zip(widths, dtypes)],
        compiler_params=pltpu.CompilerParams(
            dimension_semantics=("parallel",), vmem_limit_bytes=VMEM_LIMIT),
        name="proj_a",
    )(x, gain, w_a, pos_rows, inv_freq, q_gain, k_gain, lb_table)


def _proj_b_body(x_ref, g_ref, w_hbm, og_ref, ga_ref, gr_ref, w_ref, sem):
    @pl.when(pl.program_id(0) == 0)
    def _():
        copy = pltpu.make_async_copy(w_hbm.at[:, COLS_A:], w_ref, sem)
        copy.start()
        copy.wait()

    h = _rms_rows(x_ref[...], g_ref[...]).astype(BF16)

    def cols(start, width):
        return _dot(h, w_ref[:, start:start + width])

    og_ref[...] = jax.nn.silu(cols(0, HG_WIDTH)).astype(BF16)
    for c in range(D_MODEL // HG_WIDTH):
        sl = slice(c * HG_WIDTH, (c + 1) * HG_WIDTH)
        ga_ref[:, sl] = jax.nn.sigmoid(cols(HG_WIDTH + c * HG_WIDTH, HG_WIDTH)).astype(BF16)
        gr_ref[:, sl] = jax.nn.sigmoid(cols(HG_WIDTH + D_MODEL + c * HG_WIDTH, HG_WIDTH)).astype(BF16)


def _proj_b(x, gain, w_in, *, tm):
    n = x.shape[0]
    row = lambda i: (i, 0)
    whole = lambda i: (0, 0)
    widths = (HG_WIDTH, D_MODEL, D_MODEL)
    return pl.pallas_call(
        _proj_b_body,
        grid=(n // tm,),
        in_specs=[
            pl.BlockSpec((tm, D_MODEL), row),
            pl.BlockSpec((1, D_MODEL), whole),
            pl.BlockSpec(memory_space=pl.ANY),
        ],
        out_specs=[pl.BlockSpec((tm, w), row) for w in widths],
        out_shape=[jax.ShapeDtypeStruct((n, w), BF16) for w in widths],
        scratch_shapes=[pltpu.VMEM((D_MODEL, COLS_B), BF16), pltpu.SemaphoreType.DMA(())],
        compiler_params=pltpu.CompilerParams(
            dimension_semantics=("arbitrary",), vmem_limit_bytes=VMEM_LIMIT),
        name="proj_b",
    )(x, gain, w_in)


def _attn_body(sinks_ref, q_ref, k_ref, kp_ref, v_ref, vp_ref, o_ref, ks_ref, vs_ref, *, tq):
    t = pl.program_id(1)
    nb = tq // BLOCK
    lane = lax.broadcasted_iota(jnp.int32, (1, LANES), 1)
    low_head = lane < HEAD_DIM

    def put_kv(dst_ref, rows, v, pad):
        swapped = pltpu.roll(v, HEAD_DIM, 1)
        dst_ref[0, rows, :] = jnp.where(low_head, v, pad).astype(BF16)
        dst_ref[1, rows, :] = jnp.where(low_head, pad, swapped).astype(BF16)
        dst_ref[2, rows, :] = jnp.where(low_head, swapped, pad).astype(BF16)
        dst_ref[3, rows, :] = jnp.where(low_head, pad, v).astype(BF16)

    put_kv(ks_ref, slice(0, BLOCK), kp_ref[...].astype(F32), 0.0)
    put_kv(ks_ref, slice(BLOCK, BLOCK + tq), k_ref[...].astype(F32), 0.0)
    put_kv(vs_ref, slice(0, BLOCK), vp_ref[...].astype(F32), 1.0)
    put_kv(vs_ref, slice(BLOCK, BLOCK + tq), v_ref[...].astype(F32), 1.0)

    rows_q = PAIRS_PER_GROUP * BLOCK
    qi = lax.broadcasted_iota(jnp.int32, (rows_q, BLOCK), 0) % BLOCK
    kj = lax.broadcasted_iota(jnp.int32, (rows_q, BLOCK), 1)
    band_prev = kj > qi + BLOCK - WINDOW
    band_cur = kj <= qi
    sink_fill = {}
    for g in range(N_KV_HEADS):
        for half in range(2):
            sink_rows = jnp.concatenate(
                [jnp.full((BLOCK, LANES), sinks_ref[g * GQA_GROUP + 2 * p + half] * LOG2E, F32)
                 for p in range(PAIRS_PER_GROUP)], axis=0)
            sink_fill[g, half] = jnp.where(kj == 0, sink_rows, -jnp.inf)
    first_key = lax.broadcasted_iota(jnp.int32, (2 * BLOCK, LANES), 0) == 0
    drop_value = (first_key & low_head, first_key & jnp.logical_not(low_head))

    def block(n, carry):
        qrows = pl.ds(pl.multiple_of(n * BLOCK, BLOCK), BLOCK)
        krows = pl.ds(pl.multiple_of(n * BLOCK, BLOCK), 2 * BLOCK)
        valid_prev = band_prev & (t * nb + n > 0)

        def scores(g, half):
            qst = jnp.concatenate(
                [q_ref[qrows, (g * PAIRS_PER_GROUP + p) * LANES:(g * PAIRS_PER_GROUP + p + 1) * LANES]
                 for p in range(PAIRS_PER_GROUP)], axis=0)
            return _dot_nt(qst, ks_ref[2 * g + half, krows, :])

        units = [(g, half) for g in range(N_KV_HEADS) for half in range(2)]
        s_next = scores(*units[0])
        o = []
        for idx, (g, half) in enumerate(units):
            s = s_next
            if idx + 1 < len(units):
                s_next = scores(*units[idx + 1])
            s_prev = jnp.where(valid_prev, s[:, :BLOCK], sink_fill[g, half])
            s_cur = jnp.where(band_cur, s[:, BLOCK:], -jnp.inf)
            m = jnp.max(jnp.maximum(s_prev, s_cur), axis=-1, keepdims=True)
            p_ = jnp.concatenate([jnp.exp2(s_prev - m), jnp.exp2(s_cur - m)], axis=1).astype(BF16)
            vwin = vs_ref[2 * g + half, krows, :]
            vwin = jnp.where(drop_value[half], jnp.zeros_like(vwin), vwin)
            o.append(_dot(p_, vwin))
            if half == 1:
                num = jnp.where(low_head, o[0], o[1])
                den = pltpu.roll(jnp.where(low_head, o[1], o[0]), HEAD_DIM, 1)
                outs = num / den
                o = []
                for p in range(PAIRS_PER_GROUP):
                    c = g * PAIRS_PER_GROUP + p
                    o_ref[qrows, c * LANES:(c + 1) * LANES] = outs[p * BLOCK:(p + 1) * BLOCK].astype(BF16)
        return carry

    lax.fori_loop(0, nb, block, 0)


def _attn(q, k, v, sinks, *, batch, seq, tq):
    n = batch * seq
    nt = seq // tq
    nb = tq // BLOCK
    bps = seq // BLOCK
    cur = lambda b, t: (b * nt + t, 0)
    prev = lambda b, t: (b * bps + jnp.maximum(t * nb - 1, 0), 0)
    return pl.pallas_call(
        functools.partial(_attn_body, tq=tq),
        grid=(batch, nt),
        in_specs=[
            pl.BlockSpec(memory_space=pltpu.SMEM),
            pl.BlockSpec((tq, ATTN_WIDTH), cur),
            pl.BlockSpec((tq, KV_WIDTH), cur),
            pl.BlockSpec((BLOCK, KV_WIDTH), prev),
            pl.BlockSpec((tq, KV_WIDTH), cur),
            pl.BlockSpec((BLOCK, KV_WIDTH), prev),
        ],
        out_specs=pl.BlockSpec((tq, ATTN_WIDTH), cur),
        out_shape=jax.ShapeDtypeStruct((n, ATTN_WIDTH), BF16),
        scratch_shapes=[
            pltpu.VMEM((4, tq + BLOCK, LANES), BF16),
            pltpu.VMEM((4, tq + BLOCK, LANES), BF16),
        ],
        compiler_params=pltpu.CompilerParams(
            dimension_semantics=("parallel", "parallel"), vmem_limit_bytes=VMEM_LIMIT),
        name="attn",
    )(sinks, q, k, k, v, v)


def _hgrn_body(hq_ref, hk_ref, hl_ref, hi_ref, og_ref, gain_ref, o_ref, state_ref, *, tile):
    @pl.when(pl.program_id(1) == 0)
    def _():
        state_ref[...] = jnp.zeros_like(state_ref)

    ri = lax.broadcasted_iota(jnp.int32, (CHUNK, CHUNK), 0)
    ci = lax.broadcasted_iota(jnp.int32, (CHUNK, CHUNK), 1)
    causal = ri >= ci
    tri = jnp.where(causal, 1.0, 0.0).astype(BF16)
    gain = gain_ref[...]

    def chunk(c, carry):
        rows = pl.ds(pl.multiple_of(c * CHUNK, CHUNK), CHUNK)
        lf = hl_ref[rows, :]
        hi = lf.astype(BF16)
        lo = (lf - hi.astype(F32)).astype(BF16)
        b = _dot(tri, hi) + _dot(tri, lo)
        b_mid = b[CHUNK // 2:CHUNK // 2 + 1]
        b_last = b[CHUNK - 1:CHUNK]
        qh = hq_ref[rows, :].astype(F32)
        kk = hk_ref[rows, :].astype(F32)
        q_mid = (qh * jnp.exp2(b - b_mid)).astype(BF16)
        k_mid = (kk * jnp.exp2(b_mid - b)).astype(BF16)
        q_in = (qh * jnp.exp2(b)).astype(BF16)
        k_out = (kk * jnp.exp2(b_last - b)).astype(BF16)
        decay = jnp.exp2(b_last)
        heads = [slice(h * HG_HEAD_DIM, (h + 1) * HG_HEAD_DIM) for h in range(HG_HEADS)]
        att = [_dot_nt(q_mid[:, sl], k_mid[:, sl]) for sl in heads]
        st = [state_ref[h] for h in range(HG_HEADS)]
        o_inter = [_dot_nt(q_in[:, sl], st[h].astype(BF16)) for h, sl in enumerate(heads)]
        upd = [_dot_tn(hi_ref[rows, sl], k_out[:, sl]) for sl in heads]
        o = [o_inter[h] + _dot(jnp.where(causal, att[h], 0.0).astype(BF16), hi_ref[rows, sl])
             for h, sl in enumerate(heads)]
        for h, sl in enumerate(heads):
            state_ref[h] = st[h] * decay[:, sl] + upd[h]
            y = _rms_rows(o[h], gain) * og_ref[rows, sl].astype(F32)
            o_ref[rows, sl] = y.astype(BF16)
        return carry

    lax.fori_loop(0, tile // CHUNK, chunk, 0, unroll=True)


def _hgrn(hq, hk, hl, hi, og, gain, *, batch, seq, tile):
    n = batch * seq
    nt = seq // tile
    cur = lambda b, t: (b * nt + t, 0)
    return pl.pallas_call(
        functools.partial(_hgrn_body, tile=tile),
        grid=(batch, nt),
        in_specs=[pl.BlockSpec((tile, HG_WIDTH), cur)] * 5 + [pl.BlockSpec((1, HG_HEAD_DIM), lambda b, t: (0, 0))],
        out_specs=pl.BlockSpec((tile, HG_WIDTH), cur),
        out_shape=jax.ShapeDtypeStruct((n, HG_WIDTH), BF16),
        scratch_shapes=[pltpu.VMEM((HG_HEADS, HG_HEAD_DIM, HG_HEAD_DIM), F32)],
        compiler_params=pltpu.CompilerParams(
            dimension_semantics=("parallel", "arbitrary"), vmem_limit_bytes=VMEM_LIMIT),
        name="hgrn",
    )(hq, hk, hl, hi, og, gain)


BF16_SUBLANES = 16


def _cast_block_rows(rows, steps):
    for rb in range(BF16_SUBLANES, rows + 1, BF16_SUBLANES):
        if rows % rb == 0 and rows // rb <= steps:
            return rb
    raise ValueError((rows, steps))


def _merge_body(x_ref, ya_ref, yh_ref, ga_ref, gr_ref, wa_ref, wr_ref, wo_ref, *rest):
    n_casts = (len(rest) - 1) // 2
    cast_src, o_ref, cast_dst = rest[:n_casts], rest[n_casts], rest[n_casts + 1:]
    a = _dot(ya_ref[...], wa_ref[...])
    r = _dot(yh_ref[...], wr_ref[...])
    merged = (ga_ref[...].astype(F32) * a + gr_ref[...].astype(F32) * r).astype(BF16)
    o_ref[...] = x_ref[...] + _dot(merged, wo_ref[...])
    for src_ref, dst_ref in zip(cast_src, cast_dst):
        dst_ref[...] = src_ref[...].astype(BF16)


def _merge(x, ya, yh, ga, gr, w_a, w_r, w_o, casts=(), *, tm):
    n = x.shape[0]
    row = lambda i: (i, 0)
    whole = lambda i: (0, 0)
    cast_specs = []
    for w in casts:
        rb = _cast_block_rows(w.shape[0], n // tm)
        cast_specs.append(pl.BlockSpec(
            (rb, w.shape[1]), functools.partial(lambda i, last: (jnp.minimum(i, last), 0), last=w.shape[0] // rb - 1)))
    outs = pl.pallas_call(
        _merge_body,
        grid=(n // tm,),
        in_specs=[
            pl.BlockSpec((tm, D_MODEL), row),
            pl.BlockSpec((tm, ATTN_WIDTH), row),
            pl.BlockSpec((tm, HG_WIDTH), row),
            pl.BlockSpec((tm, D_MODEL), row),
            pl.BlockSpec((tm, D_MODEL), row),
            pl.BlockSpec((ATTN_WIDTH, D_MODEL), whole, pipeline_mode=pl.Buffered(1)),
            pl.BlockSpec((HG_WIDTH, D_MODEL), whole, pipeline_mode=pl.Buffered(1)),
            pl.BlockSpec((D_MODEL, D_MODEL), whole, pipeline_mode=pl.Buffered(1)),
        ] + cast_specs,
        out_specs=[pl.BlockSpec((tm, D_MODEL), row)] + cast_specs,
        out_shape=[jax.ShapeDtypeStruct((n, D_MODEL), F32)] + [jax.ShapeDtypeStruct(w.shape, BF16) for w in casts],
        compiler_params=pltpu.CompilerParams(
            dimension_semantics=("arbitrary",), vmem_limit_bytes=VMEM_LIMIT),
        name="merge",
    )(x, ya, yh, ga, gr, w_a, w_r, w_o, *casts)
    return outs[0], outs[1:]


def _tile(n, want):
    t = min(want, n)
    assert n % t == 0, (n, t)
    return t


def kernel(x, positions, lb_table, ffn1_norm, ffn1_w_gu, ffn1_w_down, mix_norm, w_in, q_norm, k_norm, sinks,
           hg_out_norm, w_attn_branch, w_hg_branch, w_out, ffn2_norm, ffn2_w_gu, ffn2_w_down):
    batch, seq = x.shape[0], x.shape[1]
    n = batch * seq
    depth = w_in.shape[0]
    assert seq % BLOCK == 0 and seq % CHUNK == 0

    half = HEAD_DIM // 2
    inv_freq = (ROPE_THETA ** (-jnp.arange(half, dtype=F32) * 2.0 / HEAD_DIM))[:, None]
    tm_proj = _tile(n, 512)
    pos_rows = positions.astype(F32).reshape(n // tm_proj, 1, tm_proj)

    xf = x.reshape(n, D_MODEL)
    for l in range(depth):
        xf = _ffn(xf, ffn1_norm[l][None, :], ffn1_w_gu[l].astype(BF16), ffn1_w_down[l].astype(BF16),
                  tm=_tile(n, 1024), tf=512)
        gain = mix_norm[l][None, :]
        w_in_l = w_in[l].astype(BF16)
        q, k, v, hq, hk, hl, hi = _proj_a(
            xf, gain, w_in_l, pos_rows, inv_freq, jnp.tile(q_norm[l], 2)[None, :],
            jnp.tile(k_norm[l], 2)[None, :], lb_table, layer=l, tm=tm_proj)
        og, ga, gr = _proj_b(xf, gain, w_in_l, tm=tm_proj)
        ya = _attn(q, k, v, sinks[l], batch=batch, seq=seq, tq=_tile(seq, 512))
        yh = _hgrn(hq, hk, hl, hi, og, hg_out_norm[l][None, :], batch=batch, seq=seq, tile=_tile(seq, 512))
        xf, (w_gu2, w_down2) = _merge(
            xf, ya, yh, ga, gr, w_attn_branch[l].astype(BF16), w_hg_branch[l].astype(BF16), w_out[l].astype(BF16),
            casts=(ffn2_w_gu[l], ffn2_w_down[l]), tm=_tile(n, 256))
        xf = _ffn(xf, ffn2_norm[l][None, :], w_gu2, w_down2, tm=_tile(n, 1024), tf=512)
    return xf.reshape(batch, seq, D_MODEL)
```

```python
import functools

import jax
import jax.numpy as jnp
from jax import lax
from jax.experimental import pallas as pl
from jax.experimental.pallas import tpu as pltpu

D_MODEL = 2048
HEAD_DIM = 64
N_Q_HEADS = 16
N_KV_HEADS = 2
GQA_GROUP = N_Q_HEADS // N_KV_HEADS
ATTN_WIDTH = N_Q_HEADS * HEAD_DIM
KV_WIDTH = N_KV_HEADS * HEAD_DIM
WINDOW = 128
BLOCK = 128
ROPE_THETA = 10000.0
HG_HEAD_DIM = 128
HG_HEADS = 8
HG_WIDTH = HG_HEADS * HG_HEAD_DIM
CHUNK = 64
D_FF = 5632
EPS = 1e-6

LANES = 128
PAIRS_PER_GROUP = GQA_GROUP // 2
VMEM_LIMIT = 58 * 1024 * 1024
LOG2E = 1.4426950408889634

N_ATTN = ATTN_WIDTH + 2 * KV_WIDTH
COLS_A = N_ATTN + 3 * HG_WIDTH
COLS_B = HG_WIDTH + 2 * D_MODEL

F32 = jnp.float32
BF16 = jnp.bfloat16


def _dot(a, b):
    return jnp.dot(a, b, preferred_element_type=F32)


def _dot_nt(a, b):
    return lax.dot_general(a, b, (((1,), (1,)), ((), ())), preferred_element_type=F32)


def _dot_tn(a, b):
    return lax.dot_general(a, b, (((0,), (0,)), ((), ())), preferred_element_type=F32)


def _sigmoid(z):
    return 0.5 * jnp.tanh(0.5 * z) + 0.5


def _silu(z):
    half = 0.5 * z
    return half * jnp.tanh(half) + half


def _rms_rows(x, gain):
    ms = jnp.mean(x * x, axis=-1, keepdims=True)
    return x * lax.rsqrt(ms + EPS) * gain


FFN_NORM_ROWS = 128
FFN_NORM_FIRST = 2


def _ffn_body(x_hbm, g_ref, wg_ref, wu_ref, wd_ref, o_ref, h0_ref, h1_ref, xs_ref, sem, *, tm, tf):
    i = pl.program_id(0)
    j = pl.program_id(1)
    n_tiles = pl.num_programs(0)
    hbufs = (h0_ref, h1_ref)

    def x_copy(tile):
        rows = pl.ds(pl.multiple_of(tile * tm, tm), tm)
        return pltpu.make_async_copy(x_hbm.at[rows], xs_ref.at[:tm], sem)

    @pl.when((i == 0) & (j == 0))
    def _():
        xs_ref[tm:, :] = jnp.zeros((FFN_NORM_ROWS, D_MODEL), F32)
        x_copy(0).start()
        x_copy(0).wait()
        h0_ref[:tm, :] = _rms_rows(xs_ref[:tm, :], g_ref[...]).astype(BF16)

    @pl.when((j == FFN_NORM_FIRST) & (i + 1 < n_tiles))
    def _():
        x_copy(i + 1).wait()

    def chunk_step(parity, first):
        h = hbufs[parity][:tm, :]
        halves = [slice(c * tf // 2, (c + 1) * tf // 2) for c in range(2)]
        gate_up = [(_dot(h, wg_ref[:, sl]), _dot(h, wu_ref[:, sl])) for sl in halves]
        act = [(_silu(g) * (u * 0.5)).astype(BF16) for g, u in gate_up]
        down = _dot(act[0], wd_ref[halves[0], :]) + _dot(act[1], wd_ref[halves[1], :])
        if first:
            o_ref[...] = xs_ref[:tm, :] + down

            @pl.when(i + 1 < n_tiles)
            def _():
                x_copy(i + 1).start()
        else:
            o_ref[...] += down
            part = j - FFN_NORM_FIRST
            live = (part < tm // FFN_NORM_ROWS) & (part >= 0) & (i + 1 < n_tiles)
            rows = pl.ds(pl.multiple_of(jnp.where(live, part * FFN_NORM_ROWS, tm), FFN_NORM_ROWS), FFN_NORM_ROWS)
            hbufs[1 - parity][rows, :] = _rms_rows(xs_ref[rows, :], g_ref[...]).astype(BF16)

    for parity in range(2):
        for first in (True, False):
            pl.when((i % 2 == parity) & ((j == 0) == first))(functools.partial(chunk_step, parity, first))


def _ffn(x, gain, w_gu, w_down, *, tm, tf):
    n = x.shape[0]
    nf = D_FF // tf
    assert tm % FFN_NORM_ROWS == 0 and nf >= FFN_NORM_FIRST + tm // FFN_NORM_ROWS
    return pl.pallas_call(
        functools.partial(_ffn_body, tm=tm, tf=tf),
        grid=(n // tm, nf),
        in_specs=[
            pl.BlockSpec(memory_space=pl.ANY),
            pl.BlockSpec((1, D_MODEL), lambda i, j: (0, 0)),
            pl.BlockSpec((D_MODEL, tf), lambda i, j: (0, j)),
            pl.BlockSpec((D_MODEL, tf), lambda i, j: (0, j + nf)),
            pl.BlockSpec((tf, D_MODEL), lambda i, j: (j, 0)),
        ],
        out_specs=pl.BlockSpec((tm, D_MODEL), lambda i, j: (i, 0)),
        out_shape=jax.ShapeDtypeStruct((n, D_MODEL), F32),
        scratch_shapes=[pltpu.VMEM((tm + FFN_NORM_ROWS, D_MODEL), BF16), pltpu.VMEM((tm + FFN_NORM_ROWS, D_MODEL), BF16),
                        pltpu.VMEM((tm + FFN_NORM_ROWS, D_MODEL), F32), pltpu.SemaphoreType.DMA(())],
        compiler_params=pltpu.CompilerParams(
            dimension_semantics=("arbitrary", "arbitrary"), vmem_limit_bytes=VMEM_LIMIT),
        name="ffn",
    )(x, gain, w_gu, w_gu, w_down)


def _proj_a_body(x_ref, g_ref, w_ref, pos_ref, invf_ref, qg_ref, kg_ref, lbt_ref,
                 q_ref, k_ref, v_ref, hq_ref, hk_ref, hl_ref, hi_ref, h_ref, *, layer):
    h = _rms_rows(x_ref[...], g_ref[...]).astype(BF16)
    h_ref[...] = h

    def cols(start, width):
        return _dot(h, w_ref[:, start:start + width])

    lane = lax.broadcasted_iota(jnp.int32, (1, LANES), 1)
    first_half = (lane % HEAD_DIM) < (HEAD_DIM // 2)
    gi = lax.broadcasted_iota(jnp.int32, (LANES, LANES), 0) // HEAD_DIM
    gj = lax.broadcasted_iota(jnp.int32, (LANES, LANES), 1) // HEAD_DIM
    gsum = jnp.where(gi == gj, 1.0, 0.0).astype(BF16)

    def head_norm(v, gain):
        sq = v * v
        hi = sq.astype(BF16)
        lo = (sq - hi.astype(F32)).astype(BF16)
        ss = _dot(hi, gsum) + _dot(lo, gsum)
        return v * lax.rsqrt(ss * (1.0 / HEAD_DIM) + EPS) * gain

    ang = invf_ref[...] * pos_ref[...]
    cos_t = jnp.cos(ang)
    sin_t = jnp.sin(ang)
    reps = LANES // HEAD_DIM
    cosf = jnp.concatenate([cos_t, cos_t] * reps, axis=0).T
    sin_signed = jnp.concatenate([-sin_t, sin_t] * reps, axis=0).T

    def rope(v):
        rot = jnp.where(first_half, pltpu.roll(v, LANES - HEAD_DIM // 2, 1), pltpu.roll(v, HEAD_DIM // 2, 1))
        return v * cosf + rot * sin_signed

    lbt = lbt_ref[...]
    e = jnp.exp(lbt - jnp.max(lbt, axis=0, keepdims=True))
    sm = e / jnp.sum(e, axis=0, keepdims=True)
    lb = jnp.sum(sm[:layer + 1], axis=0, keepdims=True)

    zq = cols(0, ATTN_WIDTH)
    zkv = cols(ATTN_WIDTH, 2 * KV_WIDTH)

    def attn_piece(c):
        if c < ATTN_WIDTH // LANES:
            sl = slice(c * LANES, (c + 1) * LANES)
            q_ref[:, sl] = (rope(head_norm(zq[:, sl], qg_ref[...])) * (HEAD_DIM ** -0.5 * LOG2E)).astype(BF16)
        else:
            k_ref[...] = rope(head_norm(zkv[:, :KV_WIDTH], kg_ref[...])).astype(BF16)
            v_ref[...] = zkv[:, KV_WIDTH:].astype(BF16)

    def hgrn_piece(kind, sl):
        z = cols(N_ATTN + kind * HG_WIDTH + sl.start, sl.stop - sl.start)
        if kind == 0:
            hq_ref[:, sl] = _silu(z).astype(BF16)
        elif kind == 1:
            f = lb[:, sl] + (1.0 - lb[:, sl]) * _sigmoid(z)
            hk_ref[:, sl] = (1.0 - f).astype(BF16)
            hl_ref[:, sl] = jnp.log2(f)
        else:
            hi_ref[:, sl] = z.astype(BF16)

    chunk = 2 * LANES
    pieces = [(kind, slice(c, c + chunk)) for kind in range(3) for c in range(0, HG_WIDTH, chunk)]
    n_attn_pieces = ATTN_WIDTH // LANES + 1
    for idx, (kind, sl) in enumerate(pieces):
        hgrn_piece(kind, sl)
        if idx < n_attn_pieces:
            attn_piece(idx)


def _proj_a(x, gain, w_a, pos_rows, inv_freq, q_gain, k_gain, lb_table, *, layer, tm):
    n = x.shape[0]
    row = lambda i: (i, 0)
    whole = lambda i: (0, 0)
    widths = (ATTN_WIDTH, KV_WIDTH, KV_WIDTH, HG_WIDTH, HG_WIDTH, HG_WIDTH, HG_WIDTH, D_MODEL)
    dtypes = (BF16, BF16, BF16, BF16, BF16, F32, BF16, BF16)
    return pl.pallas_call(
        functools.partial(_proj_a_body, layer=layer),
        grid=(n // tm,),
        in_specs=[
            pl.BlockSpec((tm, D_MODEL), row),
            pl.BlockSpec((1, D_MODEL), whole),
            pl.BlockSpec((D_MODEL, COLS_A), whole, pipeline_mode=pl.Buffered(1)),
            pl.BlockSpec((None, 1, tm), lambda i: (i, 0, 0)),
            pl.BlockSpec((HEAD_DIM // 2, 1), whole),
            pl.BlockSpec((1, LANES), whole),
            pl.BlockSpec((1, LANES), whole),
            pl.BlockSpec(lb_table.shape, whole),
        ],
        out_specs=[pl.BlockSpec((tm, w), row) for w in widths],
        out_shape=[jax.ShapeDtypeStruct((n, w), d) for w, d in zip(widths, dtypes)],
        compiler_params=pltpu.CompilerParams(
            dimension_semantics=("parallel",), vmem_limit_bytes=VMEM_LIMIT),
        name="proj_a",
    )(x, gain, w_a, pos_rows, inv_freq, q_gain, k_gain, lb_table)


def _proj_b_body(h_ref, w_hbm, og_ref, ga_ref, gr_ref, w_ref, sem):
    @pl.when(pl.program_id(0) == 0)
    def _():
        copy = pltpu.make_async_copy(w_hbm.at[:, COLS_A:], w_ref, sem)
        copy.start()
        copy.wait()

    def cols(start, width):
        return _dot(h_ref[...], w_ref[:, start:start + width])

    og_ref[...] = _silu(cols(0, HG_WIDTH)).astype(BF16)
    for c in range(D_MODEL // HG_WIDTH):
        sl = slice(c * HG_WIDTH, (c + 1) * HG_WIDTH)
        ga_ref[:, sl] = _sigmoid(cols(HG_WIDTH + c * HG_WIDTH, HG_WIDTH)).astype(BF16)
        gr_ref[:, sl] = _sigmoid(cols(HG_WIDTH + D_MODEL + c * HG_WIDTH, HG_WIDTH)).astype(BF16)


def _proj_b(h, w_in, *, tm):
    n = h.shape[0]
    row = lambda i: (i, 0)
    widths = (HG_WIDTH, D_MODEL, D_MODEL)
    return pl.pallas_call(
        _proj_b_body,
        grid=(n // tm,),
        in_specs=[
            pl.BlockSpec((tm, D_MODEL), row),
            pl.BlockSpec(memory_space=pl.ANY),
        ],
        out_specs=[pl.BlockSpec((tm, w), row) for w in widths],
        out_shape=[jax.ShapeDtypeStruct((n, w), BF16) for w in widths],
        scratch_shapes=[pltpu.VMEM((D_MODEL, COLS_B), BF16), pltpu.SemaphoreType.DMA(())],
        compiler_params=pltpu.CompilerParams(
            dimension_semantics=("arbitrary",), vmem_limit_bytes=VMEM_LIMIT),
        name="proj_b",
    )(h, w_in)


def _attn_body(sinks_ref, q_ref, k_ref, kp_ref, v_ref, vp_ref, o_ref, ks_ref, vs_ref, *, tq):
    t = pl.program_id(1)
    nb = tq // BLOCK
    lane = lax.broadcasted_iota(jnp.int32, (1, LANES), 1)
    low_head = lane < HEAD_DIM

    def put_kv(dst_ref, rows, v, pad):
        swapped = pltpu.roll(v, HEAD_DIM, 1)
        dst_ref[0, rows, :] = jnp.where(low_head, v, pad).astype(BF16)
        dst_ref[1, rows, :] = jnp.where(low_head, pad, swapped).astype(BF16)
        dst_ref[2, rows, :] = jnp.where(low_head, swapped, pad).astype(BF16)
        dst_ref[3, rows, :] = jnp.where(low_head, pad, v).astype(BF16)

    put_kv(ks_ref, slice(0, BLOCK), kp_ref[...].astype(F32), 0.0)
    put_kv(ks_ref, slice(BLOCK, BLOCK + tq), k_ref[...].astype(F32), 0.0)
    put_kv(vs_ref, slice(0, BLOCK), vp_ref[...].astype(F32), 1.0)
    put_kv(vs_ref, slice(BLOCK, BLOCK + tq), v_ref[...].astype(F32), 1.0)

    rows_q = PAIRS_PER_GROUP * BLOCK
    qi = lax.broadcasted_iota(jnp.int32, (rows_q, BLOCK), 0) % BLOCK
    kj = lax.broadcasted_iota(jnp.int32, (rows_q, BLOCK), 1)
    band_prev = kj > qi + BLOCK - WINDOW
    band_cur = kj <= qi
    sink_fill = {}
    for g in range(N_KV_HEADS):
        for half in range(2):
            sink_rows = jnp.concatenate(
                [jnp.full((BLOCK, LANES), sinks_ref[g * GQA_GROUP + 2 * p + half] * LOG2E, F32)
                 for p in range(PAIRS_PER_GROUP)], axis=0)
            sink_fill[g, half] = jnp.where(kj == 0, sink_rows, -jnp.inf)
    first_key = lax.broadcasted_iota(jnp.int32, (2 * BLOCK, LANES), 0) == 0
    drop_value = (first_key & low_head, first_key & jnp.logical_not(low_head))

    def block(n, carry):
        qrows = pl.ds(pl.multiple_of(n * BLOCK, BLOCK), BLOCK)
        krows = pl.ds(pl.multiple_of(n * BLOCK, BLOCK), 2 * BLOCK)
        valid_prev = band_prev & (t * nb + n > 0)

        def scores(g, half):
            qst = jnp.concatenate(
                [q_ref[qrows, (g * PAIRS_PER_GROUP + p) * LANES:(g * PAIRS_PER_GROUP + p + 1) * LANES]
                 for p in range(PAIRS_PER_GROUP)], axis=0)
            return _dot_nt(qst, ks_ref[2 * g + half, krows, :])

        units = [(g, half) for g in range(N_KV_HEADS) for half in range(2)]
        s_next = scores(*units[0])
        o = []
        for idx, (g, half) in enumerate(units):
            s = s_next
            if idx + 1 < len(units):
                s_next = scores(*units[idx + 1])
            s_prev = jnp.where(valid_prev, s[:, :BLOCK], sink_fill[g, half])
            s_cur = jnp.where(band_cur, s[:, BLOCK:], -jnp.inf)
            m = jnp.max(jnp.maximum(s_prev, s_cur), axis=-1, keepdims=True)
            p_ = jnp.concatenate([jnp.exp2(s_prev - m), jnp.exp2(s_cur - m)], axis=1).astype(BF16)
            vwin = vs_ref[2 * g + half, krows, :]
            vwin = jnp.where(drop_value[half], jnp.zeros_like(vwin), vwin)
            o.append(_dot(p_, vwin))
            if half == 1:
                num = jnp.where(low_head, o[0], o[1])
                den = pltpu.roll(jnp.where(low_head, o[1], o[0]), HEAD_DIM, 1)
                outs = num / den
                o = []
                for p in range(PAIRS_PER_GROUP):
                    c = g * PAIRS_PER_GROUP + p
                    o_ref[qrows, c * LANES:(c + 1) * LANES] = outs[p * BLOCK:(p + 1) * BLOCK].astype(BF16)
        return carry

    lax.fori_loop(0, nb, block, 0)


def _attn(q, k, v, sinks, *, batch, seq, tq):
    n = batch * seq
    nt = seq // tq
    nb = tq // BLOCK
    bps = seq // BLOCK
    cur = lambda b, t: (b * nt + t, 0)
    prev = lambda b, t: (b * bps + jnp.maximum(t * nb - 1, 0), 0)
    return pl.pallas_call(
        functools.partial(_attn_body, tq=tq),
        grid=(batch, nt),
        in_specs=[
            pl.BlockSpec(memory_space=pltpu.SMEM),
            pl.BlockSpec((tq, ATTN_WIDTH), cur),
            pl.BlockSpec((tq, KV_WIDTH), cur),
            pl.BlockSpec((BLOCK, KV_WIDTH), prev),
            pl.BlockSpec((tq, KV_WIDTH), cur),
            pl.BlockSpec((BLOCK, KV_WIDTH), prev),
        ],
        out_specs=pl.BlockSpec((tq, ATTN_WIDTH), cur),
        out_shape=jax.ShapeDtypeStruct((n, ATTN_WIDTH), BF16),
        scratch_shapes=[
            pltpu.VMEM((4, tq + BLOCK, LANES), BF16),
            pltpu.VMEM((4, tq + BLOCK, LANES), BF16),
        ],
        compiler_params=pltpu.CompilerParams(
            dimension_semantics=("parallel", "parallel"), vmem_limit_bytes=VMEM_LIMIT),
        name="attn",
    )(sinks, q, k, k, v, v)


def _hgrn_body(hq_ref, hk_ref, hl_ref, hi_ref, og_ref, gain_ref, o_ref, state_ref, *, tile):
    @pl.when(pl.program_id(1) == 0)
    def _():
        state_ref[...] = jnp.zeros_like(state_ref)

    ri = lax.broadcasted_iota(jnp.int32, (CHUNK, CHUNK), 0)
    ci = lax.broadcasted_iota(jnp.int32, (CHUNK, CHUNK), 1)
    causal = ri >= ci
    tri = jnp.where(causal, 1.0, 0.0).astype(BF16)
    gain = gain_ref[...]

    def chunk(c, carry):
        rows = pl.ds(pl.multiple_of(c * CHUNK, CHUNK), CHUNK)
        lf = hl_ref[rows, :]
        hi = lf.astype(BF16)
        lo = (lf - hi.astype(F32)).astype(BF16)
        b = _dot(tri, hi) + _dot(tri, lo)
        b_mid = b[CHUNK // 2:CHUNK // 2 + 1]
        b_last = b[CHUNK - 1:CHUNK]
        qh = hq_ref[rows, :].astype(F32)
        kk = hk_ref[rows, :].astype(F32)
        q_mid = (qh * jnp.exp2(b - b_mid)).astype(BF16)
        k_mid = (kk * jnp.exp2(b_mid - b)).astype(BF16)
        q_in = (qh * jnp.exp2(b)).astype(BF16)
        k_out = (kk * jnp.exp2(b_last - b)).astype(BF16)
        decay = jnp.exp2(b_last)
        heads = [slice(h * HG_HEAD_DIM, (h + 1) * HG_HEAD_DIM) for h in range(HG_HEADS)]
        att = [_dot_nt(q_mid[:, sl], k_mid[:, sl]) for sl in heads]
        st = [state_ref[h] for h in range(HG_HEADS)]
        o_inter = [_dot_nt(q_in[:, sl], st[h].astype(BF16)) for h, sl in enumerate(heads)]
        upd = [_dot_tn(hi_ref[rows, sl], k_out[:, sl]) for sl in heads]
        o = [o_inter[h] + _dot(jnp.where(causal, att[h], 0.0).astype(BF16), hi_ref[rows, sl])
             for h, sl in enumerate(heads)]
        for h, sl in enumerate(heads):
            state_ref[h] = st[h] * decay[:, sl] + upd[h]
            y = _rms_rows(o[h], gain) * og_ref[rows, sl].astype(F32)
            o_ref[rows, sl] = y.astype(BF16)
        return carry

    lax.fori_loop(0, tile // CHUNK, chunk, 0, unroll=True)


def _hgrn(hq, hk, hl, hi, og, gain, *, batch, seq, tile):
    n = batch * seq
    nt = seq // tile
    cur = lambda b, t: (b * nt + t, 0)
    return pl.pallas_call(
        functools.partial(_hgrn_body, tile=tile),
        grid=(batch, nt),
        in_specs=[pl.BlockSpec((tile, HG_WIDTH), cur)] * 5 + [pl.BlockSpec((1, HG_HEAD_DIM), lambda b, t: (0, 0))],
        out_specs=pl.BlockSpec((tile, HG_WIDTH), cur),
        out_shape=jax.ShapeDtypeStruct((n, HG_WIDTH), BF16),
        scratch_shapes=[pltpu.VMEM((HG_HEADS, HG_HEAD_DIM, HG_HEAD_DIM), F32)],
        compiler_params=pltpu.CompilerParams(
            dimension_semantics=("parallel", "arbitrary"), vmem_limit_bytes=VMEM_LIMIT),
        name="hgrn",
    )(hq, hk, hl, hi, og, gain)


BF16_SUBLANES = 16


def _cast_block_rows(rows, steps):
    for rb in range(BF16_SUBLANES, rows + 1, BF16_SUBLANES):
        if rows % rb == 0 and rows // rb <= steps:
            return rb
    raise ValueError((rows, steps))


def _merge_body(x_ref, ya_ref, yh_ref, ga_ref, gr_ref, wa_ref, wr_ref, wo_ref, *rest):
    n_casts = (len(rest) - 1) // 2
    cast_src, o_ref, cast_dst = rest[:n_casts], rest[n_casts], rest[n_casts + 1:]
    a = _dot(ya_ref[...], wa_ref[...])
    r = _dot(yh_ref[...], wr_ref[...])
    merged = (ga_ref[...].astype(F32) * a + gr_ref[...].astype(F32) * r).astype(BF16)
    o_ref[...] = x_ref[...] + _dot(merged, wo_ref[...])
    for src_ref, dst_ref in zip(cast_src, cast_dst):
        dst_ref[...] = src_ref[...].astype(BF16)


def _merge(x, ya, yh, ga, gr, w_a, w_r, w_o, casts=(), *, tm):
    n = x.shape[0]
    row = lambda i: (i, 0)
    whole = lambda i: (0, 0)
    cast_specs = []
    for w in casts:
        rb = _cast_block_rows(w.shape[0], n // tm)
        cast_specs.append(pl.BlockSpec(
            (rb, w.shape[1]), functools.partial(lambda i, last: (jnp.minimum(i, last), 0), last=w.shape[0] // rb - 1)))
    outs = pl.pallas_call(
        _merge_body,
        grid=(n // tm,),
        in_specs=[
            pl.BlockSpec((tm, D_MODEL), row),
            pl.BlockSpec((tm, ATTN_WIDTH), row),
            pl.BlockSpec((tm, HG_WIDTH), row),
            pl.BlockSpec((tm, D_MODEL), row),
            pl.BlockSpec((tm, D_MODEL), row),
            pl.BlockSpec((ATTN_WIDTH, D_MODEL), whole, pipeline_mode=pl.Buffered(1)),
            pl.BlockSpec((HG_WIDTH, D_MODEL), whole, pipeline_mode=pl.Buffered(1)),
            pl.BlockSpec((D_MODEL, D_MODEL), whole, pipeline_mode=pl.Buffered(1)),
        ] + cast_specs,
        out_specs=[pl.BlockSpec((tm, D_MODEL), row)] + cast_specs,
        out_shape=[jax.ShapeDtypeStruct((n, D_MODEL), F32)] + [jax.ShapeDtypeStruct(w.shape, BF16) for w in casts],
        compiler_params=pltpu.CompilerParams(
            dimension_semantics=("arbitrary",), vmem_limit_bytes=VMEM_LIMIT),
        name="merge",
    )(x, ya, yh, ga, gr, w_a, w_r, w_o, *casts)
    return outs[0], outs[1:]


def _tile(n, want):
    t = min(want, n)
    assert n % t == 0, (n, t)
    return t


def kernel(x, positions, lb_table, ffn1_norm, ffn1_w_gu, ffn1_w_down, mix_norm, w_in, q_norm, k_norm, sinks,
           hg_out_norm, w_attn_branch, w_hg_branch, w_out, ffn2_norm, ffn2_w_gu, ffn2_w_down):
    batch, seq = x.shape[0], x.shape[1]
    n = batch * seq
    depth = w_in.shape[0]
    assert seq % BLOCK == 0 and seq % CHUNK == 0

    half = HEAD_DIM // 2
    inv_freq = (ROPE_THETA ** (-jnp.arange(half, dtype=F32) * 2.0 / HEAD_DIM))[:, None]
    tm_proj = _tile(n, 512)
    pos_rows = positions.astype(F32).reshape(n // tm_proj, 1, tm_proj)

    xf = x.reshape(n, D_MODEL)
    for l in range(depth):
        xf = _ffn(xf, ffn1_norm[l][None, :], ffn1_w_gu[l].astype(BF16), ffn1_w_down[l].astype(BF16),
                  tm=_tile(n, 1024), tf=512)
        gain = mix_norm[l][None, :]
        w_in_l = w_in[l].astype(BF16)
        q, k, v, hq, hk, hl, hi, h_mix = _proj_a(
            xf, gain, w_in_l, pos_rows, inv_freq, jnp.tile(q_norm[l], 2)[None, :],
            jnp.tile(k_norm[l], 2)[None, :], lb_table, layer=l, tm=tm_proj)
        og, ga, gr = _proj_b(h_mix, w_in_l, tm=_tile(n, 1024))
        ya = _attn(q, k, v, sinks[l], batch=batch, seq=seq, tq=_tile(seq, 1024))
        yh = _hgrn(hq, hk, hl, hi, og, hg_out_norm[l][None, :], batch=batch, seq=seq, tile=_tile(seq, 1024))
        xf, (w_gu2, w_down2) = _merge(
            xf, ya, yh, ga, gr, w_attn_branch[l].astype(BF16), w_hg_branch[l].astype(BF16), w_out[l].astype(BF16),
            casts=(ffn2_w_gu[l], ffn2_w_down[l]), tm=_tile(n, 512))
        xf = _ffn(xf, ffn2_norm[l][None, :], w_gu2, w_down2, tm=_tile(n, 1024), tf=512)
    return xf.reshape(batch, seq, D_MODEL)
```

```python
import functools

import jax
import jax.numpy as jnp
from jax import lax
from jax.experimental import pallas as pl
from jax.experimental.pallas import tpu as pltpu

D_MODEL = 2048
HEAD_DIM = 64
N_Q_HEADS = 16
N_KV_HEADS = 2
GQA_GROUP = N_Q_HEADS // N_KV_HEADS
ATTN_WIDTH = N_Q_HEADS * HEAD_DIM
KV_WIDTH = N_KV_HEADS * HEAD_DIM
WINDOW = 128
BLOCK = 128
ROPE_THETA = 10000.0
HG_HEAD_DIM = 128
HG_HEADS = 8
HG_WIDTH = HG_HEADS * HG_HEAD_DIM
CHUNK = 64
D_FF = 5632
EPS = 1e-6

LANES = 128
PAIRS_PER_GROUP = GQA_GROUP // 2
VMEM_LIMIT = 58 * 1024 * 1024
LOG2E = 1.4426950408889634

N_ATTN = ATTN_WIDTH + 2 * KV_WIDTH
COLS_A = N_ATTN + 3 * HG_WIDTH
COLS_B = HG_WIDTH + 2 * D_MODEL

F32 = jnp.float32
BF16 = jnp.bfloat16


def _dot(a, b):
    return jnp.dot(a, b, preferred_element_type=F32)


def _dot_nt(a, b):
    return lax.dot_general(a, b, (((1,), (1,)), ((), ())), preferred_element_type=F32)


def _dot_tn(a, b):
    return lax.dot_general(a, b, (((0,), (0,)), ((), ())), preferred_element_type=F32)


def _sigmoid(z):
    return 0.5 * jnp.tanh(0.5 * z) + 0.5


def _silu(z):
    half = 0.5 * z
    return half * jnp.tanh(half) + half


def _rms_rows(x, gain):
    ms = jnp.mean(x * x, axis=-1, keepdims=True)
    return x * lax.rsqrt(ms + EPS) * gain


FFN_NORM_ROWS = 128
FFN_NORM_FIRST = 2


def _ffn_body(x_hbm, g_ref, wg_ref, wu_ref, wd_ref, o_ref, h0_ref, h1_ref, xs_ref, sem, *, tm, tf):
    i = pl.program_id(0)
    j = pl.program_id(1)
    n_tiles = pl.num_programs(0)
    hbufs = (h0_ref, h1_ref)

    def x_copy(tile):
        rows = pl.ds(pl.multiple_of(tile * tm, tm), tm)
        return pltpu.make_async_copy(x_hbm.at[rows], xs_ref.at[:tm], sem)

    @pl.when((i == 0) & (j == 0))
    def _():
        xs_ref[tm:, :] = jnp.zeros((FFN_NORM_ROWS, D_MODEL), F32)
        x_copy(0).start()
        x_copy(0).wait()
        h0_ref[:tm, :] = _rms_rows(xs_ref[:tm, :], g_ref[...]).astype(BF16)

    @pl.when((j == FFN_NORM_FIRST) & (i + 1 < n_tiles))
    def _():
        x_copy(i + 1).wait()

    def chunk_step(parity, first):
        h = hbufs[parity][:tm, :]
        halves = [slice(c * tf // 2, (c + 1) * tf // 2) for c in range(2)]
        gate_up = [(_dot(h, wg_ref[:, sl]), _dot(h, wu_ref[:, sl])) for sl in halves]
        act = [(_silu(g) * (u * 0.5)).astype(BF16) for g, u in gate_up]
        down = _dot(act[0], wd_ref[halves[0], :]) + _dot(act[1], wd_ref[halves[1], :])
        if first:
            o_ref[...] = xs_ref[:tm, :] + down

            @pl.when(i + 1 < n_tiles)
            def _():
                x_copy(i + 1).start()
        else:
            o_ref[...] += down
            part = j - FFN_NORM_FIRST
            live = (part < tm // FFN_NORM_ROWS) & (part >= 0) & (i + 1 < n_tiles)
            rows = pl.ds(pl.multiple_of(jnp.where(live, part * FFN_NORM_ROWS, tm), FFN_NORM_ROWS), FFN_NORM_ROWS)
            hbufs[1 - parity][rows, :] = _rms_rows(xs_ref[rows, :], g_ref[...]).astype(BF16)

    for parity in range(2):
        for first in (True, False):
            pl.when((i % 2 == parity) & ((j == 0) == first))(functools.partial(chunk_step, parity, first))


def _ffn(x, gain, w_gu, w_down, *, tm, tf):
    n = x.shape[0]
    nf = D_FF // tf
    assert tm % FFN_NORM_ROWS == 0 and nf >= FFN_NORM_FIRST + tm // FFN_NORM_ROWS
    return pl.pallas_call(
        functools.partial(_ffn_body, tm=tm, tf=tf),
        grid=(n // tm, nf),
        in_specs=[
            pl.BlockSpec(memory_space=pl.ANY),
            pl.BlockSpec((1, D_MODEL), lambda i, j: (0, 0)),
            pl.BlockSpec((D_MODEL, tf), lambda i, j: (0, j)),
            pl.BlockSpec((D_MODEL, tf), lambda i, j: (0, j + nf)),
            pl.BlockSpec((tf, D_MODEL), lambda i, j: (j, 0)),
        ],
        out_specs=pl.BlockSpec((tm, D_MODEL), lambda i, j: (i, 0)),
        out_shape=jax.ShapeDtypeStruct((n, D_MODEL), F32),
        scratch_shapes=[pltpu.VMEM((tm + FFN_NORM_ROWS, D_MODEL), BF16), pltpu.VMEM((tm + FFN_NORM_ROWS, D_MODEL), BF16),
                        pltpu.VMEM((tm + FFN_NORM_ROWS, D_MODEL), F32), pltpu.SemaphoreType.DMA(())],
        compiler_params=pltpu.CompilerParams(
            dimension_semantics=("arbitrary", "arbitrary"), vmem_limit_bytes=VMEM_LIMIT),
        name="ffn",
    )(x, gain, w_gu, w_gu, w_down)


def _proj_a_body(x_ref, g_ref, w_ref, pos_ref, invf_ref, qg_ref, kg_ref, lbt_ref,
                 q_ref, k_ref, v_ref, hq_ref, hk_ref, hl_ref, hi_ref, h_ref, *, layer):
    h = _rms_rows(x_ref[...], g_ref[...]).astype(BF16)
    h_ref[...] = h

    def cols(start, width):
        return _dot(h, w_ref[:, start:start + width])

    lane = lax.broadcasted_iota(jnp.int32, (1, LANES), 1)
    first_half = (lane % HEAD_DIM) < (HEAD_DIM // 2)
    gi = lax.broadcasted_iota(jnp.int32, (LANES, LANES), 0) // HEAD_DIM
    gj = lax.broadcasted_iota(jnp.int32, (LANES, LANES), 1) // HEAD_DIM
    gsum = jnp.where(gi == gj, 1.0, 0.0).astype(BF16)

    def head_norm(v, gain):
        sq = v * v
        hi = sq.astype(BF16)
        lo = (sq - hi.astype(F32)).astype(BF16)
        ss = _dot(hi, gsum) + _dot(lo, gsum)
        return v * lax.rsqrt(ss * (1.0 / HEAD_DIM) + EPS) * gain

    ang = invf_ref[...] * pos_ref[...]
    cos_t = jnp.cos(ang)
    sin_t = jnp.sin(ang)
    reps = LANES // HEAD_DIM
    cosf = jnp.concatenate([cos_t, cos_t] * reps, axis=0).T
    sin_signed = jnp.concatenate([-sin_t, sin_t] * reps, axis=0).T

    def rope(v):
        rot = jnp.where(first_half, pltpu.roll(v, LANES - HEAD_DIM // 2, 1), pltpu.roll(v, HEAD_DIM // 2, 1))
        return v * cosf + rot * sin_signed

    lbt = lbt_ref[...]
    e = jnp.exp(lbt - jnp.max(lbt, axis=0, keepdims=True))
    sm = e / jnp.sum(e, axis=0, keepdims=True)
    lb = jnp.sum(sm[:layer + 1], axis=0, keepdims=True)

    zq = cols(0, ATTN_WIDTH)
    zkv = cols(ATTN_WIDTH, 2 * KV_WIDTH)

    def attn_piece(c):
        if c < ATTN_WIDTH // LANES:
            sl = slice(c * LANES, (c + 1) * LANES)
            q_ref[:, sl] = (rope(head_norm(zq[:, sl], qg_ref[...])) * (HEAD_DIM ** -0.5 * LOG2E)).astype(BF16)
        else:
            k_ref[...] = rope(head_norm(zkv[:, :KV_WIDTH], kg_ref[...])).astype(BF16)
            v_ref[...] = zkv[:, KV_WIDTH:].astype(BF16)

    def hgrn_piece(kind, sl):
        z = cols(N_ATTN + kind * HG_WIDTH + sl.start, sl.stop - sl.start)
        if kind == 0:
            hq_ref[:, sl] = _silu(z).astype(BF16)
        elif kind == 1:
            f = lb[:, sl] + (1.0 - lb[:, sl]) * _sigmoid(z)
            hk_ref[:, sl] = (1.0 - f).astype(BF16)
            hl_ref[:, sl] = jnp.log2(f)
        else:
            hi_ref[:, sl] = z.astype(BF16)

    chunk = 2 * LANES
    pieces = [(kind, slice(c, c + chunk)) for kind in range(3) for c in range(0, HG_WIDTH, chunk)]
    n_attn_pieces = ATTN_WIDTH // LANES + 1
    for idx, (kind, sl) in enumerate(pieces):
        hgrn_piece(kind, sl)
        if idx < n_attn_pieces:
            attn_piece(idx)


def _proj_a(x, gain, w_a, pos_rows, inv_freq, q_gain, k_gain, lb_table, *, layer, tm):
    n = x.shape[0]
    row = lambda i: (i, 0)
    whole = lambda i: (0, 0)
    widths = (ATTN_WIDTH, KV_WIDTH, KV_WIDTH, HG_WIDTH, HG_WIDTH, HG_WIDTH, HG_WIDTH, D_MODEL)
    dtypes = (BF16, BF16, BF16, BF16, BF16, F32, BF16, BF16)
    return pl.pallas_call(
        functools.partial(_proj_a_body, layer=layer),
        grid=(n // tm,),
        in_specs=[
            pl.BlockSpec((tm, D_MODEL), row),
            pl.BlockSpec((1, D_MODEL), whole),
            pl.BlockSpec((D_MODEL, COLS_A), whole, pipeline_mode=pl.Buffered(1)),
            pl.BlockSpec((None, 1, tm), lambda i: (i, 0, 0)),
            pl.BlockSpec((HEAD_DIM // 2, 1), whole),
            pl.BlockSpec((1, LANES), whole),
            pl.BlockSpec((1, LANES), whole),
            pl.BlockSpec(lb_table.shape, whole),
        ],
        out_specs=[pl.BlockSpec((tm, w), row) for w in widths],
        out_shape=[jax.ShapeDtypeStruct((n, w), d) for w, d in zip(widths, dtypes)],
        compiler_params=pltpu.CompilerParams(
            dimension_semantics=("parallel",), vmem_limit_bytes=VMEM_LIMIT),
        name="proj_a",
    )(x, gain, w_a, pos_rows, inv_freq, q_gain, k_gain, lb_table)


BF16_SUBLANES = 16


def _cast_specs(casts, steps):
    specs = []
    for w in casts:
        rows = w.shape[0]
        rb = next(r for r in range(BF16_SUBLANES, rows + 1, BF16_SUBLANES) if rows % r == 0 and rows // r <= steps)
        specs.append(pl.BlockSpec(
            (rb, w.shape[1]), functools.partial(lambda i, last: (jnp.minimum(i, last), 0), last=rows // rb - 1)))
    return specs


def _run_casts(src_refs, dst_refs):
    for src_ref, dst_ref in zip(src_refs, dst_refs):
        dst_ref[...] = src_ref[...].astype(BF16)


def _proj_b_body(h_ref, w_hbm, *rest):
    n_casts = (len(rest) - 5) // 2
    cast_src, (og_ref, ga_ref, gr_ref) = rest[:n_casts], rest[n_casts:n_casts + 3]
    cast_dst, (w_ref, sem) = rest[n_casts + 3:2 * n_casts + 3], rest[2 * n_casts + 3:]

    @pl.when(pl.program_id(0) == 0)
    def _():
        copy = pltpu.make_async_copy(w_hbm.at[:, COLS_A:], w_ref, sem)
        copy.start()
        copy.wait()

    def cols(start, width):
        return _dot(h_ref[...], w_ref[:, start:start + width])

    og_ref[...] = _silu(cols(0, HG_WIDTH)).astype(BF16)
    for c in range(D_MODEL // HG_WIDTH):
        sl = slice(c * HG_WIDTH, (c + 1) * HG_WIDTH)
        ga_ref[:, sl] = _sigmoid(cols(HG_WIDTH + c * HG_WIDTH, HG_WIDTH)).astype(BF16)
        gr_ref[:, sl] = _sigmoid(cols(HG_WIDTH + D_MODEL + c * HG_WIDTH, HG_WIDTH)).astype(BF16)
    _run_casts(cast_src, cast_dst)


def _proj_b(h, w_in, casts=(), *, tm):
    n = h.shape[0]
    row = lambda i: (i, 0)
    widths = (HG_WIDTH, D_MODEL, D_MODEL)
    cast_specs = _cast_specs(casts, n // tm)
    outs = pl.pallas_call(
        _proj_b_body,
        grid=(n // tm,),
        in_specs=[
            pl.BlockSpec((tm, D_MODEL), row),
            pl.BlockSpec(memory_space=pl.ANY),
        ] + cast_specs,
        out_specs=[pl.BlockSpec((tm, w), row) for w in widths] + cast_specs,
        out_shape=([jax.ShapeDtypeStruct((n, w), BF16) for w in widths]
                   + [jax.ShapeDtypeStruct(w.shape, BF16) for w in casts]),
        scratch_shapes=[pltpu.VMEM((D_MODEL, COLS_B), BF16), pltpu.SemaphoreType.DMA(())],
        compiler_params=pltpu.CompilerParams(
            dimension_semantics=("arbitrary",), vmem_limit_bytes=VMEM_LIMIT),
        name="proj_b",
    )(h, w_in, *casts)
    return outs[:3], outs[3:]


def _attn_body(sinks_ref, q_ref, k_ref, kp_ref, v_ref, vp_ref, o_ref, ks_ref, vs_ref, *, tq):
    t = pl.program_id(1)
    nb = tq // BLOCK
    lane = lax.broadcasted_iota(jnp.int32, (1, LANES), 1)
    low_head = lane < HEAD_DIM

    def put_kv(dst_ref, rows, v, pad):
        swapped = pltpu.roll(v, HEAD_DIM, 1)
        dst_ref[0, rows, :] = jnp.where(low_head, v, pad).astype(BF16)
        dst_ref[1, rows, :] = jnp.where(low_head, pad, swapped).astype(BF16)
        dst_ref[2, rows, :] = jnp.where(low_head, swapped, pad).astype(BF16)
        dst_ref[3, rows, :] = jnp.where(low_head, pad, v).astype(BF16)

    put_kv(ks_ref, slice(0, BLOCK), kp_ref[...].astype(F32), 0.0)
    put_kv(ks_ref, slice(BLOCK, BLOCK + tq), k_ref[...].astype(F32), 0.0)
    put_kv(vs_ref, slice(0, BLOCK), vp_ref[...].astype(F32), 1.0)
    put_kv(vs_ref, slice(BLOCK, BLOCK + tq), v_ref[...].astype(F32), 1.0)

    rows_q = PAIRS_PER_GROUP * BLOCK
    qi = lax.broadcasted_iota(jnp.int32, (rows_q, BLOCK), 0) % BLOCK
    kj = lax.broadcasted_iota(jnp.int32, (rows_q, BLOCK), 1)
    band_prev = kj > qi + BLOCK - WINDOW
    band_cur = kj <= qi
    sink_fill = {}
    for g in range(N_KV_HEADS):
        for half in range(2):
            sink_rows = jnp.concatenate(
                [jnp.full((BLOCK, LANES), sinks_ref[g * GQA_GROUP + 2 * p + half] * LOG2E, F32)
                 for p in range(PAIRS_PER_GROUP)], axis=0)
            sink_fill[g, half] = jnp.where(kj == 0, sink_rows, -jnp.inf)
    first_key = lax.broadcasted_iota(jnp.int32, (2 * BLOCK, LANES), 0) == 0
    drop_value = (first_key & low_head, first_key & jnp.logical_not(low_head))

    def block(n, carry):
        qrows = pl.ds(pl.multiple_of(n * BLOCK, BLOCK), BLOCK)
        krows = pl.ds(pl.multiple_of(n * BLOCK, BLOCK), 2 * BLOCK)
        valid_prev = band_prev & (t * nb + n > 0)

        def scores(g, half):
            qst = jnp.concatenate(
                [q_ref[qrows, (g * PAIRS_PER_GROUP + p) * LANES:(g * PAIRS_PER_GROUP + p + 1) * LANES]
                 for p in range(PAIRS_PER_GROUP)], axis=0)
            return _dot_nt(qst, ks_ref[2 * g + half, krows, :])

        units = [(g, half) for g in range(N_KV_HEADS) for half in range(2)]
        s_next = scores(*units[0])
        o = []
        for idx, (g, half) in enumerate(units):
            s = s_next
            if idx + 1 < len(units):
                s_next = scores(*units[idx + 1])
            s_prev = jnp.where(valid_prev, s[:, :BLOCK], sink_fill[g, half])
            s_cur = jnp.where(band_cur, s[:, BLOCK:], -jnp.inf)
            m = jnp.max(jnp.maximum(s_prev, s_cur), axis=-1, keepdims=True)
            p_ = jnp.concatenate([jnp.exp2(s_prev - m), jnp.exp2(s_cur - m)], axis=1).astype(BF16)
            vwin = vs_ref[2 * g + half, krows, :]
            vwin = jnp.where(drop_value[half], jnp.zeros_like(vwin), vwin)
            o.append(_dot(p_, vwin))
            if half == 1:
                num = jnp.where(low_head, o[0], o[1])
                den = pltpu.roll(jnp.where(low_head, o[1], o[0]), HEAD_DIM, 1)
                outs = num / den
                o = []
                for p in range(PAIRS_PER_GROUP):
                    c = g * PAIRS_PER_GROUP + p
                    o_ref[qrows, c * LANES:(c + 1) * LANES] = outs[p * BLOCK:(p + 1) * BLOCK].astype(BF16)
        return carry

    lax.fori_loop(0, nb, block, 0, unroll=min(4, nb))


def _attn(q, k, v, sinks, *, batch, seq, tq):
    n = batch * seq
    nt = seq // tq
    nb = tq // BLOCK
    bps = seq // BLOCK
    cur = lambda b, t: (b * nt + t, 0)
    prev = lambda b, t: (b * bps + jnp.maximum(t * nb - 1, 0), 0)
    return pl.pallas_call(
        functools.partial(_attn_body, tq=tq),
        grid=(batch, nt),
        in_specs=[
            pl.BlockSpec(memory_space=pltpu.SMEM),
            pl.BlockSpec((tq, ATTN_WIDTH), cur),
            pl.BlockSpec((tq, KV_WIDTH), cur),
            pl.BlockSpec((BLOCK, KV_WIDTH), prev),
            pl.BlockSpec((tq, KV_WIDTH), cur),
            pl.BlockSpec((BLOCK, KV_WIDTH), prev),
        ],
        out_specs=pl.BlockSpec((tq, ATTN_WIDTH), cur),
        out_shape=jax.ShapeDtypeStruct((n, ATTN_WIDTH), BF16),
        scratch_shapes=[
            pltpu.VMEM((4, tq + BLOCK, LANES), BF16),
            pltpu.VMEM((4, tq + BLOCK, LANES), BF16),
        ],
        compiler_params=pltpu.CompilerParams(
            dimension_semantics=("parallel", "parallel"), vmem_limit_bytes=VMEM_LIMIT),
        name="attn",
    )(sinks, q, k, k, v, v)


def _hgrn_body(hq_ref, hk_ref, hl_ref, hi_ref, og_ref, gain_ref, o_ref, state_ref, *, tile):
    @pl.when(pl.program_id(1) == 0)
    def _():
        state_ref[...] = jnp.zeros_like(state_ref)

    ri = lax.broadcasted_iota(jnp.int32, (CHUNK, CHUNK), 0)
    ci = lax.broadcasted_iota(jnp.int32, (CHUNK, CHUNK), 1)
    causal = ri >= ci
    tri = jnp.where(causal, 1.0, 0.0).astype(BF16)
    gain = gain_ref[...]

    def chunk(c, carry):
        rows = pl.ds(pl.multiple_of(c * CHUNK, CHUNK), CHUNK)
        lf = hl_ref[rows, :]
        hi = lf.astype(BF16)
        lo = (lf - hi.astype(F32)).astype(BF16)
        b = _dot(tri, hi) + _dot(tri, lo)
        b_mid = b[CHUNK // 2:CHUNK // 2 + 1]
        b_last = b[CHUNK - 1:CHUNK]
        qh = hq_ref[rows, :].astype(F32)
        kk = hk_ref[rows, :].astype(F32)
        q_mid = (qh * jnp.exp2(b - b_mid)).astype(BF16)
        k_mid = (kk * jnp.exp2(b_mid - b)).astype(BF16)
        q_in = (qh * jnp.exp2(b)).astype(BF16)
        k_out = (kk * jnp.exp2(b_last - b)).astype(BF16)
        decay = jnp.exp2(b_last)
        heads = [slice(h * HG_HEAD_DIM, (h + 1) * HG_HEAD_DIM) for h in range(HG_HEADS)]
        att = [_dot_nt(q_mid[:, sl], k_mid[:, sl]) for sl in heads]
        st = [state_ref[h] for h in range(HG_HEADS)]
        o_inter = [_dot_nt(q_in[:, sl], st[h].astype(BF16)) for h, sl in enumerate(heads)]
        upd = [_dot_tn(hi_ref[rows, sl], k_out[:, sl]) for sl in heads]
        o = [o_inter[h] + _dot(jnp.where(causal, att[h], 0.0).astype(BF16), hi_ref[rows, sl])
             for h, sl in enumerate(heads)]
        for h, sl in enumerate(heads):
            state_ref[h] = st[h] * decay[:, sl] + upd[h]
            y = _rms_rows(o[h], gain) * og_ref[rows, sl].astype(F32)
            o_ref[rows, sl] = y.astype(BF16)
        return carry

    lax.fori_loop(0, tile // CHUNK, chunk, 0, unroll=True)


def _hgrn(hq, hk, hl, hi, og, gain, *, batch, seq, tile):
    n = batch * seq
    nt = seq // tile
    cur = lambda b, t: (b * nt + t, 0)
    return pl.pallas_call(
        functools.partial(_hgrn_body, tile=tile),
        grid=(batch, nt),
        in_specs=[pl.BlockSpec((tile, HG_WIDTH), cur)] * 5 + [pl.BlockSpec((1, HG_HEAD_DIM), lambda b, t: (0, 0))],
        out_specs=pl.BlockSpec((tile, HG_WIDTH), cur),
        out_shape=jax.ShapeDtypeStruct((n, HG_WIDTH), BF16),
        scratch_shapes=[pltpu.VMEM((HG_HEADS, HG_HEAD_DIM, HG_HEAD_DIM), F32)],
        compiler_params=pltpu.CompilerParams(
            dimension_semantics=("parallel", "arbitrary"), vmem_limit_bytes=VMEM_LIMIT),
        name="hgrn",
    )(hq, hk, hl, hi, og, gain)


def _merge_body(x_ref, ya_ref, yh_ref, ga_ref, gr_ref, wa_ref, wr_ref, wo_ref, *rest):
    n_casts = (len(rest) - 1) // 2
    cast_src, o_ref, cast_dst = rest[:n_casts], rest[n_casts], rest[n_casts + 1:]
    a = _dot(ya_ref[...], wa_ref[...])
    r = _dot(yh_ref[...], wr_ref[...])
    merged = (ga_ref[...].astype(F32) * a + gr_ref[...].astype(F32) * r).astype(BF16)
    o_ref[...] = x_ref[...] + _dot(merged, wo_ref[...])
    _run_casts(cast_src, cast_dst)


def _merge(x, ya, yh, ga, gr, w_a, w_r, w_o, casts=(), *, tm):
    n = x.shape[0]
    row = lambda i: (i, 0)
    whole = lambda i: (0, 0)
    cast_specs = _cast_specs(casts, n // tm)
    outs = pl.pallas_call(
        _merge_body,
        grid=(n // tm,),
        in_specs=[
            pl.BlockSpec((tm, D_MODEL), row),
            pl.BlockSpec((tm, ATTN_WIDTH), row),
            pl.BlockSpec((tm, HG_WIDTH), row),
            pl.BlockSpec((tm, D_MODEL), row),
            pl.BlockSpec((tm, D_MODEL), row),
            pl.BlockSpec((ATTN_WIDTH, D_MODEL), whole, pipeline_mode=pl.Buffered(1)),
            pl.BlockSpec((HG_WIDTH, D_MODEL), whole, pipeline_mode=pl.Buffered(1)),
            pl.BlockSpec((D_MODEL, D_MODEL), whole, pipeline_mode=pl.Buffered(1)),
        ] + cast_specs,
        out_specs=[pl.BlockSpec((tm, D_MODEL), row)] + cast_specs,
        out_shape=[jax.ShapeDtypeStruct((n, D_MODEL), F32)] + [jax.ShapeDtypeStruct(w.shape, BF16) for w in casts],
        compiler_params=pltpu.CompilerParams(
            dimension_semantics=("arbitrary",), vmem_limit_bytes=VMEM_LIMIT),
        name="merge",
    )(x, ya, yh, ga, gr, w_a, w_r, w_o, *casts)
    return outs[0], outs[1:]


def _tile(n, want):
    t = min(want, n)
    assert n % t == 0, (n, t)
    return t


def kernel(x, positions, lb_table, ffn1_norm, ffn1_w_gu, ffn1_w_down, mix_norm, w_in, q_norm, k_norm, sinks,
           hg_out_norm, w_attn_branch, w_hg_branch, w_out, ffn2_norm, ffn2_w_gu, ffn2_w_down):
    batch, seq = x.shape[0], x.shape[1]
    n = batch * seq
    depth = w_in.shape[0]
    assert seq % BLOCK == 0 and seq % CHUNK == 0

    half = HEAD_DIM // 2
    inv_freq = (ROPE_THETA ** (-jnp.arange(half, dtype=F32) * 2.0 / HEAD_DIM))[:, None]
    tm_proj = _tile(n, 512)
    pos_rows = positions.astype(F32).reshape(n // tm_proj, 1, tm_proj)

    xf = x.reshape(n, D_MODEL)
    for l in range(depth):
        xf = _ffn(xf, ffn1_norm[l][None, :], ffn1_w_gu[l].astype(BF16), ffn1_w_down[l].astype(BF16),
                  tm=_tile(n, 1024), tf=512)
        gain = mix_norm[l][None, :]
        w_in_l = w_in[l].astype(BF16)
        q, k, v, hq, hk, hl, hi, h_mix = _proj_a(
            xf, gain, w_in_l, pos_rows, inv_freq, jnp.tile(q_norm[l], 2)[None, :],
            jnp.tile(k_norm[l], 2)[None, :], lb_table, layer=l, tm=tm_proj)
        (og, ga, gr), (w_a, w_r, w_o) = _proj_b(
            h_mix, w_in_l, casts=(w_attn_branch[l], w_hg_branch[l], w_out[l]), tm=_tile(n, 1024))
        ya = _attn(q, k, v, sinks[l], batch=batch, seq=seq, tq=_tile(seq, 1024))
        yh = _hgrn(hq, hk, hl, hi, og, hg_out_norm[l][None, :], batch=batch, seq=seq, tile=_tile(seq, 1024))
        xf, (w_gu2, w_down2) = _merge(
            xf, ya, yh, ga, gr, w_a, w_r, w_o, casts=(ffn2_w_gu[l], ffn2_w_down[l]), tm=_tile(n, 512))
        xf = _ffn(xf, ffn2_norm[l][None, :], w_gu2, w_down2, tm=_tile(n, 1024), tf=512)
    return xf.reshape(batch, seq, D_MODEL)
```

```python
import functools

import jax
import jax.numpy as jnp
from jax import lax
from jax.experimental import pallas as pl
from jax.experimental.pallas import tpu as pltpu

D_MODEL = 2048
HEAD_DIM = 64
N_Q_HEADS = 16
N_KV_HEADS = 2
GQA_GROUP = N_Q_HEADS // N_KV_HEADS
ATTN_WIDTH = N_Q_HEADS * HEAD_DIM
KV_WIDTH = N_KV_HEADS * HEAD_DIM
WINDOW = 128
BLOCK = 128
ROPE_THETA = 10000.0
HG_HEAD_DIM = 128
HG_HEADS = 8
HG_WIDTH = HG_HEADS * HG_HEAD_DIM
CHUNK = 64
D_FF = 5632
EPS = 1e-6

LANES = 128
PAIRS_PER_GROUP = GQA_GROUP // 2
VMEM_LIMIT = 58 * 1024 * 1024
LOG2E = 1.4426950408889634

N_ATTN = ATTN_WIDTH + 2 * KV_WIDTH
COLS_A = N_ATTN + 3 * HG_WIDTH
COLS_B = HG_WIDTH + 2 * D_MODEL

F32 = jnp.float32
BF16 = jnp.bfloat16


def _dot(a, b):
    return jnp.dot(a, b, preferred_element_type=F32)


def _dot_nt(a, b):
    return lax.dot_general(a, b, (((1,), (1,)), ((), ())), preferred_element_type=F32)


def _dot_tn(a, b):
    return lax.dot_general(a, b, (((0,), (0,)), ((), ())), preferred_element_type=F32)


def _sigmoid(z):
    return 0.5 * jnp.tanh(0.5 * z) + 0.5


def _silu(z):
    half = 0.5 * z
    return half * jnp.tanh(half) + half


def _rms_rows(x, gain):
    ms = jnp.mean(x * x, axis=-1, keepdims=True)
    return x * lax.rsqrt(ms + EPS) * gain


FFN_TF = 512
FFN_NORM_ROWS = 128
FFN_NORM_FIRST = 2


def _ffn_body(x_hbm, g_ref, wgu_ref, wd_ref, o_ref, h0_ref, h1_ref, xs_ref, sem, *, tm, tf):
    i = pl.program_id(0)
    j = pl.program_id(1)
    n_tiles = pl.num_programs(0)
    hbufs = (h0_ref, h1_ref)

    def x_copy(tile):
        rows = pl.ds(pl.multiple_of(tile * tm, tm), tm)
        return pltpu.make_async_copy(x_hbm.at[rows], xs_ref.at[:tm], sem)

    @pl.when((i == 0) & (j == 0))
    def _():
        xs_ref[tm:, :] = jnp.zeros((FFN_NORM_ROWS, D_MODEL), F32)
        x_copy(0).start()
        x_copy(0).wait()
        h0_ref[:tm, :] = _rms_rows(xs_ref[:tm, :], g_ref[...]).astype(BF16)

    @pl.when((j == FFN_NORM_FIRST) & (i + 1 < n_tiles))
    def _():
        x_copy(i + 1).wait()

    def chunk_step(parity, first):
        h = hbufs[parity][:tm, :]
        halves = [slice(c * tf // 2, (c + 1) * tf // 2) for c in range(2)]
        gate_up = [(_dot(h, wgu_ref[:, sl]), _dot(h, wgu_ref[:, tf + sl.start:tf + sl.stop])) for sl in halves]
        act = [(_silu(g) * (u * 0.5)).astype(BF16) for g, u in gate_up]
        down = _dot(act[0], wd_ref[halves[0], :]) + _dot(act[1], wd_ref[halves[1], :])
        if first:
            o_ref[...] = xs_ref[:tm, :] + down

            @pl.when(i + 1 < n_tiles)
            def _():
                x_copy(i + 1).start()
        else:
            o_ref[...] += down
            part = j - FFN_NORM_FIRST
            live = (part < tm // FFN_NORM_ROWS) & (part >= 0) & (i + 1 < n_tiles)
            rows = pl.ds(pl.multiple_of(jnp.where(live, part * FFN_NORM_ROWS, tm), FFN_NORM_ROWS), FFN_NORM_ROWS)
            hbufs[1 - parity][rows, :] = _rms_rows(xs_ref[rows, :], g_ref[...]).astype(BF16)

    for parity in range(2):
        for first in (True, False):
            pl.when((i % 2 == parity) & ((j == 0) == first))(functools.partial(chunk_step, parity, first))


def _ffn(x, gain, w_gu, w_down, *, tm):
    n = x.shape[0]
    nf, tf = w_gu.shape[0], w_gu.shape[2] // 2
    assert tm % FFN_NORM_ROWS == 0 and nf >= FFN_NORM_FIRST + tm // FFN_NORM_ROWS
    return pl.pallas_call(
        functools.partial(_ffn_body, tm=tm, tf=tf),
        grid=(n // tm, nf),
        in_specs=[
            pl.BlockSpec(memory_space=pl.ANY),
            pl.BlockSpec((1, D_MODEL), lambda i, j: (0, 0)),
            pl.BlockSpec((None, D_MODEL, 2 * tf), lambda i, j: (j, 0, 0)),
            pl.BlockSpec((tf, D_MODEL), lambda i, j: (j, 0)),
        ],
        out_specs=pl.BlockSpec((tm, D_MODEL), lambda i, j: (i, 0)),
        out_shape=jax.ShapeDtypeStruct((n, D_MODEL), F32),
        scratch_shapes=[pltpu.VMEM((tm + FFN_NORM_ROWS, D_MODEL), BF16), pltpu.VMEM((tm + FFN_NORM_ROWS, D_MODEL), BF16),
                        pltpu.VMEM((tm + FFN_NORM_ROWS, D_MODEL), F32), pltpu.SemaphoreType.DMA(())],
        compiler_params=pltpu.CompilerParams(
            dimension_semantics=("arbitrary", "arbitrary"), vmem_limit_bytes=VMEM_LIMIT),
        name="ffn",
    )(x, gain, w_gu, w_down)


def _proj_a_body(x_ref, g_ref, w_ref, pos_ref, invf_ref, qg_ref, kg_ref, lbt_ref,
                 q_ref, k_ref, v_ref, hq_ref, hk_ref, hl_ref, hi_ref, h_ref, *, layer):
    h = _rms_rows(x_ref[...], g_ref[...]).astype(BF16)
    h_ref[...] = h

    def cols(start, width):
        return _dot(h, w_ref[:, start:start + width])

    lane = lax.broadcasted_iota(jnp.int32, (1, LANES), 1)
    first_half = (lane % HEAD_DIM) < (HEAD_DIM // 2)
    gi = lax.broadcasted_iota(jnp.int32, (LANES, LANES), 0) // HEAD_DIM
    gj = lax.broadcasted_iota(jnp.int32, (LANES, LANES), 1) // HEAD_DIM
    gsum = jnp.where(gi == gj, 1.0, 0.0).astype(BF16)

    def head_norm(v, gain):
        sq = v * v
        hi = sq.astype(BF16)
        lo = (sq - hi.astype(F32)).astype(BF16)
        ss = _dot(hi, gsum) + _dot(lo, gsum)
        return v * lax.rsqrt(ss * (1.0 / HEAD_DIM) + EPS) * gain

    ang = invf_ref[...] * pos_ref[...]
    cos_t = jnp.cos(ang)
    sin_t = jnp.sin(ang)
    reps = LANES // HEAD_DIM
    cosf = jnp.concatenate([cos_t, cos_t] * reps, axis=0).T
    sin_signed = jnp.concatenate([-sin_t, sin_t] * reps, axis=0).T

    def rope(v):
        rot = jnp.where(first_half, pltpu.roll(v, LANES - HEAD_DIM // 2, 1), pltpu.roll(v, HEAD_DIM // 2, 1))
        return v * cosf + rot * sin_signed

    lbt = lbt_ref[...]
    e = jnp.exp(lbt - jnp.max(lbt, axis=0, keepdims=True))
    sm = e / jnp.sum(e, axis=0, keepdims=True)
    lb = jnp.sum(sm[:layer + 1], axis=0, keepdims=True)

    zq = cols(0, ATTN_WIDTH)
    zkv = cols(ATTN_WIDTH, 2 * KV_WIDTH)

    def attn_piece(c):
        if c < ATTN_WIDTH // LANES:
            sl = slice(c * LANES, (c + 1) * LANES)
            q_ref[:, sl] = (rope(head_norm(zq[:, sl], qg_ref[...])) * (HEAD_DIM ** -0.5 * LOG2E)).astype(BF16)
        else:
            k_ref[...] = rope(head_norm(zkv[:, :KV_WIDTH], kg_ref[...])).astype(BF16)
            v_ref[...] = zkv[:, KV_WIDTH:].astype(BF16)

    def hgrn_piece(kind, sl):
        z = cols(N_ATTN + kind * HG_WIDTH + sl.start, sl.stop - sl.start)
        if kind == 0:
            hq_ref[:, sl] = _silu(z).astype(BF16)
        elif kind == 1:
            f = lb[:, sl] + (1.0 - lb[:, sl]) * _sigmoid(z)
            hk_ref[:, sl] = (1.0 - f).astype(BF16)
            hl_ref[:, sl] = jnp.log2(f)
        else:
            hi_ref[:, sl] = z.astype(BF16)

    chunk = 2 * LANES
    pieces = [(kind, slice(c, c + chunk)) for kind in range(3) for c in range(0, HG_WIDTH, chunk)]
    n_attn_pieces = ATTN_WIDTH // LANES + 1
    for idx, (kind, sl) in enumerate(pieces):
        hgrn_piece(kind, sl)
        if idx < n_attn_pieces:
            attn_piece(idx)


def _proj_a(x, gain, w_a, pos_rows, inv_freq, q_gain, k_gain, lb_table, *, layer, tm):
    n = x.shape[0]
    row = lambda i: (i, 0)
    whole = lambda i: (0, 0)
    widths = (ATTN_WIDTH, KV_WIDTH, KV_WIDTH, HG_WIDTH, HG_WIDTH, HG_WIDTH, HG_WIDTH, D_MODEL)
    dtypes = (BF16, BF16, BF16, BF16, BF16, F32, BF16, BF16)
    return pl.pallas_call(
        functools.partial(_proj_a_body, layer=layer),
        grid=(n // tm,),
        in_specs=[
            pl.BlockSpec((tm, D_MODEL), row),
            pl.BlockSpec((1, D_MODEL), whole),
            pl.BlockSpec((D_MODEL, COLS_A), whole, pipeline_mode=pl.Buffered(1)),
            pl.BlockSpec((None, 1, tm), lambda i: (i, 0, 0)),
            pl.BlockSpec((HEAD_DIM // 2, 1), whole),
            pl.BlockSpec((1, LANES), whole),
            pl.BlockSpec((1, LANES), whole),
            pl.BlockSpec(lb_table.shape, whole),
        ],
        out_specs=[pl.BlockSpec((tm, w), row) for w in widths],
        out_shape=[jax.ShapeDtypeStruct((n, w), d) for w, d in zip(widths, dtypes)],
        compiler_params=pltpu.CompilerParams(
            dimension_semantics=("parallel",), vmem_limit_bytes=VMEM_LIMIT),
        name="proj_a",
    )(x, gain, w_a, pos_rows, inv_freq, q_gain, k_gain, lb_table)


BF16_SUBLANES = 16


def _cast_specs(casts, steps, gate_up_tf):
    in_specs, out_specs, out_shapes = [], [], []
    for w in casts:
        rows, width = w.shape
        rb = next(r for r in range(BF16_SUBLANES, rows + 1, BF16_SUBLANES) if rows % r == 0 and rows // r <= steps)
        last = rows // rb - 1
        in_specs.append(pl.BlockSpec((rb, width), functools.partial(lambda i, last: (jnp.minimum(i, last), 0), last=last)))
        if width == 2 * D_FF:
            nf = D_FF // gate_up_tf
            out_specs.append(pl.BlockSpec(
                (nf, rb, 2 * gate_up_tf), functools.partial(lambda i, last: (0, jnp.minimum(i, last), 0), last=last)))
            out_shapes.append(jax.ShapeDtypeStruct((nf, rows, 2 * gate_up_tf), BF16))
        else:
            out_specs.append(in_specs[-1])
            out_shapes.append(jax.ShapeDtypeStruct(w.shape, BF16))
    return in_specs, out_specs, out_shapes


def _run_casts(src_refs, dst_refs):
    for src_ref, dst_ref in zip(src_refs, dst_refs):
        if len(dst_ref.shape) == 2:
            dst_ref[...] = src_ref[...].astype(BF16)
        else:
            nf, _, two_tf = dst_ref.shape
            tf = two_tf // 2
            for c in range(nf):
                dst_ref[c, :, :tf] = src_ref[:, c * tf:(c + 1) * tf].astype(BF16)
                dst_ref[c, :, tf:] = src_ref[:, D_FF + c * tf:D_FF + (c + 1) * tf].astype(BF16)


def _chunk_gate_up(w_gu, tf):
    nf = D_FF // tf
    return w_gu.reshape(D_MODEL, 2, nf, tf).transpose(2, 0, 1, 3).reshape(nf, D_MODEL, 2 * tf).astype(BF16)


def _proj_b_body(h_ref, w_hbm, *rest):
    n_casts = (len(rest) - 5) // 2
    cast_src, (og_ref, ga_ref, gr_ref) = rest[:n_casts], rest[n_casts:n_casts + 3]
    cast_dst, (w_ref, sem) = rest[n_casts + 3:2 * n_casts + 3], rest[2 * n_casts + 3:]

    @pl.when(pl.program_id(0) == 0)
    def _():
        copy = pltpu.make_async_copy(w_hbm.at[:, COLS_A:], w_ref, sem)
        copy.start()
        copy.wait()

    def cols(start, width):
        return _dot(h_ref[...], w_ref[:, start:start + width])

    og_ref[...] = _silu(cols(0, HG_WIDTH)).astype(BF16)
    for c in range(D_MODEL // HG_WIDTH):
        sl = slice(c * HG_WIDTH, (c + 1) * HG_WIDTH)
        ga_ref[:, sl] = _sigmoid(cols(HG_WIDTH + c * HG_WIDTH, HG_WIDTH)).astype(BF16)
        gr_ref[:, sl] = _sigmoid(cols(HG_WIDTH + D_MODEL + c * HG_WIDTH, HG_WIDTH)).astype(BF16)
    _run_casts(cast_src, cast_dst)


def _proj_b(h, w_in, casts=(), *, tm):
    n = h.shape[0]
    row = lambda i: (i, 0)
    widths = (HG_WIDTH, D_MODEL, D_MODEL)
    cast_in, cast_out, cast_shapes = _cast_specs(casts, n // tm, FFN_TF)
    outs = pl.pallas_call(
        _proj_b_body,
        grid=(n // tm,),
        in_specs=[
            pl.BlockSpec((tm, D_MODEL), row),
            pl.BlockSpec(memory_space=pl.ANY),
        ] + cast_in,
        out_specs=[pl.BlockSpec((tm, w), row) for w in widths] + cast_out,
        out_shape=[jax.ShapeDtypeStruct((n, w), BF16) for w in widths] + cast_shapes,
        scratch_shapes=[pltpu.VMEM((D_MODEL, COLS_B), BF16), pltpu.SemaphoreType.DMA(())],
        compiler_params=pltpu.CompilerParams(
            dimension_semantics=("arbitrary",), vmem_limit_bytes=VMEM_LIMIT),
        name="proj_b",
    )(h, w_in, *casts)
    return outs[:3], outs[3:]


def _attn_body(sinks_ref, q_ref, k_ref, kp_ref, v_ref, vp_ref, o_ref, ks_ref, vs_ref, *, tq):
    t = pl.program_id(1)
    nb = tq // BLOCK
    lane = lax.broadcasted_iota(jnp.int32, (1, LANES), 1)
    low_head = lane < HEAD_DIM

    def put_kv(dst_ref, rows, v, pad):
        swapped = pltpu.roll(v, HEAD_DIM, 1)
        dst_ref[0, rows, :] = jnp.where(low_head, v, pad).astype(BF16)
        dst_ref[1, rows, :] = jnp.where(low_head, pad, swapped).astype(BF16)
        dst_ref[2, rows, :] = jnp.where(low_head, swapped, pad).astype(BF16)
        dst_ref[3, rows, :] = jnp.where(low_head, pad, v).astype(BF16)

    put_kv(ks_ref, slice(0, BLOCK), kp_ref[...].astype(F32), 0.0)
    put_kv(ks_ref, slice(BLOCK, BLOCK + tq), k_ref[...].astype(F32), 0.0)
    put_kv(vs_ref, slice(0, BLOCK), vp_ref[...].astype(F32), 1.0)
    put_kv(vs_ref, slice(BLOCK, BLOCK + tq), v_ref[...].astype(F32), 1.0)

    rows_q = PAIRS_PER_GROUP * BLOCK
    qi = lax.broadcasted_iota(jnp.int32, (rows_q, BLOCK), 0) % BLOCK
    kj = lax.broadcasted_iota(jnp.int32, (rows_q, BLOCK), 1)
    band_prev = kj > qi + BLOCK - WINDOW
    band_cur = kj <= qi
    sink_fill = {}
    for g in range(N_KV_HEADS):
        for half in range(2):
            sink_rows = jnp.concatenate(
                [jnp.full((BLOCK, LANES), sinks_ref[g * GQA_GROUP + 2 * p + half] * LOG2E, F32)
                 for p in range(PAIRS_PER_GROUP)], axis=0)
            sink_fill[g, half] = jnp.where(kj == 0, sink_rows, -jnp.inf)
    first_key = lax.broadcasted_iota(jnp.int32, (2 * BLOCK, LANES), 0) == 0
    drop_value = (first_key & low_head, first_key & jnp.logical_not(low_head))

    def block(n, carry):
        qrows = pl.ds(pl.multiple_of(n * BLOCK, BLOCK), BLOCK)
        krows = pl.ds(pl.multiple_of(n * BLOCK, BLOCK), 2 * BLOCK)
        valid_prev = band_prev & (t * nb + n > 0)

        def scores(g, half):
            qst = jnp.concatenate(
                [q_ref[qrows, (g * PAIRS_PER_GROUP + p) * LANES:(g * PAIRS_PER_GROUP + p + 1) * LANES]
                 for p in range(PAIRS_PER_GROUP)], axis=0)
            return _dot_nt(qst, ks_ref[2 * g + half, krows, :])

        units = [(g, half) for g in range(N_KV_HEADS) for half in range(2)]
        s_next = scores(*units[0])
        o = []
        for idx, (g, half) in enumerate(units):
            s = s_next
            if idx + 1 < len(units):
                s_next = scores(*units[idx + 1])
            s_prev = jnp.where(valid_prev, s[:, :BLOCK], sink_fill[g, half])
            s_cur = jnp.where(band_cur, s[:, BLOCK:], -jnp.inf)
            m = jnp.max(jnp.maximum(s_prev, s_cur), axis=-1, keepdims=True)
            p_ = jnp.concatenate([jnp.exp2(s_prev - m), jnp.exp2(s_cur - m)], axis=1).astype(BF16)
            vwin = vs_ref[2 * g + half, krows, :]
            vwin = jnp.where(drop_value[half], jnp.zeros_like(vwin), vwin)
            o.append(_dot(p_, vwin))
            if half == 1:
                num = jnp.where(low_head, o[0], o[1])
                den = pltpu.roll(jnp.where(low_head, o[1], o[0]), HEAD_DIM, 1)
                outs = num / den
                o = []
                for p in range(PAIRS_PER_GROUP):
                    c = g * PAIRS_PER_GROUP + p
                    o_ref[qrows, c * LANES:(c + 1) * LANES] = outs[p * BLOCK:(p + 1) * BLOCK].astype(BF16)
        return carry

    lax.fori_loop(0, nb, block, 0, unroll=min(4, nb))


def _attn(q, k, v, sinks, *, batch, seq, tq):
    n = batch * seq
    nt = seq // tq
    nb = tq // BLOCK
    bps = seq // BLOCK
    cur = lambda b, t: (b * nt + t, 0)
    prev = lambda b, t: (b * bps + jnp.maximum(t * nb - 1, 0), 0)
    return pl.pallas_call(
        functools.partial(_attn_body, tq=tq),
        grid=(batch, nt),
        in_specs=[
            pl.BlockSpec(memory_space=pltpu.SMEM),
            pl.BlockSpec((tq, ATTN_WIDTH), cur),
            pl.BlockSpec((tq, KV_WIDTH), cur),
            pl.BlockSpec((BLOCK, KV_WIDTH), prev),
            pl.BlockSpec((tq, KV_WIDTH), cur),
            pl.BlockSpec((BLOCK, KV_WIDTH), prev),
        ],
        out_specs=pl.BlockSpec((tq, ATTN_WIDTH), cur),
        out_shape=jax.ShapeDtypeStruct((n, ATTN_WIDTH), BF16),
        scratch_shapes=[
            pltpu.VMEM((4, tq + BLOCK, LANES), BF16),
            pltpu.VMEM((4, tq + BLOCK, LANES), BF16),
        ],
        compiler_params=pltpu.CompilerParams(
            dimension_semantics=("parallel", "parallel"), vmem_limit_bytes=VMEM_LIMIT),
        name="attn",
    )(sinks, q, k, k, v, v)


def _hgrn_body(hq_ref, hk_ref, hl_ref, hi_ref, og_ref, gain_ref, o_ref, state_ref, *, tile):
    @pl.when(pl.program_id(1) == 0)
    def _():
        state_ref[...] = jnp.zeros_like(state_ref)

    ri = lax.broadcasted_iota(jnp.int32, (CHUNK, CHUNK), 0)
    ci = lax.broadcasted_iota(jnp.int32, (CHUNK, CHUNK), 1)
    causal = ri >= ci
    tri = jnp.where(causal, 1.0, 0.0).astype(BF16)
    gain = gain_ref[...]

    def chunk(c, carry):
        rows = pl.ds(pl.multiple_of(c * CHUNK, CHUNK), CHUNK)
        lf = hl_ref[rows, :]
        hi = lf.astype(BF16)
        lo = (lf - hi.astype(F32)).astype(BF16)
        b = _dot(tri, hi) + _dot(tri, lo)
        b_mid = b[CHUNK // 2:CHUNK // 2 + 1]
        b_last = b[CHUNK - 1:CHUNK]
        qh = hq_ref[rows, :].astype(F32)
        kk = hk_ref[rows, :].astype(F32)
        q_mid = (qh * jnp.exp2(b - b_mid)).astype(BF16)
        k_mid = (kk * jnp.exp2(b_mid - b)).astype(BF16)
        q_in = (qh * jnp.exp2(b)).astype(BF16)
        k_out = (kk * jnp.exp2(b_last - b)).astype(BF16)
        decay = jnp.exp2(b_last)
        heads = [slice(h * HG_HEAD_DIM, (h + 1) * HG_HEAD_DIM) for h in range(HG_HEADS)]
        att = [_dot_nt(q_mid[:, sl], k_mid[:, sl]) for sl in heads]
        st = [state_ref[h] for h in range(HG_HEADS)]
        o_inter = [_dot_nt(q_in[:, sl], st[h].astype(BF16)) for h, sl in enumerate(heads)]
        upd = [_dot_tn(hi_ref[rows, sl], k_out[:, sl]) for sl in heads]
        o = [o_inter[h] + _dot(jnp.where(causal, att[h], 0.0).astype(BF16), hi_ref[rows, sl])
             for h, sl in enumerate(heads)]
        for h, sl in enumerate(heads):
            state_ref[h] = st[h] * decay[:, sl] + upd[h]
            y = _rms_rows(o[h], gain) * og_ref[rows, sl].astype(F32)
            o_ref[rows, sl] = y.astype(BF16)
        return carry

    lax.fori_loop(0, tile // CHUNK, chunk, 0, unroll=True)


def _hgrn(hq, hk, hl, hi, og, gain, *, batch, seq, tile):
    n = batch * seq
    nt = seq // tile
    cur = lambda b, t: (b * nt + t, 0)
    return pl.pallas_call(
        functools.partial(_hgrn_body, tile=tile),
        grid=(batch, nt),
        in_specs=[pl.BlockSpec((tile, HG_WIDTH), cur)] * 5 + [pl.BlockSpec((1, HG_HEAD_DIM), lambda b, t: (0, 0))],
        out_specs=pl.BlockSpec((tile, HG_WIDTH), cur),
        out_shape=jax.ShapeDtypeStruct((n, HG_WIDTH), BF16),
        scratch_shapes=[pltpu.VMEM((HG_HEADS, HG_HEAD_DIM, HG_HEAD_DIM), F32)],
        compiler_params=pltpu.CompilerParams(
            dimension_semantics=("parallel", "arbitrary"), vmem_limit_bytes=VMEM_LIMIT),
        name="hgrn",
    )(hq, hk, hl, hi, og, gain)


def _merge_body(x_ref, ya_ref, yh_ref, ga_ref, gr_ref, wa_ref, wr_ref, wo_ref, *rest):
    n_casts = (len(rest) - 1) // 2
    cast_src, o_ref, cast_dst = rest[:n_casts], rest[n_casts], rest[n_casts + 1:]
    a = _dot(ya_ref[...], wa_ref[...])
    r = _dot(yh_ref[...], wr_ref[...])
    merged = (ga_ref[...].astype(F32) * a + gr_ref[...].astype(F32) * r).astype(BF16)
    o_ref[...] = x_ref[...] + _dot(merged, wo_ref[...])
    _run_casts(cast_src, cast_dst)


def _merge(x, ya, yh, ga, gr, w_a, w_r, w_o, casts=(), *, tm):
    n = x.shape[0]
    row = lambda i: (i, 0)
    whole = lambda i: (0, 0)
    cast_in, cast_out, cast_shapes = _cast_specs(casts, n // tm, FFN_TF)
    outs = pl.pallas_call(
        _merge_body,
        grid=(n // tm,),
        in_specs=[
            pl.BlockSpec((tm, D_MODEL), row),
            pl.BlockSpec((tm, ATTN_WIDTH), row),
            pl.BlockSpec((tm, HG_WIDTH), row),
            pl.BlockSpec((tm, D_MODEL), row),
            pl.BlockSpec((tm, D_MODEL), row),
            pl.BlockSpec((ATTN_WIDTH, D_MODEL), whole, pipeline_mode=pl.Buffered(1)),
            pl.BlockSpec((HG_WIDTH, D_MODEL), whole, pipeline_mode=pl.Buffered(1)),
            pl.BlockSpec((D_MODEL, D_MODEL), whole, pipeline_mode=pl.Buffered(1)),
        ] + cast_in,
        out_specs=[pl.BlockSpec((tm, D_MODEL), row)] + cast_out,
        out_shape=[jax.ShapeDtypeStruct((n, D_MODEL), F32)] + cast_shapes,
        compiler_params=pltpu.CompilerParams(
            dimension_semantics=("arbitrary",), vmem_limit_bytes=VMEM_LIMIT),
        name="merge",
    )(x, ya, yh, ga, gr, w_a, w_r, w_o, *casts)
    return outs[0], outs[1:]


def _tile(n, want):
    t = min(want, n)
    assert n % t == 0, (n, t)
    return t


def kernel(x, positions, lb_table, ffn1_norm, ffn1_w_gu, ffn1_w_down, mix_norm, w_in, q_norm, k_norm, sinks,
           hg_out_norm, w_attn_branch, w_hg_branch, w_out, ffn2_norm, ffn2_w_gu, ffn2_w_down):
    batch, seq = x.shape[0], x.shape[1]
    n = batch * seq
    depth = w_in.shape[0]
    assert seq % BLOCK == 0 and seq % CHUNK == 0

    half = HEAD_DIM // 2
    inv_freq = (ROPE_THETA ** (-jnp.arange(half, dtype=F32) * 2.0 / HEAD_DIM))[:, None]
    tm_proj = _tile(n, 512)
    pos_rows = positions.astype(F32).reshape(n // tm_proj, 1, tm_proj)

    xf = x.reshape(n, D_MODEL)
    for l in range(depth):
        xf = _ffn(xf, ffn1_norm[l][None, :], _chunk_gate_up(ffn1_w_gu[l], FFN_TF), ffn1_w_down[l].astype(BF16),
                  tm=_tile(n, 1024))
        gain = mix_norm[l][None, :]
        w_in_l = w_in[l].astype(BF16)
        q, k, v, hq, hk, hl, hi, h_mix = _proj_a(
            xf, gain, w_in_l, pos_rows, inv_freq, jnp.tile(q_norm[l], 2)[None, :],
            jnp.tile(k_norm[l], 2)[None, :], lb_table, layer=l, tm=tm_proj)
        (og, ga, gr), (w_a, w_r, w_o) = _proj_b(
            h_mix, w_in_l, casts=(w_attn_branch[l], w_hg_branch[l], w_out[l]), tm=_tile(n, 1024))
        ya = _attn(q, k, v, sinks[l], batch=batch, seq=seq, tq=_tile(seq, 1024))
        yh = _hgrn(hq, hk, hl, hi, og, hg_out_norm[l][None, :], batch=batch, seq=seq, tile=_tile(seq, 1024))
        xf, (w_gu2, w_down2) = _merge(
            xf, ya, yh, ga, gr, w_a, w_r, w_o, casts=(ffn2_w_gu[l], ffn2_w_down[l]), tm=_tile(n, 512))
        xf = _ffn(xf, ffn2_norm[l][None, :], w_gu2, w_down2, tm=_tile(n, 1024))
    return xf.reshape(batch, seq, D_MODEL)
```

```python
import functools

import jax
import jax.numpy as jnp
from jax import lax
from jax.experimental import pallas as pl
from jax.experimental.pallas import tpu as pltpu

D_MODEL = 2048
HEAD_DIM = 64
N_Q_HEADS = 16
N_KV_HEADS = 2
GQA_GROUP = N_Q_HEADS // N_KV_HEADS
ATTN_WIDTH = N_Q_HEADS * HEAD_DIM
KV_WIDTH = N_KV_HEADS * HEAD_DIM
WINDOW = 128
BLOCK = 128
ROPE_THETA = 10000.0
HG_HEAD_DIM = 128
HG_HEADS = 8
HG_WIDTH = HG_HEADS * HG_HEAD_DIM
CHUNK = 64
D_FF = 5632
EPS = 1e-6

LANES = 128
PAIRS_PER_GROUP = GQA_GROUP // 2
VMEM_LIMIT = 58 * 1024 * 1024
LOG2E = 1.4426950408889634

N_ATTN = ATTN_WIDTH + 2 * KV_WIDTH
COLS_A = N_ATTN + 3 * HG_WIDTH
COLS_B = HG_WIDTH + 2 * D_MODEL

F32 = jnp.float32
BF16 = jnp.bfloat16


def _dot(a, b):
    return jnp.dot(a, b, preferred_element_type=F32)


def _dot_nt(a, b):
    return lax.dot_general(a, b, (((1,), (1,)), ((), ())), preferred_element_type=F32)


def _dot_tn(a, b):
    return lax.dot_general(a, b, (((0,), (0,)), ((), ())), preferred_element_type=F32)


def _sigmoid(z):
    return 0.5 * jnp.tanh(0.5 * z) + 0.5


def _silu(z):
    half = 0.5 * z
    return half * jnp.tanh(half) + half


def _rms_rows(x, gain):
    ms = jnp.mean(x * x, axis=-1, keepdims=True)
    return x * lax.rsqrt(ms + EPS) * gain


FFN_NORM_ROWS = 128
FFN_NORM_FIRST = 2


def _ffn_body(x_hbm, g_ref, wg_ref, wu_ref, wd_ref, o_ref, h0_ref, h1_ref, xs_ref, sem, *, tm, tf):
    i = pl.program_id(0)
    j = pl.program_id(1)
    n_tiles = pl.num_programs(0)
    hbufs = (h0_ref, h1_ref)

    def x_copy(tile):
        rows = pl.ds(pl.multiple_of(tile * tm, tm), tm)
        return pltpu.make_async_copy(x_hbm.at[rows], xs_ref.at[:tm], sem)

    @pl.when((i == 0) & (j == 0))
    def _():
        xs_ref[tm:, :] = jnp.zeros((FFN_NORM_ROWS, D_MODEL), F32)
        x_copy(0).start()
        x_copy(0).wait()
        h0_ref[:tm, :] = _rms_rows(xs_ref[:tm, :], g_ref[...]).astype(BF16)

    @pl.when((j == FFN_NORM_FIRST) & (i + 1 < n_tiles))
    def _():
        x_copy(i + 1).wait()

    def chunk_step(parity, first):
        h = hbufs[parity][:tm, :]
        halves = [slice(c * tf // 2, (c + 1) * tf // 2) for c in range(2)]
        gate_up = [(_dot(h, wg_ref[:, sl]), _dot(h, wu_ref[:, sl])) for sl in halves]
        act = [(_silu(g) * (u * 0.5)).astype(BF16) for g, u in gate_up]
        down = _dot(act[0], wd_ref[halves[0], :]) + _dot(act[1], wd_ref[halves[1], :])
        if first:
            o_ref[...] = xs_ref[:tm, :] + down

            @pl.when(i + 1 < n_tiles)
            def _():
                x_copy(i + 1).start()
        else:
            o_ref[...] += down
            part = j - FFN_NORM_FIRST
            live = (part < tm // FFN_NORM_ROWS) & (part >= 0) & (i + 1 < n_tiles)
            rows = pl.ds(pl.multiple_of(jnp.where(live, part * FFN_NORM_ROWS, tm), FFN_NORM_ROWS), FFN_NORM_ROWS)
            hbufs[1 - parity][rows, :] = _rms_rows(xs_ref[rows, :], g_ref[...]).astype(BF16)

    for parity in range(2):
        for first in (True, False):
            pl.when((i % 2 == parity) & ((j == 0) == first))(functools.partial(chunk_step, parity, first))


def _ffn(x, gain, w_gu, w_down, *, tm, tf):
    n = x.shape[0]
    nf = D_FF // tf
    assert tm % FFN_NORM_ROWS == 0 and nf >= FFN_NORM_FIRST + tm // FFN_NORM_ROWS
    return pl.pallas_call(
        functools.partial(_ffn_body, tm=tm, tf=tf),
        grid=(n // tm, nf),
        in_specs=[
            pl.BlockSpec(memory_space=pl.ANY),
            pl.BlockSpec((1, D_MODEL), lambda i, j: (0, 0)),
            pl.BlockSpec((D_MODEL, tf), lambda i, j: (0, j)),
            pl.BlockSpec((D_MODEL, tf), lambda i, j: (0, j + nf)),
            pl.BlockSpec((tf, D_MODEL), lambda i, j: (j, 0)),
        ],
        out_specs=pl.BlockSpec((tm, D_MODEL), lambda i, j: (i, 0)),
        out_shape=jax.ShapeDtypeStruct((n, D_MODEL), F32),
        scratch_shapes=[pltpu.VMEM((tm + FFN_NORM_ROWS, D_MODEL), BF16), pltpu.VMEM((tm + FFN_NORM_ROWS, D_MODEL), BF16),
                        pltpu.VMEM((tm + FFN_NORM_ROWS, D_MODEL), F32), pltpu.SemaphoreType.DMA(())],
        compiler_params=pltpu.CompilerParams(
            dimension_semantics=("arbitrary", "arbitrary"), vmem_limit_bytes=VMEM_LIMIT),
        name="ffn",
    )(x, gain, w_gu, w_gu, w_down)


def _proj_a_body(x_ref, g_ref, w_ref, pos_ref, invf_ref, qg_ref, kg_ref, lbt_ref,
                 q_ref, k_ref, v_ref, hq_ref, hk_ref, hl_ref, hi_ref, h_ref, *, layer):
    h = _rms_rows(x_ref[...], g_ref[...]).astype(BF16)
    h_ref[...] = h

    def cols(start, width):
        return _dot(h, w_ref[:, start:start + width])

    lane = lax.broadcasted_iota(jnp.int32, (1, LANES), 1)
    first_half = (lane % HEAD_DIM) < (HEAD_DIM // 2)
    gi = lax.broadcasted_iota(jnp.int32, (LANES, LANES), 0) // HEAD_DIM
    gj = lax.broadcasted_iota(jnp.int32, (LANES, LANES), 1) // HEAD_DIM
    gsum = jnp.where(gi == gj, 1.0, 0.0).astype(BF16)

    def head_norm(v, gain):
        sq = v * v
        hi = sq.astype(BF16)
        lo = (sq - hi.astype(F32)).astype(BF16)
        ss = _dot(hi, gsum) + _dot(lo, gsum)
        return v * lax.rsqrt(ss * (1.0 / HEAD_DIM) + EPS) * gain

    ang = invf_ref[...] * pos_ref[...]
    cos_t = jnp.cos(ang)
    sin_t = jnp.sin(ang)
    reps = LANES // HEAD_DIM
    cosf = jnp.concatenate([cos_t, cos_t] * reps, axis=0).T
    sin_signed = jnp.concatenate([-sin_t, sin_t] * reps, axis=0).T

    def rope(v):
        rot = jnp.where(first_half, pltpu.roll(v, LANES - HEAD_DIM // 2, 1), pltpu.roll(v, HEAD_DIM // 2, 1))
        return v * cosf + rot * sin_signed

    lbt = lbt_ref[...]
    e = jnp.exp(lbt - jnp.max(lbt, axis=0, keepdims=True))
    sm = e / jnp.sum(e, axis=0, keepdims=True)
    lb = jnp.sum(sm[:layer + 1], axis=0, keepdims=True)

    zq = cols(0, ATTN_WIDTH)
    zkv = cols(ATTN_WIDTH, 2 * KV_WIDTH)

    def attn_piece(c):
        if c < ATTN_WIDTH // LANES:
            sl = slice(c * LANES, (c + 1) * LANES)
            q_ref[:, sl] = (rope(head_norm(zq[:, sl], qg_ref[...])) * (HEAD_DIM ** -0.5 * LOG2E)).astype(BF16)
        else:
            k_ref[...] = rope(head_norm(zkv[:, :KV_WIDTH], kg_ref[...])).astype(BF16)
            v_ref[...] = zkv[:, KV_WIDTH:].astype(BF16)

    def hgrn_piece(kind, sl):
        z = cols(N_ATTN + kind * HG_WIDTH + sl.start, sl.stop - sl.start)
        if kind == 0:
            hq_ref[:, sl] = _silu(z).astype(BF16)
        elif kind == 1:
            f = lb[:, sl] + (1.0 - lb[:, sl]) * _sigmoid(z)
            hk_ref[:, sl] = (1.0 - f).astype(BF16)
            hl_ref[:, sl] = jnp.log2(f)
        else:
            hi_ref[:, sl] = z.astype(BF16)

    chunk = 2 * LANES
    pieces = [(kind, slice(c, c + chunk)) for kind in range(3) for c in range(0, HG_WIDTH, chunk)]
    n_attn_pieces = ATTN_WIDTH // LANES + 1
    for idx, (kind, sl) in enumerate(pieces):
        hgrn_piece(kind, sl)
        if idx < n_attn_pieces:
            attn_piece(idx)


def _proj_a(x, gain, w_a, pos_rows, inv_freq, q_gain, k_gain, lb_table, *, layer, tm):
    n = x.shape[0]
    row = lambda i: (i, 0)
    whole = lambda i: (0, 0)
    widths = (ATTN_WIDTH, KV_WIDTH, KV_WIDTH, HG_WIDTH, HG_WIDTH, HG_WIDTH, HG_WIDTH, D_MODEL)
    dtypes = (BF16, BF16, BF16, BF16, BF16, F32, BF16, BF16)
    return pl.pallas_call(
        functools.partial(_proj_a_body, layer=layer),
        grid=(n // tm,),
        in_specs=[
            pl.BlockSpec((tm, D_MODEL), row),
            pl.BlockSpec((1, D_MODEL), whole),
            pl.BlockSpec((D_MODEL, COLS_A), whole, pipeline_mode=pl.Buffered(1)),
            pl.BlockSpec((None, 1, tm), lambda i: (i, 0, 0)),
            pl.BlockSpec((HEAD_DIM // 2, 1), whole),
            pl.BlockSpec((1, LANES), whole),
            pl.BlockSpec((1, LANES), whole),
            pl.BlockSpec(lb_table.shape, whole),
        ],
        out_specs=[pl.BlockSpec((tm, w), row) for w in widths],
        out_shape=[jax.ShapeDtypeStruct((n, w), d) for w, d in zip(widths, dtypes)],
        compiler_params=pltpu.CompilerParams(
            dimension_semantics=("parallel",), vmem_limit_bytes=VMEM_LIMIT),
        name="proj_a",
    )(x, gain, w_a, pos_rows, inv_freq, q_gain, k_gain, lb_table)


BF16_SUBLANES = 16


def _cast_specs(casts, steps):
    specs = []
    for w in casts:
        rows = w.shape[0]
        rb = next(r for r in range(BF16_SUBLANES, rows + 1, BF16_SUBLANES) if rows % r == 0 and rows // r <= steps)
        specs.append(pl.BlockSpec(
            (rb, w.shape[1]), functools.partial(lambda i, last: (jnp.minimum(i, last), 0), last=rows // rb - 1)))
    return specs


def _run_casts(src_refs, dst_refs):
    for src_ref, dst_ref in zip(src_refs, dst_refs):
        dst_ref[...] = src_ref[...].astype(BF16)


def _proj_b_body(h_ref, w_hbm, *rest):
    n_casts = (len(rest) - 5) // 2
    cast_src, (og_ref, ga_ref, gr_ref) = rest[:n_casts], rest[n_casts:n_casts + 3]
    cast_dst, (w_ref, sem) = rest[n_casts + 3:2 * n_casts + 3], rest[2 * n_casts + 3:]

    @pl.when(pl.program_id(0) == 0)
    def _():
        copy = pltpu.make_async_copy(w_hbm.at[:, COLS_A:], w_ref, sem)
        copy.start()
        copy.wait()

    def cols(start, width):
        return _dot(h_ref[...], w_ref[:, start:start + width])

    og_ref[...] = _silu(cols(0, HG_WIDTH)).astype(BF16)
    for c in range(D_MODEL // HG_WIDTH):
        sl = slice(c * HG_WIDTH, (c + 1) * HG_WIDTH)
        ga_ref[:, sl] = _sigmoid(cols(HG_WIDTH + c * HG_WIDTH, HG_WIDTH)).astype(BF16)
        gr_ref[:, sl] = _sigmoid(cols(HG_WIDTH + D_MODEL + c * HG_WIDTH, HG_WIDTH)).astype(BF16)
    _run_casts(cast_src, cast_dst)


def _proj_b(h, w_in, casts=(), *, tm):
    n = h.shape[0]
    row = lambda i: (i, 0)
    widths = (HG_WIDTH, D_MODEL, D_MODEL)
    cast_specs = _cast_specs(casts, n // tm)
    outs = pl.pallas_call(
        _proj_b_body,
        grid=(n // tm,),
        in_specs=[
            pl.BlockSpec((tm, D_MODEL), row),
            pl.BlockSpec(memory_space=pl.ANY),
        ] + cast_specs,
        out_specs=[pl.BlockSpec((tm, w), row) for w in widths] + cast_specs,
        out_shape=([jax.ShapeDtypeStruct((n, w), BF16) for w in widths]
                   + [jax.ShapeDtypeStruct(w.shape, BF16) for w in casts]),
        scratch_shapes=[pltpu.VMEM((D_MODEL, COLS_B), BF16), pltpu.SemaphoreType.DMA(())],
        compiler_params=pltpu.CompilerParams(
            dimension_semantics=("arbitrary",), vmem_limit_bytes=VMEM_LIMIT),
        name="proj_b",
    )(h, w_in, *casts)
    return outs[:3], outs[3:]


def _attn_body(sinks_ref, q_ref, k_ref, kp_ref, v_ref, vp_ref, o_ref, ks_ref, vs_ref, *, tq):
    t = pl.program_id(1)
    nb = tq // BLOCK
    lane = lax.broadcasted_iota(jnp.int32, (1, LANES), 1)
    low_head = lane < HEAD_DIM

    def put_kv(dst_ref, rows, v, pad):
        swapped = pltpu.roll(v, HEAD_DIM, 1)
        dst_ref[0, rows, :] = jnp.where(low_head, v, pad).astype(BF16)
        dst_ref[1, rows, :] = jnp.where(low_head, pad, swapped).astype(BF16)
        dst_ref[2, rows, :] = jnp.where(low_head, swapped, pad).astype(BF16)
        dst_ref[3, rows, :] = jnp.where(low_head, pad, v).astype(BF16)

    put_kv(ks_ref, slice(0, BLOCK), kp_ref[...].astype(F32), 0.0)
    put_kv(ks_ref, slice(BLOCK, BLOCK + tq), k_ref[...].astype(F32), 0.0)
    put_kv(vs_ref, slice(0, BLOCK), vp_ref[...].astype(F32), 1.0)
    put_kv(vs_ref, slice(BLOCK, BLOCK + tq), v_ref[...].astype(F32), 1.0)

    rows_q = PAIRS_PER_GROUP * BLOCK
    qi = lax.broadcasted_iota(jnp.int32, (rows_q, BLOCK), 0) % BLOCK
    kj = lax.broadcasted_iota(jnp.int32, (rows_q, BLOCK), 1)
    band_prev = kj > qi + BLOCK - WINDOW
    band_cur = kj <= qi
    sink_fill = {}
    for g in range(N_KV_HEADS):
        for half in range(2):
            sink_rows = jnp.concatenate(
                [jnp.full((BLOCK, LANES), sinks_ref[g * GQA_GROUP + 2 * p + half] * LOG2E, F32)
                 for p in range(PAIRS_PER_GROUP)], axis=0)
            sink_fill[g, half] = jnp.where(kj == 0, sink_rows, -jnp.inf)
    first_key = lax.broadcasted_iota(jnp.int32, (2 * BLOCK, LANES), 0) == 0
    drop_value = (first_key & low_head, first_key & jnp.logical_not(low_head))

    def block(n, carry):
        qrows = pl.ds(pl.multiple_of(n * BLOCK, BLOCK), BLOCK)
        krows = pl.ds(pl.multiple_of(n * BLOCK, BLOCK), 2 * BLOCK)
        valid_prev = band_prev & (t * nb + n > 0)

        def scores(g, half):
            qst = jnp.concatenate(
                [q_ref[qrows, (g * PAIRS_PER_GROUP + p) * LANES:(g * PAIRS_PER_GROUP + p + 1) * LANES]
                 for p in range(PAIRS_PER_GROUP)], axis=0)
            return _dot_nt(qst, ks_ref[2 * g + half, krows, :])

        units = [(g, half) for g in range(N_KV_HEADS) for half in range(2)]
        s_next = scores(*units[0])
        o = []
        for idx, (g, half) in enumerate(units):
            s = s_next
            if idx + 1 < len(units):
                s_next = scores(*units[idx + 1])
            s_prev = jnp.where(valid_prev, s[:, :BLOCK], sink_fill[g, half])
            s_cur = jnp.where(band_cur, s[:, BLOCK:], -jnp.inf)
            m = jnp.max(jnp.maximum(s_prev, s_cur), axis=-1, keepdims=True)
            p_ = jnp.concatenate([jnp.exp2(s_prev - m), jnp.exp2(s_cur - m)], axis=1).astype(BF16)
            vwin = vs_ref[2 * g + half, krows, :]
            vwin = jnp.where(drop_value[half], jnp.zeros_like(vwin), vwin)
            o.append(_dot(p_, vwin))
            if half == 1:
                num = jnp.where(low_head, o[0], o[1])
                den = pltpu.roll(jnp.where(low_head, o[1], o[0]), HEAD_DIM, 1)
                outs = num / den
                o = []
                for p in range(PAIRS_PER_GROUP):
                    c = g * PAIRS_PER_GROUP + p
                    o_ref[qrows, c * LANES:(c + 1) * LANES] = outs[p * BLOCK:(p + 1) * BLOCK].astype(BF16)
        return carry

    lax.fori_loop(0, nb, block, 0, unroll=min(4, nb))


def _attn(q, k, v, sinks, *, batch, seq, tq):
    n = batch * seq
    nt = seq // tq
    nb = tq // BLOCK
    bps = seq // BLOCK
    cur = lambda b, t: (b * nt + t, 0)
    prev = lambda b, t: (b * bps + jnp.maximum(t * nb - 1, 0), 0)
    return pl.pallas_call(
        functools.partial(_attn_body, tq=tq),
        grid=(batch, nt),
        in_specs=[
            pl.BlockSpec(memory_space=pltpu.SMEM),
            pl.BlockSpec((tq, ATTN_WIDTH), cur),
            pl.BlockSpec((tq, KV_WIDTH), cur),
            pl.BlockSpec((BLOCK, KV_WIDTH), prev),
            pl.BlockSpec((tq, KV_WIDTH), cur),
            pl.BlockSpec((BLOCK, KV_WIDTH), prev),
        ],
        out_specs=pl.BlockSpec((tq, ATTN_WIDTH), cur),
        out_shape=jax.ShapeDtypeStruct((n, ATTN_WIDTH), BF16),
        scratch_shapes=[
            pltpu.VMEM((4, tq + BLOCK, LANES), BF16),
            pltpu.VMEM((4, tq + BLOCK, LANES), BF16),
        ],
        compiler_params=pltpu.CompilerParams(
            dimension_semantics=("parallel", "parallel"), vmem_limit_bytes=VMEM_LIMIT),
        name="attn",
    )(sinks, q, k, k, v, v)


def _hgrn_body(hq_ref, hk_ref, hl_ref, hi_ref, og_ref, gain_ref, o_ref, state_ref, *, tile):
    n_seq = hq_ref.shape[0]

    @pl.when(pl.program_id(0) == 0)
    def _():
        state_ref[...] = jnp.zeros_like(state_ref)

    ri = lax.broadcasted_iota(jnp.int32, (CHUNK, CHUNK), 0)
    ci = lax.broadcasted_iota(jnp.int32, (CHUNK, CHUNK), 1)
    causal = ri >= ci
    tri = jnp.where(causal, 1.0, 0.0).astype(BF16)
    gain = gain_ref[...]
    heads = [slice(h * HG_HEAD_DIM, (h + 1) * HG_HEAD_DIM) for h in range(HG_HEADS)]
    units = [(s, h) for s in range(n_seq) for h in range(HG_HEADS)]

    def chunk(c, carry):
        rows = pl.ds(pl.multiple_of(c * CHUNK, CHUNK), CHUNK)
        q_mid, k_mid, q_in, k_out, decay = [], [], [], [], []
        for s in range(n_seq):
            lf = hl_ref[s, rows, :]
            hi = lf.astype(BF16)
            lo = (lf - hi.astype(F32)).astype(BF16)
            b = _dot(tri, hi) + _dot(tri, lo)
            b_mid = b[CHUNK // 2:CHUNK // 2 + 1]
            b_last = b[CHUNK - 1:CHUNK]
            qh = hq_ref[s, rows, :].astype(F32)
            kk = hk_ref[s, rows, :].astype(F32)
            q_mid.append((qh * jnp.exp2(b - b_mid)).astype(BF16))
            k_mid.append((kk * jnp.exp2(b_mid - b)).astype(BF16))
            q_in.append((qh * jnp.exp2(b)).astype(BF16))
            k_out.append((kk * jnp.exp2(b_last - b)).astype(BF16))
            decay.append(jnp.exp2(b_last))
        att = [_dot_nt(q_mid[s][:, heads[h]], k_mid[s][:, heads[h]]) for s, h in units]
        st = [state_ref[s, h] for s, h in units]
        o_inter = [_dot_nt(q_in[s][:, heads[h]], st[u].astype(BF16)) for u, (s, h) in enumerate(units)]
        upd = [_dot_tn(hi_ref[s, rows, heads[h]], k_out[s][:, heads[h]]) for s, h in units]
        o = [o_inter[u] + _dot(jnp.where(causal, att[u], 0.0).astype(BF16), hi_ref[s, rows, heads[h]])
             for u, (s, h) in enumerate(units)]
        for u, (s, h) in enumerate(units):
            state_ref[s, h] = st[u] * decay[s][:, heads[h]] + upd[u]
            y = _rms_rows(o[u], gain) * og_ref[s, rows, heads[h]].astype(F32)
            o_ref[s, rows, heads[h]] = y.astype(BF16)
        return carry

    lax.fori_loop(0, tile // CHUNK, chunk, 0, unroll=True)


def _hgrn(hq, hk, hl, hi, og, gain, *, batch, seq, tile):
    n = batch * seq
    cur = lambda t: (0, t, 0)
    ins = [a.reshape(batch, seq, HG_WIDTH) for a in (hq, hk, hl, hi, og)]
    out = pl.pallas_call(
        functools.partial(_hgrn_body, tile=tile),
        grid=(seq // tile,),
        in_specs=[pl.BlockSpec((batch, tile, HG_WIDTH), cur)] * 5 + [pl.BlockSpec((1, HG_HEAD_DIM), lambda t: (0, 0))],
        out_specs=pl.BlockSpec((batch, tile, HG_WIDTH), cur),
        out_shape=jax.ShapeDtypeStruct((batch, seq, HG_WIDTH), BF16),
        scratch_shapes=[pltpu.VMEM((batch, HG_HEADS, HG_HEAD_DIM, HG_HEAD_DIM), F32)],
        compiler_params=pltpu.CompilerParams(
            dimension_semantics=("arbitrary",), vmem_limit_bytes=VMEM_LIMIT),
        name="hgrn",
    )(*ins, gain)
    return out.reshape(n, HG_WIDTH)


def _merge_body(x_ref, ya_ref, yh_ref, ga_ref, gr_ref, wa_ref, wr_ref, wo_ref, *rest):
    n_casts = (len(rest) - 1) // 2
    cast_src, o_ref, cast_dst = rest[:n_casts], rest[n_casts], rest[n_casts + 1:]
    a = _dot(ya_ref[...], wa_ref[...])
    r = _dot(yh_ref[...], wr_ref[...])
    merged = (ga_ref[...].astype(F32) * a + gr_ref[...].astype(F32) * r).astype(BF16)
    o_ref[...] = x_ref[...] + _dot(merged, wo_ref[...])
    _run_casts(cast_src, cast_dst)


def _merge(x, ya, yh, ga, gr, w_a, w_r, w_o, casts=(), *, tm):
    n = x.shape[0]
    row = lambda i: (i, 0)
    whole = lambda i: (0, 0)
    cast_specs = _cast_specs(casts, n // tm)
    outs = pl.pallas_call(
        _merge_body,
        grid=(n // tm,),
        in_specs=[
            pl.BlockSpec((tm, D_MODEL), row),
            pl.BlockSpec((tm, ATTN_WIDTH), row),
            pl.BlockSpec((tm, HG_WIDTH), row),
            pl.BlockSpec((tm, D_MODEL), row),
            pl.BlockSpec((tm, D_MODEL), row),
            pl.BlockSpec((ATTN_WIDTH, D_MODEL), whole, pipeline_mode=pl.Buffered(1)),
            pl.BlockSpec((HG_WIDTH, D_MODEL), whole, pipeline_mode=pl.Buffered(1)),
            pl.BlockSpec((D_MODEL, D_MODEL), whole, pipeline_mode=pl.Buffered(1)),
        ] + cast_specs,
        out_specs=[pl.BlockSpec((tm, D_MODEL), row)] + cast_specs,
        out_shape=[jax.ShapeDtypeStruct((n, D_MODEL), F32)] + [jax.ShapeDtypeStruct(w.shape, BF16) for w in casts],
        compiler_params=pltpu.CompilerParams(
            dimension_semantics=("arbitrary",), vmem_limit_bytes=VMEM_LIMIT),
        name="merge",
    )(x, ya, yh, ga, gr, w_a, w_r, w_o, *casts)
    return outs[0], outs[1:]


def _tile(n, want):
    t = min(want, n)
    assert n % t == 0, (n, t)
    return t


def kernel(x, positions, lb_table, ffn1_norm, ffn1_w_gu, ffn1_w_down, mix_norm, w_in, q_norm, k_norm, sinks,
           hg_out_norm, w_attn_branch, w_hg_branch, w_out, ffn2_norm, ffn2_w_gu, ffn2_w_down):
    batch, seq = x.shape[0], x.shape[1]
    n = batch * seq
    depth = w_in.shape[0]
    assert seq % BLOCK == 0 and seq % CHUNK == 0

    half = HEAD_DIM // 2
    inv_freq = (ROPE_THETA ** (-jnp.arange(half, dtype=F32) * 2.0 / HEAD_DIM))[:, None]
    tm_proj = _tile(n, 512)
    pos_rows = positions.astype(F32).reshape(n // tm_proj, 1, tm_proj)

    xf = x.reshape(n, D_MODEL)
    for l in range(depth):
        xf = _ffn(xf, ffn1_norm[l][None, :], ffn1_w_gu[l].astype(BF16), ffn1_w_down[l].astype(BF16),
                  tm=_tile(n, 1024), tf=512)
        gain = mix_norm[l][None, :]
        w_in_l = w_in[l].astype(BF16)
        q, k, v, hq, hk, hl, hi, h_mix = _proj_a(
            xf, gain, w_in_l, pos_rows, inv_freq, jnp.tile(q_norm[l], 2)[None, :],
            jnp.tile(k_norm[l], 2)[None, :], lb_table, layer=l, tm=tm_proj)
        (og, ga, gr), (w_a, w_r, w_o) = _proj_b(
            h_mix, w_in_l, casts=(w_attn_branch[l], w_hg_branch[l], w_out[l]), tm=_tile(n, 1024))
        ya = _attn(q, k, v, sinks[l], batch=batch, seq=seq, tq=_tile(seq, 1024))
        yh = _hgrn(hq, hk, hl, hi, og, hg_out_norm[l][None, :], batch=batch, seq=seq, tile=_tile(seq, 512))
        xf, (w_gu2, w_down2) = _merge(
            xf, ya, yh, ga, gr, w_a, w_r, w_o, casts=(ffn2_w_gu[l], ffn2_w_down[l]), tm=_tile(n, 512))
        xf = _ffn(xf, ffn2_norm[l][None, :], w_gu2, w_down2, tm=_tile(n, 1024), tf=512)
    return xf.reshape(batch, seq, D_MODEL)
```

```python
import functools

import jax
import jax.numpy as jnp
from jax import lax
from jax.experimental import pallas as pl
from jax.experimental.pallas import tpu as pltpu

D_MODEL = 2048
HEAD_DIM = 64
N_Q_HEADS = 16
N_KV_HEADS = 2
GQA_GROUP = N_Q_HEADS // N_KV_HEADS
ATTN_WIDTH = N_Q_HEADS * HEAD_DIM
KV_WIDTH = N_KV_HEADS * HEAD_DIM
WINDOW = 128
BLOCK = 128
ROPE_THETA = 10000.0
HG_HEAD_DIM = 128
HG_HEADS = 8
HG_WIDTH = HG_HEADS * HG_HEAD_DIM
CHUNK = 64
D_FF = 5632
EPS = 1e-6

LANES = 128
PAIRS_PER_GROUP = GQA_GROUP // 2
VMEM_LIMIT = 58 * 1024 * 1024
LOG2E = 1.4426950408889634

N_ATTN = ATTN_WIDTH + 2 * KV_WIDTH
COLS_A = N_ATTN + 3 * HG_WIDTH
COLS_B = HG_WIDTH + 2 * D_MODEL

F32 = jnp.float32
BF16 = jnp.bfloat16


def _dot(a, b):
    return jnp.dot(a, b, preferred_element_type=F32)


def _dot_nt(a, b):
    return lax.dot_general(a, b, (((1,), (1,)), ((), ())), preferred_element_type=F32)


def _dot_tn(a, b):
    return lax.dot_general(a, b, (((0,), (0,)), ((), ())), preferred_element_type=F32)


def _sigmoid(z):
    return 0.5 * jnp.tanh(0.5 * z) + 0.5


def _silu(z):
    half = 0.5 * z
    return half * jnp.tanh(half) + half


def _rms_rows(x, gain):
    ms = jnp.mean(x * x, axis=-1, keepdims=True)
    return x * lax.rsqrt(ms + EPS) * gain


FFN_NORM_ROWS = 128
FFN_NORM_FIRST = 2


def _ffn_body(x_hbm, g_ref, wg_ref, wu_ref, wd_ref, o_ref, h0_ref, h1_ref, xs_ref, sem, *, tm, tf):
    i = pl.program_id(0)
    j = pl.program_id(1)
    n_tiles = pl.num_programs(0)
    hbufs = (h0_ref, h1_ref)

    def x_copy(tile):
        rows = pl.ds(pl.multiple_of(tile * tm, tm), tm)
        return pltpu.make_async_copy(x_hbm.at[rows], xs_ref.at[:tm], sem)

    @pl.when((i == 0) & (j == 0))
    def _():
        xs_ref[tm:, :] = jnp.zeros((FFN_NORM_ROWS, D_MODEL), F32)
        x_copy(0).start()
        x_copy(0).wait()
        h0_ref[:tm, :] = _rms_rows(xs_ref[:tm, :], g_ref[...]).astype(BF16)

    @pl.when((j == FFN_NORM_FIRST) & (i + 1 < n_tiles))
    def _():
        x_copy(i + 1).wait()

    def chunk_step(parity, first):
        h = hbufs[parity][:tm, :]
        halves = [slice(c * tf // 2, (c + 1) * tf // 2) for c in range(2)]
        gate_up = [(_dot(h, wg_ref[:, sl]), _dot(h, wu_ref[:, sl])) for sl in halves]
        act = [(_silu(g) * (u * 0.5)).astype(BF16) for g, u in gate_up]
        down = _dot(act[0], wd_ref[halves[0], :]) + _dot(act[1], wd_ref[halves[1], :])
        if first:
            o_ref[...] = xs_ref[:tm, :] + down

            @pl.when(i + 1 < n_tiles)
            def _():
                x_copy(i + 1).start()
        else:
            o_ref[...] += down
            part = j - FFN_NORM_FIRST
            live = (part < tm // FFN_NORM_ROWS) & (part >= 0) & (i + 1 < n_tiles)
            rows = pl.ds(pl.multiple_of(jnp.where(live, part * FFN_NORM_ROWS, tm), FFN_NORM_ROWS), FFN_NORM_ROWS)
            hbufs[1 - parity][rows, :] = _rms_rows(xs_ref[rows, :], g_ref[...]).astype(BF16)

    for parity in range(2):
        for first in (True, False):
            pl.when((i % 2 == parity) & ((j == 0) == first))(functools.partial(chunk_step, parity, first))


def _ffn(x, gain, w_gu, w_down, *, tm, tf):
    n = x.shape[0]
    nf = D_FF // tf
    assert tm % FFN_NORM_ROWS == 0 and nf >= FFN_NORM_FIRST + tm // FFN_NORM_ROWS
    return pl.pallas_call(
        functools.partial(_ffn_body, tm=tm, tf=tf),
        grid=(n // tm, nf),
        in_specs=[
            pl.BlockSpec(memory_space=pl.ANY),
            pl.BlockSpec((1, D_MODEL), lambda i, j: (0, 0)),
            pl.BlockSpec((D_MODEL, tf), lambda i, j: (0, j)),
            pl.BlockSpec((D_MODEL, tf), lambda i, j: (0, j + nf)),
            pl.BlockSpec((tf, D_MODEL), lambda i, j: (j, 0)),
        ],
        out_specs=pl.BlockSpec((tm, D_MODEL), lambda i, j: (i, 0)),
        out_shape=jax.ShapeDtypeStruct((n, D_MODEL), F32),
        scratch_shapes=[pltpu.VMEM((tm + FFN_NORM_ROWS, D_MODEL), BF16), pltpu.VMEM((tm + FFN_NORM_ROWS, D_MODEL), BF16),
                        pltpu.VMEM((tm + FFN_NORM_ROWS, D_MODEL), F32), pltpu.SemaphoreType.DMA(())],
        compiler_params=pltpu.CompilerParams(
            dimension_semantics=("arbitrary", "arbitrary"), vmem_limit_bytes=VMEM_LIMIT),
        name="ffn",
    )(x, gain, w_gu, w_gu, w_down)


def _proj_a_body(x_ref, g_ref, w_ref, pos_ref, invf_ref, qg_ref, kg_ref, lbt_ref,
                 q_ref, k_ref, v_ref, hq_ref, hk_ref, hl_ref, hi_ref, h_ref, *, layer):
    h = _rms_rows(x_ref[...], g_ref[...]).astype(BF16)
    h_ref[...] = h

    def cols(start, width):
        return _dot(h, w_ref[:, start:start + width])

    lane = lax.broadcasted_iota(jnp.int32, (1, LANES), 1)
    first_half = (lane % HEAD_DIM) < (HEAD_DIM // 2)
    gi = lax.broadcasted_iota(jnp.int32, (LANES, LANES), 0) // HEAD_DIM
    gj = lax.broadcasted_iota(jnp.int32, (LANES, LANES), 1) // HEAD_DIM
    gsum = jnp.where(gi == gj, 1.0, 0.0).astype(BF16)

    def head_norm(v, gain):
        sq = v * v
        hi = sq.astype(BF16)
        lo = (sq - hi.astype(F32)).astype(BF16)
        ss = _dot(hi, gsum) + _dot(lo, gsum)
        return v * lax.rsqrt(ss * (1.0 / HEAD_DIM) + EPS) * gain

    ang = invf_ref[...] * pos_ref[...]
    cos_t = jnp.cos(ang)
    sin_t = jnp.sin(ang)
    reps = LANES // HEAD_DIM
    cosf = jnp.concatenate([cos_t, cos_t] * reps, axis=0).T
    sin_signed = jnp.concatenate([-sin_t, sin_t] * reps, axis=0).T

    def rope(v):
        rot = jnp.where(first_half, pltpu.roll(v, LANES - HEAD_DIM // 2, 1), pltpu.roll(v, HEAD_DIM // 2, 1))
        return v * cosf + rot * sin_signed

    lbt = lbt_ref[...]
    e = jnp.exp(lbt - jnp.max(lbt, axis=0, keepdims=True))
    sm = e / jnp.sum(e, axis=0, keepdims=True)
    lb = jnp.sum(sm[:layer + 1], axis=0, keepdims=True)

    zq = cols(0, ATTN_WIDTH)
    zkv = cols(ATTN_WIDTH, 2 * KV_WIDTH)

    def attn_piece(c):
        if c < ATTN_WIDTH // LANES:
            sl = slice(c * LANES, (c + 1) * LANES)
            q_ref[:, sl] = (rope(head_norm(zq[:, sl], qg_ref[...])) * (HEAD_DIM ** -0.5 * LOG2E)).astype(BF16)
        else:
            k_ref[...] = rope(head_norm(zkv[:, :KV_WIDTH], kg_ref[...])).astype(BF16)
            v_ref[...] = zkv[:, KV_WIDTH:].astype(BF16)

    def hgrn_piece(kind, sl):
        z = cols(N_ATTN + kind * HG_WIDTH + sl.start, sl.stop - sl.start)
        if kind == 0:
            hq_ref[:, sl] = _silu(z).astype(BF16)
        elif kind == 1:
            f = lb[:, sl] + (1.0 - lb[:, sl]) * _sigmoid(z)
            hk_ref[:, sl] = (1.0 - f).astype(BF16)
            hl_ref[:, sl] = jnp.log2(f)
        else:
            hi_ref[:, sl] = z.astype(BF16)

    chunk = 2 * LANES
    pieces = [(kind, slice(c, c + chunk)) for kind in range(3) for c in range(0, HG_WIDTH, chunk)]
    n_attn_pieces = ATTN_WIDTH // LANES + 1
    for idx, (kind, sl) in enumerate(pieces):
        hgrn_piece(kind, sl)
        if idx < n_attn_pieces:
            attn_piece(idx)


def _proj_a(x, gain, w_a, pos_rows, inv_freq, q_gain, k_gain, lb_table, *, layer, tm):
    n = x.shape[0]
    row = lambda i: (i, 0)
    whole = lambda i: (0, 0)
    widths = (ATTN_WIDTH, KV_WIDTH, KV_WIDTH, HG_WIDTH, HG_WIDTH, HG_WIDTH, HG_WIDTH, D_MODEL)
    dtypes = (BF16, BF16, BF16, BF16, BF16, F32, BF16, BF16)
    return pl.pallas_call(
        functools.partial(_proj_a_body, layer=layer),
        grid=(n // tm,),
        in_specs=[
            pl.BlockSpec((tm, D_MODEL), row),
            pl.BlockSpec((1, D_MODEL), whole),
            pl.BlockSpec((D_MODEL, COLS_A), whole, pipeline_mode=pl.Buffered(1)),
            pl.BlockSpec((None, 1, tm), lambda i: (i, 0, 0)),
            pl.BlockSpec((HEAD_DIM // 2, 1), whole),
            pl.BlockSpec((1, LANES), whole),
            pl.BlockSpec((1, LANES), whole),
            pl.BlockSpec(lb_table.shape, whole),
        ],
        out_specs=[pl.BlockSpec((tm, w), row) for w in widths],
        out_shape=[jax.ShapeDtypeStruct((n, w), d) for w, d in zip(widths, dtypes)],
        compiler_params=pltpu.CompilerParams(
            dimension_semantics=("parallel",), vmem_limit_bytes=VMEM_LIMIT),
        name="proj_a",
    )(x, gain, w_a, pos_rows, inv_freq, q_gain, k_gain, lb_table)


BF16_SUBLANES = 16


def _cast_specs(casts, steps):
    specs = []
    for w in casts:
        rows = w.shape[0]
        rb = next(r for r in range(BF16_SUBLANES, rows + 1, BF16_SUBLANES) if rows % r == 0 and rows // r <= steps)
        specs.append(pl.BlockSpec(
            (rb, w.shape[1]), functools.partial(lambda i, last: (jnp.minimum(i, last), 0), last=rows // rb - 1)))
    return specs


def _run_casts(src_refs, dst_refs):
    for src_ref, dst_ref in zip(src_refs, dst_refs):
        dst_ref[...] = src_ref[...].astype(BF16)


def _proj_b_body(h_ref, w_hbm, *rest):
    n_casts = (len(rest) - 5) // 2
    cast_src, (og_ref, ga_ref, gr_ref) = rest[:n_casts], rest[n_casts:n_casts + 3]
    cast_dst, (w_ref, sem) = rest[n_casts + 3:2 * n_casts + 3], rest[2 * n_casts + 3:]

    @pl.when(pl.program_id(0) == 0)
    def _():
        copy = pltpu.make_async_copy(w_hbm.at[:, COLS_A:], w_ref, sem)
        copy.start()
        copy.wait()

    def cols(start, width):
        return _dot(h_ref[...], w_ref[:, start:start + width])

    og_ref[...] = _silu(cols(0, HG_WIDTH)).astype(BF16)
    for c in range(D_MODEL // HG_WIDTH):
        sl = slice(c * HG_WIDTH, (c + 1) * HG_WIDTH)
        ga_ref[:, sl] = _sigmoid(cols(HG_WIDTH + c * HG_WIDTH, HG_WIDTH)).astype(BF16)
        gr_ref[:, sl] = _sigmoid(cols(HG_WIDTH + D_MODEL + c * HG_WIDTH, HG_WIDTH)).astype(BF16)
    _run_casts(cast_src, cast_dst)


def _proj_b(h, w_in, casts=(), *, tm):
    n = h.shape[0]
    row = lambda i: (i, 0)
    widths = (HG_WIDTH, D_MODEL, D_MODEL)
    cast_specs = _cast_specs(casts, n // tm)
    outs = pl.pallas_call(
        _proj_b_body,
        grid=(n // tm,),
        in_specs=[
            pl.BlockSpec((tm, D_MODEL), row),
            pl.BlockSpec(memory_space=pl.ANY),
        ] + cast_specs,
        out_specs=[pl.BlockSpec((tm, w), row) for w in widths] + cast_specs,
        out_shape=([jax.ShapeDtypeStruct((n, w), BF16) for w in widths]
                   + [jax.ShapeDtypeStruct(w.shape, BF16) for w in casts]),
        scratch_shapes=[pltpu.VMEM((D_MODEL, COLS_B), BF16), pltpu.SemaphoreType.DMA(())],
        compiler_params=pltpu.CompilerParams(
            dimension_semantics=("arbitrary",), vmem_limit_bytes=VMEM_LIMIT),
        name="proj_b",
    )(h, w_in, *casts)
    return outs[:3], outs[3:]


def _attn_body(sinks_ref, q_ref, k_ref, kp_ref, v_ref, vp_ref, o_ref, ks_ref, vs_ref, *, tq):
    t = pl.program_id(1)
    nb = tq // BLOCK
    lane = lax.broadcasted_iota(jnp.int32, (1, LANES), 1)
    low_head = lane < HEAD_DIM

    def put_kv(dst_ref, rows, v, pad):
        swapped = pltpu.roll(v, HEAD_DIM, 1)
        dst_ref[0, rows, :] = jnp.where(low_head, v, pad).astype(BF16)
        dst_ref[1, rows, :] = jnp.where(low_head, pad, swapped).astype(BF16)
        dst_ref[2, rows, :] = jnp.where(low_head, swapped, pad).astype(BF16)
        dst_ref[3, rows, :] = jnp.where(low_head, pad, v).astype(BF16)

    put_kv(ks_ref, slice(0, BLOCK), kp_ref[...].astype(F32), 0.0)
    put_kv(ks_ref, slice(BLOCK, BLOCK + tq), k_ref[...].astype(F32), 0.0)
    put_kv(vs_ref, slice(0, BLOCK), vp_ref[...].astype(F32), 1.0)
    put_kv(vs_ref, slice(BLOCK, BLOCK + tq), v_ref[...].astype(F32), 1.0)

    rows_q = PAIRS_PER_GROUP * BLOCK
    qi = lax.broadcasted_iota(jnp.int32, (rows_q, BLOCK), 0) % BLOCK
    kj = lax.broadcasted_iota(jnp.int32, (rows_q, BLOCK), 1)
    band_prev = kj > qi + BLOCK - WINDOW
    band_cur = kj <= qi
    sink_fill = {}
    for g in range(N_KV_HEADS):
        for half in range(2):
            sink_rows = jnp.concatenate(
                [jnp.full((BLOCK, LANES), sinks_ref[g * GQA_GROUP + 2 * p + half] * LOG2E, F32)
                 for p in range(PAIRS_PER_GROUP)], axis=0)
            sink_fill[g, half] = jnp.where(kj == 0, sink_rows, -jnp.inf)
    first_key = lax.broadcasted_iota(jnp.int32, (2 * BLOCK, LANES), 0) == 0
    drop_value = (first_key & low_head, first_key & jnp.logical_not(low_head))

    def block(n, carry):
        qrows = pl.ds(pl.multiple_of(n * BLOCK, BLOCK), BLOCK)
        krows = pl.ds(pl.multiple_of(n * BLOCK, BLOCK), 2 * BLOCK)
        valid_prev = band_prev & (t * nb + n > 0)

        def scores(g, half):
            qst = jnp.concatenate(
                [q_ref[qrows, (g * PAIRS_PER_GROUP + p) * LANES:(g * PAIRS_PER_GROUP + p + 1) * LANES]
                 for p in range(PAIRS_PER_GROUP)], axis=0)
            return _dot_nt(qst, ks_ref[2 * g + half, krows, :])

        units = [(g, half) for g in range(N_KV_HEADS) for half in range(2)]
        s_next = scores(*units[0])
        o = []
        for idx, (g, half) in enumerate(units):
            s = s_next
            if idx + 1 < len(units):
                s_next = scores(*units[idx + 1])
            s_prev = jnp.where(valid_prev, s[:, :BLOCK], sink_fill[g, half])
            s_cur = jnp.where(band_cur, s[:, BLOCK:], -jnp.inf)
            m = jnp.max(jnp.maximum(s_prev, s_cur), axis=-1, keepdims=True)
            p_ = jnp.concatenate([jnp.exp2(s_prev - m), jnp.exp2(s_cur - m)], axis=1).astype(BF16)
            vwin = vs_ref[2 * g + half, krows, :]
            vwin = jnp.where(drop_value[half], jnp.zeros_like(vwin), vwin)
            o.append(_dot(p_, vwin))
            if half == 1:
                num = jnp.where(low_head, o[0], o[1])
                den = pltpu.roll(jnp.where(low_head, o[1], o[0]), HEAD_DIM, 1)
                outs = num / den
                o = []
                for p in range(PAIRS_PER_GROUP):
                    c = g * PAIRS_PER_GROUP + p
                    o_ref[qrows, c * LANES:(c + 1) * LANES] = outs[p * BLOCK:(p + 1) * BLOCK].astype(BF16)
        return carry

    lax.fori_loop(0, nb, block, 0, unroll=min(4, nb))


def _attn(q, k, v, sinks, *, batch, seq, tq):
    n = batch * seq
    nt = seq // tq
    nb = tq // BLOCK
    bps = seq // BLOCK
    cur = lambda b, t: (b * nt + t, 0)
    prev = lambda b, t: (b * bps + jnp.maximum(t * nb - 1, 0), 0)
    return pl.pallas_call(
        functools.partial(_attn_body, tq=tq),
        grid=(batch, nt),
        in_specs=[
            pl.BlockSpec(memory_space=pltpu.SMEM),
            pl.BlockSpec((tq, ATTN_WIDTH), cur),
            pl.BlockSpec((tq, KV_WIDTH), cur),
            pl.BlockSpec((BLOCK, KV_WIDTH), prev),
            pl.BlockSpec((tq, KV_WIDTH), cur),
            pl.BlockSpec((BLOCK, KV_WIDTH), prev),
        ],
        out_specs=pl.BlockSpec((tq, ATTN_WIDTH), cur),
        out_shape=jax.ShapeDtypeStruct((n, ATTN_WIDTH), BF16),
        scratch_shapes=[
            pltpu.VMEM((4, tq + BLOCK, LANES), BF16),
            pltpu.VMEM((4, tq + BLOCK, LANES), BF16),
        ],
        compiler_params=pltpu.CompilerParams(
            dimension_semantics=("parallel", "parallel"), vmem_limit_bytes=VMEM_LIMIT),
        name="attn",
    )(sinks, q, k, k, v, v)


def _hgrn_body(hq_ref, hk_ref, hl_ref, hi_ref, og_ref, gain_ref, o_ref, state_ref, *, tile):
    n_seq = hq_ref.shape[0]

    @pl.when(pl.program_id(0) == 0)
    def _():
        state_ref[...] = jnp.zeros_like(state_ref)

    ri = lax.broadcasted_iota(jnp.int32, (CHUNK, CHUNK), 0)
    ci = lax.broadcasted_iota(jnp.int32, (CHUNK, CHUNK), 1)
    causal = ri >= ci
    tri = jnp.where(causal, 1.0, 0.0).astype(BF16)
    gain = gain_ref[...]
    heads = [slice(h * HG_HEAD_DIM, (h + 1) * HG_HEAD_DIM) for h in range(HG_HEADS)]
    units = [(s, h) for s in range(n_seq) for h in range(HG_HEADS)]

    def chunk(c, carry):
        rows = pl.ds(pl.multiple_of(c * CHUNK, CHUNK), CHUNK)
        q_mid, k_mid, q_in, k_out, decay = [], [], [], [], []
        for s in range(n_seq):
            lf = hl_ref[s, rows, :]
            hi = lf.astype(BF16)
            lo = (lf - hi.astype(F32)).astype(BF16)
            b = _dot(tri, hi) + _dot(tri, lo)
            b_mid = b[CHUNK // 2:CHUNK // 2 + 1]
            b_last = b[CHUNK - 1:CHUNK]
            qm = hq_ref[s, rows, :].astype(F32) * jnp.exp2(b - b_mid)
            km = hk_ref[s, rows, :].astype(F32) * jnp.exp2(b_mid - b)
            q_mid.append(qm.astype(BF16))
            k_mid.append(km.astype(BF16))
            q_in.append((qm * jnp.exp2(b_mid)).astype(BF16))
            k_out.append((km * jnp.exp2(b_last - b_mid)).astype(BF16))
            decay.append(jnp.exp2(b_last))
        att = [_dot_nt(q_mid[s][:, heads[h]], k_mid[s][:, heads[h]]) for s, h in units]
        st = [state_ref[s, h] for s, h in units]
        o_inter = [_dot_nt(q_in[s][:, heads[h]], st[u].astype(BF16)) for u, (s, h) in enumerate(units)]
        upd = [_dot_tn(hi_ref[s, rows, heads[h]], k_out[s][:, heads[h]]) for s, h in units]
        o = [o_inter[u] + _dot(jnp.where(causal, att[u], 0.0).astype(BF16), hi_ref[s, rows, heads[h]])
             for u, (s, h) in enumerate(units)]
        for u, (s, h) in enumerate(units):
            state_ref[s, h] = st[u] * decay[s][:, heads[h]] + upd[u]
            y = _rms_rows(o[u], gain) * og_ref[s, rows, heads[h]].astype(F32)
            o_ref[s, rows, heads[h]] = y.astype(BF16)
        return carry

    lax.fori_loop(0, tile // CHUNK, chunk, 0, unroll=True)


def _hgrn(hq, hk, hl, hi, og, gain, *, batch, seq, tile):
    n = batch * seq
    cur = lambda t: (0, t, 0)
    ins = [a.reshape(batch, seq, HG_WIDTH) for a in (hq, hk, hl, hi, og)]
    out = pl.pallas_call(
        functools.partial(_hgrn_body, tile=tile),
        grid=(seq // tile,),
        in_specs=[pl.BlockSpec((batch, tile, HG_WIDTH), cur)] * 5 + [pl.BlockSpec((1, HG_HEAD_DIM), lambda t: (0, 0))],
        out_specs=pl.BlockSpec((batch, tile, HG_WIDTH), cur),
        out_shape=jax.ShapeDtypeStruct((batch, seq, HG_WIDTH), BF16),
        scratch_shapes=[pltpu.VMEM((batch, HG_HEADS, HG_HEAD_DIM, HG_HEAD_DIM), F32)],
        compiler_params=pltpu.CompilerParams(
            dimension_semantics=("arbitrary",), vmem_limit_bytes=VMEM_LIMIT),
        name="hgrn",
    )(*ins, gain)
    return out.reshape(n, HG_WIDTH)


def _merge_body(x_ref, ya_ref, yh_ref, ga_ref, gr_ref, wa_ref, wr_ref, wo_ref, *rest):
    n_casts = (len(rest) - 1) // 2
    cast_src, o_ref, cast_dst = rest[:n_casts], rest[n_casts], rest[n_casts + 1:]
    a = _dot(ya_ref[...], wa_ref[...])
    r = _dot(yh_ref[...], wr_ref[...])
    merged = (ga_ref[...].astype(F32) * a + gr_ref[...].astype(F32) * r).astype(BF16)
    o_ref[...] = x_ref[...] + _dot(merged, wo_ref[...])
    _run_casts(cast_src, cast_dst)


def _merge(x, ya, yh, ga, gr, w_a, w_r, w_o, casts=(), *, tm):
    n = x.shape[0]
    row = lambda i: (i, 0)
    whole = lambda i: (0, 0)
    cast_specs = _cast_specs(casts, n // tm)
    outs = pl.pallas_call(
        _merge_body,
        grid=(n // tm,),
        in_specs=[
            pl.BlockSpec((tm, D_MODEL), row),
            pl.BlockSpec((tm, ATTN_WIDTH), row),
            pl.BlockSpec((tm, HG_WIDTH), row),
            pl.BlockSpec((tm, D_MODEL), row),
            pl.BlockSpec((tm, D_MODEL), row),
            pl.BlockSpec((ATTN_WIDTH, D_MODEL), whole, pipeline_mode=pl.Buffered(1)),
            pl.BlockSpec((HG_WIDTH, D_MODEL), whole, pipeline_mode=pl.Buffered(1)),
            pl.BlockSpec((D_MODEL, D_MODEL), whole, pipeline_mode=pl.Buffered(1)),
        ] + cast_specs,
        out_specs=[pl.BlockSpec((tm, D_MODEL), row)] + cast_specs,
        out_shape=[jax.ShapeDtypeStruct((n, D_MODEL), F32)] + [jax.ShapeDtypeStruct(w.shape, BF16) for w in casts],
        compiler_params=pltpu.CompilerParams(
            dimension_semantics=("arbitrary",), vmem_limit_bytes=VMEM_LIMIT),
        name="merge",
    )(x, ya, yh, ga, gr, w_a, w_r, w_o, *casts)
    return outs[0], outs[1:]


def _tile(n, want):
    t = min(want, n)
    assert n % t == 0, (n, t)
    return t


def kernel(x, positions, lb_table, ffn1_norm, ffn1_w_gu, ffn1_w_down, mix_norm, w_in, q_norm, k_norm, sinks,
           hg_out_norm, w_attn_branch, w_hg_branch, w_out, ffn2_norm, ffn2_w_gu, ffn2_w_down):
    batch, seq = x.shape[0], x.shape[1]
    n = batch * seq
    depth = w_in.shape[0]
    assert seq % BLOCK == 0 and seq % CHUNK == 0

    half = HEAD_DIM // 2
    inv_freq = (ROPE_THETA ** (-jnp.arange(half, dtype=F32) * 2.0 / HEAD_DIM))[:, None]
    tm_proj = _tile(n, 512)
    pos_rows = positions.astype(F32).reshape(n // tm_proj, 1, tm_proj)

    xf = x.reshape(n, D_MODEL)
    for l in range(depth):
        xf = _ffn(xf, ffn1_norm[l][None, :], ffn1_w_gu[l].astype(BF16), ffn1_w_down[l].astype(BF16),
                  tm=_tile(n, 1024), tf=512)
        gain = mix_norm[l][None, :]
        w_in_l = w_in[l].astype(BF16)
        q, k, v, hq, hk, hl, hi, h_mix = _proj_a(
            xf, gain, w_in_l, pos_rows, inv_freq, jnp.tile(q_norm[l], 2)[None, :],
            jnp.tile(k_norm[l], 2)[None, :], lb_table, layer=l, tm=tm_proj)
        (og, ga, gr), (w_a, w_r, w_o) = _proj_b(
            h_mix, w_in_l, casts=(w_attn_branch[l], w_hg_branch[l], w_out[l]), tm=_tile(n, 1024))
        ya = _attn(q, k, v, sinks[l], batch=batch, seq=seq, tq=_tile(seq, 2048))
        yh = _hgrn(hq, hk, hl, hi, og, hg_out_norm[l][None, :], batch=batch, seq=seq, tile=_tile(seq, 512))
        xf, (w_gu2, w_down2) = _merge(
            xf, ya, yh, ga, gr, w_a, w_r, w_o, casts=(ffn2_w_gu[l], ffn2_w_down[l]), tm=_tile(n, 512))
        xf = _ffn(xf, ffn2_norm[l][None, :], w_gu2, w_down2, tm=_tile(n, 1024), tf=512)
    return xf.reshape(batch, seq, D_MODEL)
```

```python
import functools

import jax
import jax.numpy as jnp
from jax import lax
from jax.experimental import pallas as pl
from jax.experimental.pallas import tpu as pltpu

D_MODEL = 2048
HEAD_DIM = 64
N_Q_HEADS = 16
N_KV_HEADS = 2
GQA_GROUP = N_Q_HEADS // N_KV_HEADS
ATTN_WIDTH = N_Q_HEADS * HEAD_DIM
KV_WIDTH = N_KV_HEADS * HEAD_DIM
WINDOW = 128
BLOCK = 128
ROPE_THETA = 10000.0
HG_HEAD_DIM = 128
HG_HEADS = 8
HG_WIDTH = HG_HEADS * HG_HEAD_DIM
CHUNK = 64
D_FF = 5632
EPS = 1e-6

LANES = 128
PAIRS_PER_GROUP = GQA_GROUP // 2
VMEM_LIMIT = 58 * 1024 * 1024
LOG2E = 1.4426950408889634

N_ATTN = ATTN_WIDTH + 2 * KV_WIDTH
COLS_A = N_ATTN + 3 * HG_WIDTH
COLS_B = HG_WIDTH + 2 * D_MODEL

F32 = jnp.float32
BF16 = jnp.bfloat16


def _dot(a, b):
    return jnp.dot(a, b, preferred_element_type=F32)


def _dot_nt(a, b):
    return lax.dot_general(a, b, (((1,), (1,)), ((), ())), preferred_element_type=F32)


def _dot_tn(a, b):
    return lax.dot_general(a, b, (((0,), (0,)), ((), ())), preferred_element_type=F32)


def _sigmoid(z):
    return 0.5 * jnp.tanh(0.5 * z) + 0.5


def _silu(z):
    half = 0.5 * z
    return half * jnp.tanh(half) + half


def _rms_rows(x, gain):
    ms = jnp.mean(x * x, axis=-1, keepdims=True)
    return x * lax.rsqrt(ms + EPS) * gain


BF16_SUBLANES = 16


def _cast_specs(casts, steps, step_of=lambda i: i):
    specs = []
    for w in casts:
        rows = w.shape[0]
        rb = next(r for r in range(BF16_SUBLANES, rows + 1, BF16_SUBLANES) if rows % r == 0 and rows // r <= steps)
        specs.append(pl.BlockSpec(
            (rb, w.shape[1]),
            functools.partial(lambda *idx, last: (jnp.minimum(step_of(*idx), last), 0), last=rows // rb - 1)))
    return specs


def _run_casts(src_refs, dst_refs):
    for src_ref, dst_ref in zip(src_refs, dst_refs):
        dst_ref[...] = src_ref[...].astype(BF16)


FFN_NORM_ROWS = 128
FFN_NORM_FIRST = 2


def _ffn_body(x_hbm, g_ref, wg_ref, wu_ref, wd_ref, *rest, tm, tf):
    n_casts = (len(rest) - 5) // 2
    cast_src, o_ref, cast_dst = rest[:n_casts], rest[n_casts], rest[n_casts + 1:2 * n_casts + 1]
    h0_ref, h1_ref, xs_ref, sem = rest[2 * n_casts + 1:]
    i = pl.program_id(0)
    j = pl.program_id(1)
    n_tiles = pl.num_programs(0)
    hbufs = (h0_ref, h1_ref)

    def x_copy(tile):
        rows = pl.ds(pl.multiple_of(tile * tm, tm), tm)
        return pltpu.make_async_copy(x_hbm.at[rows], xs_ref.at[:tm], sem)

    @pl.when((i == 0) & (j == 0))
    def _():
        xs_ref[tm:, :] = jnp.zeros((FFN_NORM_ROWS, D_MODEL), F32)
        x_copy(0).start()
        x_copy(0).wait()
        h0_ref[:tm, :] = _rms_rows(xs_ref[:tm, :], g_ref[...]).astype(BF16)

    @pl.when((j == FFN_NORM_FIRST) & (i + 1 < n_tiles))
    def _():
        x_copy(i + 1).wait()

    def chunk_step(parity, first):
        h = hbufs[parity][:tm, :]
        halves = [slice(c * tf // 2, (c + 1) * tf // 2) for c in range(2)]
        gate_up = [(_dot(h, wg_ref[:, sl]), _dot(h, wu_ref[:, sl])) for sl in halves]
        act = [(_silu(g) * (u * 0.5)).astype(BF16) for g, u in gate_up]
        down = _dot(act[0], wd_ref[halves[0], :]) + _dot(act[1], wd_ref[halves[1], :])
        if first:
            o_ref[...] = xs_ref[:tm, :] + down

            @pl.when(i + 1 < n_tiles)
            def _():
                x_copy(i + 1).start()
        else:
            o_ref[...] += down
            part = j - FFN_NORM_FIRST
            live = (part < tm // FFN_NORM_ROWS) & (part >= 0) & (i + 1 < n_tiles)
            rows = pl.ds(pl.multiple_of(jnp.where(live, part * FFN_NORM_ROWS, tm), FFN_NORM_ROWS), FFN_NORM_ROWS)
            hbufs[1 - parity][rows, :] = _rms_rows(xs_ref[rows, :], g_ref[...]).astype(BF16)
        _run_casts(cast_src, cast_dst)

    for parity in range(2):
        for first in (True, False):
            pl.when((i % 2 == parity) & ((j == 0) == first))(functools.partial(chunk_step, parity, first))


def _ffn(x, gain, w_gu, w_down, casts=(), *, tm, tf):
    n = x.shape[0]
    nf = D_FF // tf
    assert tm % FFN_NORM_ROWS == 0 and nf >= FFN_NORM_FIRST + tm // FFN_NORM_ROWS
    cast_specs = _cast_specs(casts, (n // tm) * nf, lambda i, j: i * nf + j)
    outs = pl.pallas_call(
        functools.partial(_ffn_body, tm=tm, tf=tf),
        grid=(n // tm, nf),
        in_specs=[
            pl.BlockSpec(memory_space=pl.ANY),
            pl.BlockSpec((1, D_MODEL), lambda i, j: (0, 0)),
            pl.BlockSpec((D_MODEL, tf), lambda i, j: (0, j)),
            pl.BlockSpec((D_MODEL, tf), lambda i, j: (0, j + nf)),
            pl.BlockSpec((tf, D_MODEL), lambda i, j: (j, 0)),
        ] + cast_specs,
        out_specs=[pl.BlockSpec((tm, D_MODEL), lambda i, j: (i, 0))] + cast_specs,
        out_shape=[jax.ShapeDtypeStruct((n, D_MODEL), F32)] + [jax.ShapeDtypeStruct(w.shape, BF16) for w in casts],
        scratch_shapes=[pltpu.VMEM((tm + FFN_NORM_ROWS, D_MODEL), BF16), pltpu.VMEM((tm + FFN_NORM_ROWS, D_MODEL), BF16),
                        pltpu.VMEM((tm + FFN_NORM_ROWS, D_MODEL), F32), pltpu.SemaphoreType.DMA(())],
        compiler_params=pltpu.CompilerParams(
            dimension_semantics=("arbitrary", "arbitrary"), vmem_limit_bytes=VMEM_LIMIT),
        name="ffn",
    )(x, gain, w_gu, w_gu, w_down, *casts)
    return outs[0], outs[1:]


def _proj_a_body(x_ref, g_ref, w_ref, pos_ref, invf_ref, qg_ref, kg_ref, lbt_ref,
                 q_ref, k_ref, v_ref, hq_ref, hk_ref, hl_ref, hi_ref, h_ref, *, layer):
    h = _rms_rows(x_ref[...], g_ref[...]).astype(BF16)
    h_ref[...] = h

    def cols(start, width):
        return _dot(h, w_ref[:, start:start + width])

    lane = lax.broadcasted_iota(jnp.int32, (1, LANES), 1)
    first_half = (lane % HEAD_DIM) < (HEAD_DIM // 2)
    gi = lax.broadcasted_iota(jnp.int32, (LANES, LANES), 0) // HEAD_DIM
    gj = lax.broadcasted_iota(jnp.int32, (LANES, LANES), 1) // HEAD_DIM
    gsum = jnp.where(gi == gj, 1.0, 0.0).astype(BF16)

    def head_norm(v, gain):
        sq = v * v
        hi = sq.astype(BF16)
        lo = (sq - hi.astype(F32)).astype(BF16)
        ss = _dot(hi, gsum) + _dot(lo, gsum)
        return v * lax.rsqrt(ss * (1.0 / HEAD_DIM) + EPS) * gain

    ang = invf_ref[...] * pos_ref[...]
    cos_t = jnp.cos(ang)
    sin_t = jnp.sin(ang)
    reps = LANES // HEAD_DIM
    cosf = jnp.concatenate([cos_t, cos_t] * reps, axis=0).T
    sin_signed = jnp.concatenate([-sin_t, sin_t] * reps, axis=0).T

    def rope(v):
        rot = jnp.where(first_half, pltpu.roll(v, LANES - HEAD_DIM // 2, 1), pltpu.roll(v, HEAD_DIM // 2, 1))
        return v * cosf + rot * sin_signed

    lbt = lbt_ref[...]
    e = jnp.exp(lbt - jnp.max(lbt, axis=0, keepdims=True))
    sm = e / jnp.sum(e, axis=0, keepdims=True)
    lb = jnp.sum(sm[:layer + 1], axis=0, keepdims=True)

    zq = cols(0, ATTN_WIDTH)
    zkv = cols(ATTN_WIDTH, 2 * KV_WIDTH)

    def attn_piece(c):
        if c < ATTN_WIDTH // LANES:
            sl = slice(c * LANES, (c + 1) * LANES)
            q_ref[:, sl] = (rope(head_norm(zq[:, sl], qg_ref[...])) * (HEAD_DIM ** -0.5 * LOG2E)).astype(BF16)
        else:
            k_ref[...] = rope(head_norm(zkv[:, :KV_WIDTH], kg_ref[...])).astype(BF16)
            v_ref[...] = zkv[:, KV_WIDTH:].astype(BF16)

    def hgrn_piece(kind, sl):
        z = cols(N_ATTN + kind * HG_WIDTH + sl.start, sl.stop - sl.start)
        if kind == 0:
            hq_ref[:, sl] = _silu(z).astype(BF16)
        elif kind == 1:
            f = lb[:, sl] + (1.0 - lb[:, sl]) * _sigmoid(z)
            hk_ref[:, sl] = (1.0 - f).astype(BF16)
            hl_ref[:, sl] = jnp.log2(f)
        else:
            hi_ref[:, sl] = z.astype(BF16)

    chunk = 2 * LANES
    pieces = [(kind, slice(c, c + chunk)) for kind in range(3) for c in range(0, HG_WIDTH, chunk)]
    n_attn_pieces = ATTN_WIDTH // LANES + 1
    for idx, (kind, sl) in enumerate(pieces):
        hgrn_piece(kind, sl)
        if idx < n_attn_pieces:
            attn_piece(idx)


def _proj_a(x, gain, w_a, pos_rows, inv_freq, q_gain, k_gain, lb_table, *, layer, tm):
    n = x.shape[0]
    row = lambda i: (i, 0)
    whole = lambda i: (0, 0)
    widths = (ATTN_WIDTH, KV_WIDTH, KV_WIDTH, HG_WIDTH, HG_WIDTH, HG_WIDTH, HG_WIDTH, D_MODEL)
    dtypes = (BF16, BF16, BF16, BF16, BF16, F32, BF16, BF16)
    return pl.pallas_call(
        functools.partial(_proj_a_body, layer=layer),
        grid=(n // tm,),
        in_specs=[
            pl.BlockSpec((tm, D_MODEL), row),
            pl.BlockSpec((1, D_MODEL), whole),
            pl.BlockSpec((D_MODEL, COLS_A), whole, pipeline_mode=pl.Buffered(1)),
            pl.BlockSpec((None, 1, tm), lambda i: (i, 0, 0)),
            pl.BlockSpec((HEAD_DIM // 2, 1), whole),
            pl.BlockSpec((1, LANES), whole),
            pl.BlockSpec((1, LANES), whole),
            pl.BlockSpec(lb_table.shape, whole),
        ],
        out_specs=[pl.BlockSpec((tm, w), row) for w in widths],
        out_shape=[jax.ShapeDtypeStruct((n, w), d) for w, d in zip(widths, dtypes)],
        compiler_params=pltpu.CompilerParams(
            dimension_semantics=("parallel",), vmem_limit_bytes=VMEM_LIMIT),
        name="proj_a",
    )(x, gain, w_a, pos_rows, inv_freq, q_gain, k_gain, lb_table)


def _proj_b_body(h_ref, w_hbm, *rest):
    n_casts = (len(rest) - 5) // 2
    cast_src, (og_ref, ga_ref, gr_ref) = rest[:n_casts], rest[n_casts:n_casts + 3]
    cast_dst, (w_ref, sem) = rest[n_casts + 3:2 * n_casts + 3], rest[2 * n_casts + 3:]

    @pl.when(pl.program_id(0) == 0)
    def _():
        copy = pltpu.make_async_copy(w_hbm.at[:, COLS_A:], w_ref, sem)
        copy.start()
        copy.wait()

    def cols(start, width):
        return _dot(h_ref[...], w_ref[:, start:start + width])

    og_ref[...] = _silu(cols(0, HG_WIDTH)).astype(BF16)
    for c in range(D_MODEL // HG_WIDTH):
        sl = slice(c * HG_WIDTH, (c + 1) * HG_WIDTH)
        ga_ref[:, sl] = _sigmoid(cols(HG_WIDTH + c * HG_WIDTH, HG_WIDTH)).astype(BF16)
        gr_ref[:, sl] = _sigmoid(cols(HG_WIDTH + D_MODEL + c * HG_WIDTH, HG_WIDTH)).astype(BF16)
    _run_casts(cast_src, cast_dst)


def _proj_b(h, w_in, casts=(), *, tm):
    n = h.shape[0]
    row = lambda i: (i, 0)
    widths = (HG_WIDTH, D_MODEL, D_MODEL)
    cast_specs = _cast_specs(casts, n // tm)
    outs = pl.pallas_call(
        _proj_b_body,
        grid=(n // tm,),
        in_specs=[
            pl.BlockSpec((tm, D_MODEL), row),
            pl.BlockSpec(memory_space=pl.ANY),
        ] + cast_specs,
        out_specs=[pl.BlockSpec((tm, w), row) for w in widths] + cast_specs,
        out_shape=([jax.ShapeDtypeStruct((n, w), BF16) for w in widths]
                   + [jax.ShapeDtypeStruct(w.shape, BF16) for w in casts]),
        scratch_shapes=[pltpu.VMEM((D_MODEL, COLS_B), BF16), pltpu.SemaphoreType.DMA(())],
        compiler_params=pltpu.CompilerParams(
            dimension_semantics=("arbitrary",), vmem_limit_bytes=VMEM_LIMIT),
        name="proj_b",
    )(h, w_in, *casts)
    return outs[:3], outs[3:]


def _attn_body(sinks_ref, q_ref, k_ref, kp_ref, v_ref, vp_ref, o_ref, ks_ref, vs_ref, *, tq):
    t = pl.program_id(1)
    nb = tq // BLOCK
    lane = lax.broadcasted_iota(jnp.int32, (1, LANES), 1)
    low_head = lane < HEAD_DIM

    def put_kv(dst_ref, rows, v, pad):
        swapped = pltpu.roll(v, HEAD_DIM, 1)
        dst_ref[0, rows, :] = jnp.where(low_head, v, pad).astype(BF16)
        dst_ref[1, rows, :] = jnp.where(low_head, pad, swapped).astype(BF16)
        dst_ref[2, rows, :] = jnp.where(low_head, swapped, pad).astype(BF16)
        dst_ref[3, rows, :] = jnp.where(low_head, pad, v).astype(BF16)

    put_kv(ks_ref, slice(0, BLOCK), kp_ref[...].astype(F32), 0.0)
    put_kv(ks_ref, slice(BLOCK, BLOCK + tq), k_ref[...].astype(F32), 0.0)
    put_kv(vs_ref, slice(0, BLOCK), vp_ref[...].astype(F32), 1.0)
    put_kv(vs_ref, slice(BLOCK, BLOCK + tq), v_ref[...].astype(F32), 1.0)

    rows_q = PAIRS_PER_GROUP * BLOCK
    qi = lax.broadcasted_iota(jnp.int32, (rows_q, BLOCK), 0) % BLOCK
    kj = lax.broadcasted_iota(jnp.int32, (rows_q, BLOCK), 1)
    band_prev = kj > qi + BLOCK - WINDOW
    band_cur = kj <= qi
    sink_fill = {}
    for g in range(N_KV_HEADS):
        for half in range(2):
            sink_rows = jnp.concatenate(
                [jnp.full((BLOCK, LANES), sinks_ref[g * GQA_GROUP + 2 * p + half] * LOG2E, F32)
                 for p in range(PAIRS_PER_GROUP)], axis=0)
            sink_fill[g, half] = jnp.where(kj == 0, sink_rows, -jnp.inf)
    first_key = lax.broadcasted_iota(jnp.int32, (2 * BLOCK, LANES), 0) == 0
    drop_value = (first_key & low_head, first_key & jnp.logical_not(low_head))

    def block(n, carry):
        qrows = pl.ds(pl.multiple_of(n * BLOCK, BLOCK), BLOCK)
        krows = pl.ds(pl.multiple_of(n * BLOCK, BLOCK), 2 * BLOCK)
        valid_prev = band_prev & (t * nb + n > 0)

        def scores(g, half):
            qst = jnp.concatenate(
                [q_ref[qrows, (g * PAIRS_PER_GROUP + p) * LANES:(g * PAIRS_PER_GROUP + p + 1) * LANES]
                 for p in range(PAIRS_PER_GROUP)], axis=0)
            return _dot_nt(qst, ks_ref[2 * g + half, krows, :])

        units = [(g, half) for g in range(N_KV_HEADS) for half in range(2)]
        s_next = scores(*units[0])
        o = []
        for idx, (g, half) in enumerate(units):
            s = s_next
            if idx + 1 < len(units):
                s_next = scores(*units[idx + 1])
            s_prev = jnp.where(valid_prev, s[:, :BLOCK], sink_fill[g, half])
            s_cur = jnp.where(band_cur, s[:, BLOCK:], -jnp.inf)
            m = jnp.max(jnp.maximum(s_prev, s_cur), axis=-1, keepdims=True)
            p_ = jnp.concatenate([jnp.exp2(s_prev - m), jnp.exp2(s_cur - m)], axis=1).astype(BF16)
            vwin = vs_ref[2 * g + half, krows, :]
            vwin = jnp.where(drop_value[half], jnp.zeros_like(vwin), vwin)
            o.append(_dot(p_, vwin))
            if half == 1:
                num = jnp.where(low_head, o[0], o[1])
                den = pltpu.roll(jnp.where(low_head, o[1], o[0]), HEAD_DIM, 1)
                outs = num / den
                o = []
                for p in range(PAIRS_PER_GROUP):
                    c = g * PAIRS_PER_GROUP + p
                    o_ref[qrows, c * LANES:(c + 1) * LANES] = outs[p * BLOCK:(p + 1) * BLOCK].astype(BF16)
        return carry

    lax.fori_loop(0, nb, block, 0, unroll=min(4, nb))


def _attn(q, k, v, sinks, *, batch, seq, tq):
    n = batch * seq
    nt = seq // tq
    nb = tq // BLOCK
    bps = seq // BLOCK
    cur = lambda b, t: (b * nt + t, 0)
    prev = lambda b, t: (b * bps + jnp.maximum(t * nb - 1, 0), 0)
    return pl.pallas_call(
        functools.partial(_attn_body, tq=tq),
        grid=(batch, nt),
        in_specs=[
            pl.BlockSpec(memory_space=pltpu.SMEM),
            pl.BlockSpec((tq, ATTN_WIDTH), cur),
            pl.BlockSpec((tq, KV_WIDTH), cur),
            pl.BlockSpec((BLOCK, KV_WIDTH), prev),
            pl.BlockSpec((tq, KV_WIDTH), cur),
            pl.BlockSpec((BLOCK, KV_WIDTH), prev),
        ],
        out_specs=pl.BlockSpec((tq, ATTN_WIDTH), cur),
        out_shape=jax.ShapeDtypeStruct((n, ATTN_WIDTH), BF16),
        scratch_shapes=[
            pltpu.VMEM((4, tq + BLOCK, LANES), BF16),
            pltpu.VMEM((4, tq + BLOCK, LANES), BF16),
        ],
        compiler_params=pltpu.CompilerParams(
            dimension_semantics=("parallel", "parallel"), vmem_limit_bytes=VMEM_LIMIT),
        name="attn",
    )(sinks, q, k, k, v, v)


def _hgrn_body(hq_ref, hk_ref, hl_ref, hi_ref, og_ref, gain_ref, o_ref, state_ref, *, tile):
    n_seq = hq_ref.shape[0]

    @pl.when(pl.program_id(0) == 0)
    def _():
        state_ref[...] = jnp.zeros_like(state_ref)

    ri = lax.broadcasted_iota(jnp.int32, (CHUNK, CHUNK), 0)
    ci = lax.broadcasted_iota(jnp.int32, (CHUNK, CHUNK), 1)
    causal = ri >= ci
    tri = jnp.where(causal, 1.0, 0.0).astype(BF16)
    gain = gain_ref[...]
    heads = [slice(h * HG_HEAD_DIM, (h + 1) * HG_HEAD_DIM) for h in range(HG_HEADS)]
    units = [(s, h) for s in range(n_seq) for h in range(HG_HEADS)]

    def chunk(c, carry):
        rows = pl.ds(pl.multiple_of(c * CHUNK, CHUNK), CHUNK)
        q_mid, k_mid, q_in, k_out, decay = [], [], [], [], []
        for s in range(n_seq):
            lf = hl_ref[s, rows, :]
            hi = lf.astype(BF16)
            lo = (lf - hi.astype(F32)).astype(BF16)
            b = _dot(tri, hi) + _dot(tri, lo)
            b_mid = b[CHUNK // 2:CHUNK // 2 + 1]
            b_last = b[CHUNK - 1:CHUNK]
            qm = hq_ref[s, rows, :].astype(F32) * jnp.exp2(b - b_mid)
            km = hk_ref[s, rows, :].astype(F32) * jnp.exp2(b_mid - b)
            q_mid.append(qm.astype(BF16))
            k_mid.append(km.astype(BF16))
            q_in.append((qm * jnp.exp2(b_mid)).astype(BF16))
            k_out.append((km * jnp.exp2(b_last - b_mid)).astype(BF16))
            decay.append(jnp.exp2(b_last))
        att = [_dot_nt(q_mid[s][:, heads[h]], k_mid[s][:, heads[h]]) for s, h in units]
        st = [state_ref[s, h] for s, h in units]
        o_inter = [_dot_nt(q_in[s][:, heads[h]], st[u].astype(BF16)) for u, (s, h) in enumerate(units)]
        upd = [_dot_tn(hi_ref[s, rows, heads[h]], k_out[s][:, heads[h]]) for s, h in units]
        o = [o_inter[u] + _dot(jnp.where(causal, att[u], 0.0).astype(BF16), hi_ref[s, rows, heads[h]])
             for u, (s, h) in enumerate(units)]
        for u, (s, h) in enumerate(units):
            state_ref[s, h] = st[u] * decay[s][:, heads[h]] + upd[u]
            y = _rms_rows(o[u], gain) * og_ref[s, rows, heads[h]].astype(F32)
            o_ref[s, rows, heads[h]] = y.astype(BF16)
        return carry

    lax.fori_loop(0, tile // CHUNK, chunk, 0, unroll=True)


def _hgrn(hq, hk, hl, hi, og, gain, *, batch, seq, tile):
    n = batch * seq
    cur = lambda t: (0, t, 0)
    ins = [a.reshape(batch, seq, HG_WIDTH) for a in (hq, hk, hl, hi, og)]
    out = pl.pallas_call(
        functools.partial(_hgrn_body, tile=tile),
        grid=(seq // tile,),
        in_specs=[pl.BlockSpec((batch, tile, HG_WIDTH), cur)] * 5 + [pl.BlockSpec((1, HG_HEAD_DIM), lambda t: (0, 0))],
        out_specs=pl.BlockSpec((batch, tile, HG_WIDTH), cur),
        out_shape=jax.ShapeDtypeStruct((batch, seq, HG_WIDTH), BF16),
        scratch_shapes=[pltpu.VMEM((batch, HG_HEADS, HG_HEAD_DIM, HG_HEAD_DIM), F32)],
        compiler_params=pltpu.CompilerParams(
            dimension_semantics=("arbitrary",), vmem_limit_bytes=VMEM_LIMIT),
        name="hgrn",
    )(*ins, gain)
    return out.reshape(n, HG_WIDTH)


def _merge_body(x_ref, ya_ref, yh_ref, ga_ref, gr_ref, wa_ref, wr_ref, wo_ref, *rest):
    n_casts = (len(rest) - 1) // 2
    cast_src, o_ref, cast_dst = rest[:n_casts], rest[n_casts], rest[n_casts + 1:]
    a = _dot(ya_ref[...], wa_ref[...])
    r = _dot(yh_ref[...], wr_ref[...])
    merged = (ga_ref[...].astype(F32) * a + gr_ref[...].astype(F32) * r).astype(BF16)
    o_ref[...] = x_ref[...] + _dot(merged, wo_ref[...])
    _run_casts(cast_src, cast_dst)


def _merge(x, ya, yh, ga, gr, w_a, w_r, w_o, casts=(), *, tm):
    n = x.shape[0]
    row = lambda i: (i, 0)
    whole = lambda i: (0, 0)
    cast_specs = _cast_specs(casts, n // tm)
    outs = pl.pallas_call(
        _merge_body,
        grid=(n // tm,),
        in_specs=[
            pl.BlockSpec((tm, D_MODEL), row),
            pl.BlockSpec((tm, ATTN_WIDTH), row),
            pl.BlockSpec((tm, HG_WIDTH), row),
            pl.BlockSpec((tm, D_MODEL), row),
            pl.BlockSpec((tm, D_MODEL), row),
            pl.BlockSpec((ATTN_WIDTH, D_MODEL), whole, pipeline_mode=pl.Buffered(1)),
            pl.BlockSpec((HG_WIDTH, D_MODEL), whole, pipeline_mode=pl.Buffered(1)),
            pl.BlockSpec((D_MODEL, D_MODEL), whole, pipeline_mode=pl.Buffered(1)),
        ] + cast_specs,
        out_specs=[pl.BlockSpec((tm, D_MODEL), row)] + cast_specs,
        out_shape=[jax.ShapeDtypeStruct((n, D_MODEL), F32)] + [jax.ShapeDtypeStruct(w.shape, BF16) for w in casts],
        compiler_params=pltpu.CompilerParams(
            dimension_semantics=("arbitrary",), vmem_limit_bytes=VMEM_LIMIT),
        name="merge",
    )(x, ya, yh, ga, gr, w_a, w_r, w_o, *casts)
    return outs[0], outs[1:]


def _tile(n, want):
    t = min(want, n)
    assert n % t == 0, (n, t)
    return t


def kernel(x, positions, lb_table, ffn1_norm, ffn1_w_gu, ffn1_w_down, mix_norm, w_in, q_norm, k_norm, sinks,
           hg_out_norm, w_attn_branch, w_hg_branch, w_out, ffn2_norm, ffn2_w_gu, ffn2_w_down):
    batch, seq = x.shape[0], x.shape[1]
    n = batch * seq
    depth = w_in.shape[0]
    assert seq % BLOCK == 0 and seq % CHUNK == 0

    half = HEAD_DIM // 2
    inv_freq = (ROPE_THETA ** (-jnp.arange(half, dtype=F32) * 2.0 / HEAD_DIM))[:, None]
    tm_proj = _tile(n, 512)
    pos_rows = positions.astype(F32).reshape(n // tm_proj, 1, tm_proj)

    xf = x.reshape(n, D_MODEL)
    for l in range(depth):
        xf, (w_in_l,) = _ffn(xf, ffn1_norm[l][None, :], ffn1_w_gu[l].astype(BF16), ffn1_w_down[l].astype(BF16),
                             casts=(w_in[l],), tm=_tile(n, 1024), tf=512)
        gain = mix_norm[l][None, :]
        q, k, v, hq, hk, hl, hi, h_mix = _proj_a(
            xf, gain, w_in_l, pos_rows, inv_freq, jnp.tile(q_norm[l], 2)[None, :],
            jnp.tile(k_norm[l], 2)[None, :], lb_table, layer=l, tm=tm_proj)
        (og, ga, gr), (w_a, w_r, w_o) = _proj_b(
            h_mix, w_in_l, casts=(w_attn_branch[l], w_hg_branch[l], w_out[l]), tm=_tile(n, 1024))
        ya = _attn(q, k, v, sinks[l], batch=batch, seq=seq, tq=_tile(seq, 2048))
        yh = _hgrn(hq, hk, hl, hi, og, hg_out_norm[l][None, :], batch=batch, seq=seq, tile=_tile(seq, 512))
        xf, (w_gu2, w_down2) = _merge(
            xf, ya, yh, ga, gr, w_a, w_r, w_o, casts=(ffn2_w_gu[l], ffn2_w_down[l]), tm=_tile(n, 512))
        xf, _ = _ffn(xf, ffn2_norm[l][None, :], w_gu2, w_down2, tm=_tile(n, 1024), tf=512)
    return xf.reshape(batch, seq, D_MODEL)
```

```python
import functools

import jax
import jax.numpy as jnp
from jax import lax
from jax.experimental import pallas as pl
from jax.experimental.pallas import tpu as pltpu

D_MODEL = 2048
HEAD_DIM = 64
N_Q_HEADS = 16
N_KV_HEADS = 2
GQA_GROUP = N_Q_HEADS // N_KV_HEADS
ATTN_WIDTH = N_Q_HEADS * HEAD_DIM
KV_WIDTH = N_KV_HEADS * HEAD_DIM
WINDOW = 128
BLOCK = 128
ROPE_THETA = 10000.0
HG_HEAD_DIM = 128
HG_HEADS = 8
HG_WIDTH = HG_HEADS * HG_HEAD_DIM
CHUNK = 64
D_FF = 5632
EPS = 1e-6

LANES = 128
PAIRS_PER_GROUP = GQA_GROUP // 2
VMEM_LIMIT = 58 * 1024 * 1024
LOG2E = 1.4426950408889634

FFN_TM, FFN_TF = 1024, 512
PROJ_A_TM, PROJ_B_TM, MERGE_TM = 512, 1024, 512
ATTN_TQ, HGRN_TILE = 2048, 512

N_ATTN = ATTN_WIDTH + 2 * KV_WIDTH
COLS_A = N_ATTN + 3 * HG_WIDTH
COLS_B = HG_WIDTH + 2 * D_MODEL

F32 = jnp.float32
BF16 = jnp.bfloat16


def _dot(a, b):
    return jnp.dot(a, b, preferred_element_type=F32)


def _dot_nt(a, b):
    return lax.dot_general(a, b, (((1,), (1,)), ((), ())), preferred_element_type=F32)


def _dot_tn(a, b):
    return lax.dot_general(a, b, (((0,), (0,)), ((), ())), preferred_element_type=F32)


def _sigmoid(z):
    return 0.5 * jnp.tanh(0.5 * z) + 0.5


def _silu(z):
    half = 0.5 * z
    return half * jnp.tanh(half) + half


def _rms_rows(x, gain):
    ms = jnp.mean(x * x, axis=-1, keepdims=True)
    return x * lax.rsqrt(ms + EPS) * gain


BF16_SUBLANES = 16


def _cast_specs(casts, steps, step_of=lambda i: i):
    specs = []
    for w in casts:
        rows = w.shape[0]
        rb = next(r for r in range(BF16_SUBLANES, rows + 1, BF16_SUBLANES) if rows % r == 0 and rows // r <= steps)
        specs.append(pl.BlockSpec(
            (rb, w.shape[1]),
            functools.partial(lambda *idx, last: (jnp.minimum(step_of(*idx), last), 0), last=rows // rb - 1)))
    return specs


def _run_casts(src_refs, dst_refs):
    for src_ref, dst_ref in zip(src_refs, dst_refs):
        dst_ref[...] = src_ref[...].astype(BF16)


FFN_NORM_ROWS = 128
FFN_NORM_FIRST = 2


def _ffn_body(x_hbm, g_ref, wg_ref, wu_ref, wd_ref, *rest, tm, tf):
    n_casts = (len(rest) - 5) // 2
    cast_src, o_ref, cast_dst = rest[:n_casts], rest[n_casts], rest[n_casts + 1:2 * n_casts + 1]
    h0_ref, h1_ref, xs_ref, sem = rest[2 * n_casts + 1:]
    i = pl.program_id(0)
    j = pl.program_id(1)
    n_tiles = pl.num_programs(0)
    hbufs = (h0_ref, h1_ref)

    def x_copy(tile):
        rows = pl.ds(pl.multiple_of(tile * tm, tm), tm)
        return pltpu.make_async_copy(x_hbm.at[rows], xs_ref.at[:tm], sem)

    @pl.when((i == 0) & (j == 0))
    def _():
        xs_ref[tm:, :] = jnp.zeros((FFN_NORM_ROWS, D_MODEL), F32)
        x_copy(0).start()
        x_copy(0).wait()
        h0_ref[:tm, :] = _rms_rows(xs_ref[:tm, :], g_ref[...]).astype(BF16)

    @pl.when((j == FFN_NORM_FIRST) & (i + 1 < n_tiles))
    def _():
        x_copy(i + 1).wait()

    def chunk_step(parity, first):
        h = hbufs[parity][:tm, :]
        halves = [slice(c * tf // 2, (c + 1) * tf // 2) for c in range(2)]
        gate_up = [(_dot(h, wg_ref[:, sl]), _dot(h, wu_ref[:, sl])) for sl in halves]
        act = [(_silu(g) * (u * 0.5)).astype(BF16) for g, u in gate_up]
        down = _dot(act[0], wd_ref[halves[0], :]) + _dot(act[1], wd_ref[halves[1], :])
        if first:
            o_ref[...] = xs_ref[:tm, :] + down

            @pl.when(i + 1 < n_tiles)
            def _():
                x_copy(i + 1).start()
        else:
            o_ref[...] += down
            part = j - FFN_NORM_FIRST
            live = (part < tm // FFN_NORM_ROWS) & (part >= 0) & (i + 1 < n_tiles)
            rows = pl.ds(pl.multiple_of(jnp.where(live, part * FFN_NORM_ROWS, tm), FFN_NORM_ROWS), FFN_NORM_ROWS)
            hbufs[1 - parity][rows, :] = _rms_rows(xs_ref[rows, :], g_ref[...]).astype(BF16)
        _run_casts(cast_src, cast_dst)

    for parity in range(2):
        for first in (True, False):
            pl.when((i % 2 == parity) & ((j == 0) == first))(functools.partial(chunk_step, parity, first))


def _ffn(x, gain, w_gu, w_down, casts=(), *, tm, tf):
    n = x.shape[0]
    nf = D_FF // tf
    assert tm % FFN_NORM_ROWS == 0 and nf >= FFN_NORM_FIRST + tm // FFN_NORM_ROWS
    cast_specs = _cast_specs(casts, (n // tm) * nf, lambda i, j: i * nf + j)
    outs = pl.pallas_call(
        functools.partial(_ffn_body, tm=tm, tf=tf),
        grid=(n // tm, nf),
        in_specs=[
            pl.BlockSpec(memory_space=pl.ANY),
            pl.BlockSpec((1, D_MODEL), lambda i, j: (0, 0)),
            pl.BlockSpec((D_MODEL, tf), lambda i, j: (0, j)),
            pl.BlockSpec((D_MODEL, tf), lambda i, j: (0, j + nf)),
            pl.BlockSpec((tf, D_MODEL), lambda i, j: (j, 0)),
        ] + cast_specs,
        out_specs=[pl.BlockSpec((tm, D_MODEL), lambda i, j: (i, 0))] + cast_specs,
        out_shape=[jax.ShapeDtypeStruct((n, D_MODEL), F32)] + [jax.ShapeDtypeStruct(w.shape, BF16) for w in casts],
        scratch_shapes=[pltpu.VMEM((tm + FFN_NORM_ROWS, D_MODEL), BF16), pltpu.VMEM((tm + FFN_NORM_ROWS, D_MODEL), BF16),
                        pltpu.VMEM((tm + FFN_NORM_ROWS, D_MODEL), F32), pltpu.SemaphoreType.DMA(())],
        compiler_params=pltpu.CompilerParams(
            dimension_semantics=("arbitrary", "arbitrary"), vmem_limit_bytes=VMEM_LIMIT),
        name="ffn",
    )(x, gain, w_gu, w_gu, w_down, *casts)
    return outs[0], outs[1:]


def _proj_a_body(x_ref, g_ref, w_ref, pos_ref, invf_ref, qg_ref, kg_ref, lbt_ref,
                 q_ref, k_ref, v_ref, hq_ref, hk_ref, hl_ref, hi_ref, h_ref, *, layer):
    h = _rms_rows(x_ref[...], g_ref[...]).astype(BF16)
    h_ref[...] = h

    def cols(start, width):
        return _dot(h, w_ref[:, start:start + width])

    lane = lax.broadcasted_iota(jnp.int32, (1, LANES), 1)
    first_half = (lane % HEAD_DIM) < (HEAD_DIM // 2)
    gi = lax.broadcasted_iota(jnp.int32, (LANES, LANES), 0) // HEAD_DIM
    gj = lax.broadcasted_iota(jnp.int32, (LANES, LANES), 1) // HEAD_DIM
    gsum = jnp.where(gi == gj, 1.0, 0.0).astype(BF16)

    def head_norm(v, gain):
        sq = v * v
        hi = sq.astype(BF16)
        lo = (sq - hi.astype(F32)).astype(BF16)
        ss = _dot(hi, gsum) + _dot(lo, gsum)
        return v * lax.rsqrt(ss * (1.0 / HEAD_DIM) + EPS) * gain

    ang = invf_ref[...] * pos_ref[...]
    cos_t = jnp.cos(ang)
    sin_t = jnp.sin(ang)
    reps = LANES // HEAD_DIM
    cosf = jnp.concatenate([cos_t, cos_t] * reps, axis=0).T
    sin_signed = jnp.concatenate([-sin_t, sin_t] * reps, axis=0).T

    def rope(v):
        rot = jnp.where(first_half, pltpu.roll(v, LANES - HEAD_DIM // 2, 1), pltpu.roll(v, HEAD_DIM // 2, 1))
        return v * cosf + rot * sin_signed

    lbt = lbt_ref[...]
    e = jnp.exp(lbt - jnp.max(lbt, axis=0, keepdims=True))
    sm = e / jnp.sum(e, axis=0, keepdims=True)
    lb = jnp.sum(sm[:layer + 1], axis=0, keepdims=True)

    zq = cols(0, ATTN_WIDTH)
    zkv = cols(ATTN_WIDTH, 2 * KV_WIDTH)

    def attn_piece(c):
        if c < ATTN_WIDTH // LANES:
            sl = slice(c * LANES, (c + 1) * LANES)
            q_ref[:, sl] = rope(head_norm(zq[:, sl], qg_ref[...])).astype(BF16)
        else:
            k_ref[...] = rope(head_norm(zkv[:, :KV_WIDTH], kg_ref[...])).astype(BF16)
            v_ref[...] = zkv[:, KV_WIDTH:].astype(BF16)

    def hgrn_piece(kind, sl):
        z = cols(N_ATTN + kind * HG_WIDTH + sl.start, sl.stop - sl.start)
        if kind == 0:
            hq_ref[:, sl] = _silu(z).astype(BF16)
        elif kind == 1:
            f = lb[:, sl] + (1.0 - lb[:, sl]) * _sigmoid(z)
            hk_ref[:, sl] = (1.0 - f).astype(BF16)
            hl_ref[:, sl] = jnp.log2(f)
        else:
            hi_ref[:, sl] = z.astype(BF16)

    chunk = 2 * LANES
    pieces = [(kind, slice(c, c + chunk)) for kind in range(3) for c in range(0, HG_WIDTH, chunk)]
    n_attn_pieces = ATTN_WIDTH // LANES + 1
    for idx, (kind, sl) in enumerate(pieces):
        hgrn_piece(kind, sl)
        if idx < n_attn_pieces:
            attn_piece(idx)


def _proj_a(x, gain, w_a, pos_rows, inv_freq, q_gain, k_gain, lb_table, *, layer, tm):
    n = x.shape[0]
    row = lambda i: (i, 0)
    whole = lambda i: (0, 0)
    widths = (ATTN_WIDTH, KV_WIDTH, KV_WIDTH, HG_WIDTH, HG_WIDTH, HG_WIDTH, HG_WIDTH, D_MODEL)
    dtypes = (BF16, BF16, BF16, BF16, BF16, F32, BF16, BF16)
    return pl.pallas_call(
        functools.partial(_proj_a_body, layer=layer),
        grid=(n // tm,),
        in_specs=[
            pl.BlockSpec((tm, D_MODEL), row),
            pl.BlockSpec((1, D_MODEL), whole),
            pl.BlockSpec((D_MODEL, COLS_A), whole, pipeline_mode=pl.Buffered(1)),
            pl.BlockSpec((None, 1, tm), lambda i: (i, 0, 0)),
            pl.BlockSpec((HEAD_DIM // 2, 1), whole),
            pl.BlockSpec((1, LANES), whole),
            pl.BlockSpec((1, LANES), whole),
            pl.BlockSpec(lb_table.shape, whole),
        ],
        out_specs=[pl.BlockSpec((tm, w), row) for w in widths],
        out_shape=[jax.ShapeDtypeStruct((n, w), d) for w, d in zip(widths, dtypes)],
        compiler_params=pltpu.CompilerParams(
            dimension_semantics=("parallel",), vmem_limit_bytes=VMEM_LIMIT),
        name="proj_a",
    )(x, gain, w_a, pos_rows, inv_freq, q_gain, k_gain, lb_table)


def _proj_b_body(h_ref, w_hbm, *rest):
    n_casts = (len(rest) - 5) // 2
    cast_src, (og_ref, ga_ref, gr_ref) = rest[:n_casts], rest[n_casts:n_casts + 3]
    cast_dst, (w_ref, sem) = rest[n_casts + 3:2 * n_casts + 3], rest[2 * n_casts + 3:]

    @pl.when(pl.program_id(0) == 0)
    def _():
        copy = pltpu.make_async_copy(w_hbm.at[:, COLS_A:], w_ref, sem)
        copy.start()
        copy.wait()

    def cols(start, width):
        return _dot(h_ref[...], w_ref[:, start:start + width])

    og_ref[...] = _silu(cols(0, HG_WIDTH)).astype(BF16)
    for c in range(D_MODEL // HG_WIDTH):
        sl = slice(c * HG_WIDTH, (c + 1) * HG_WIDTH)
        ga_ref[:, sl] = _sigmoid(cols(HG_WIDTH + c * HG_WIDTH, HG_WIDTH)).astype(BF16)
        gr_ref[:, sl] = _sigmoid(cols(HG_WIDTH + D_MODEL + c * HG_WIDTH, HG_WIDTH)).astype(BF16)
    _run_casts(cast_src, cast_dst)


def _proj_b(h, w_in, casts=(), *, tm):
    n = h.shape[0]
    row = lambda i: (i, 0)
    widths = (HG_WIDTH, D_MODEL, D_MODEL)
    cast_specs = _cast_specs(casts, n // tm)
    outs = pl.pallas_call(
        _proj_b_body,
        grid=(n // tm,),
        in_specs=[
            pl.BlockSpec((tm, D_MODEL), row),
            pl.BlockSpec(memory_space=pl.ANY),
        ] + cast_specs,
        out_specs=[pl.BlockSpec((tm, w), row) for w in widths] + cast_specs,
        out_shape=([jax.ShapeDtypeStruct((n, w), BF16) for w in widths]
                   + [jax.ShapeDtypeStruct(w.shape, BF16) for w in casts]),
        scratch_shapes=[pltpu.VMEM((D_MODEL, COLS_B), BF16), pltpu.SemaphoreType.DMA(())],
        compiler_params=pltpu.CompilerParams(
            dimension_semantics=("arbitrary",), vmem_limit_bytes=VMEM_LIMIT),
        name="proj_b",
    )(h, w_in, *casts)
    return outs[:3], outs[3:]


def _attn_body(sinks_ref, q_ref, k_ref, kp_ref, v_ref, vp_ref, o_ref, ks_ref, vs_ref, *, tq):
    t = pl.program_id(1)
    nb = tq // BLOCK
    lane = lax.broadcasted_iota(jnp.int32, (1, LANES), 1)
    low_head = lane < HEAD_DIM

    def put_kv(dst_ref, rows, v, pad):
        swapped = pltpu.roll(v, HEAD_DIM, 1)
        dst_ref[0, rows, :] = jnp.where(low_head, v, pad).astype(BF16)
        dst_ref[1, rows, :] = jnp.where(low_head, pad, swapped).astype(BF16)
        dst_ref[2, rows, :] = jnp.where(low_head, swapped, pad).astype(BF16)
        dst_ref[3, rows, :] = jnp.where(low_head, pad, v).astype(BF16)

    put_kv(ks_ref, slice(0, BLOCK), kp_ref[...].astype(F32), 0.0)
    put_kv(ks_ref, slice(BLOCK, BLOCK + tq), k_ref[...].astype(F32), 0.0)
    put_kv(vs_ref, slice(0, BLOCK), vp_ref[...].astype(F32), 1.0)
    put_kv(vs_ref, slice(BLOCK, BLOCK + tq), v_ref[...].astype(F32), 1.0)

    rows_q = PAIRS_PER_GROUP * BLOCK
    qi = lax.broadcasted_iota(jnp.int32, (rows_q, BLOCK), 0) % BLOCK
    kj = lax.broadcasted_iota(jnp.int32, (rows_q, BLOCK), 1)
    band_prev = kj > qi + BLOCK - WINDOW
    band_cur = kj <= qi
    sink_fill = {}
    for g in range(N_KV_HEADS):
        for half in range(2):
            sink_rows = jnp.concatenate(
                [jnp.full((BLOCK, LANES), sinks_ref[g * GQA_GROUP + 2 * p + half] * LOG2E, F32)
                 for p in range(PAIRS_PER_GROUP)], axis=0)
            sink_fill[g, half] = jnp.where(kj == 0, sink_rows, -jnp.inf)
    first_key = lax.broadcasted_iota(jnp.int32, (2 * BLOCK, LANES), 0) == 0
    drop_value = (first_key & low_head, first_key & jnp.logical_not(low_head))

    def block(n, carry):
        qrows = pl.ds(pl.multiple_of(n * BLOCK, BLOCK), BLOCK)
        krows = pl.ds(pl.multiple_of(n * BLOCK, BLOCK), 2 * BLOCK)
        valid_prev = band_prev & (t * nb + n > 0)

        def scores(g, half):
            qst = jnp.concatenate(
                [q_ref[qrows, (g * PAIRS_PER_GROUP + p) * LANES:(g * PAIRS_PER_GROUP + p + 1) * LANES]
                 for p in range(PAIRS_PER_GROUP)], axis=0)
            return _dot_nt(qst, ks_ref[2 * g + half, krows, :])

        units = [(g, half) for g in range(N_KV_HEADS) for half in range(2)]
        s_next = scores(*units[0])
        o = []
        for idx, (g, half) in enumerate(units):
            s = s_next
            if idx + 1 < len(units):
                s_next = scores(*units[idx + 1])
            s_prev = jnp.where(valid_prev, s[:, :BLOCK], sink_fill[g, half])
            s_cur = jnp.where(band_cur, s[:, BLOCK:], -jnp.inf)
            m = jnp.max(jnp.maximum(s_prev, s_cur), axis=-1, keepdims=True)
            p_ = jnp.concatenate([jnp.exp2(s_prev - m), jnp.exp2(s_cur - m)], axis=1).astype(BF16)
            vwin = vs_ref[2 * g + half, krows, :]
            vwin = jnp.where(drop_value[half], jnp.zeros_like(vwin), vwin)
            o.append(_dot(p_, vwin))
            if half == 1:
                num = jnp.where(low_head, o[0], o[1])
                den = pltpu.roll(jnp.where(low_head, o[1], o[0]), HEAD_DIM, 1)
                outs = num / den
                o = []
                for p in range(PAIRS_PER_GROUP):
                    c = g * PAIRS_PER_GROUP + p
                    o_ref[qrows, c * LANES:(c + 1) * LANES] = outs[p * BLOCK:(p + 1) * BLOCK].astype(BF16)
        return carry

    lax.fori_loop(0, nb, block, 0, unroll=min(4, nb))


def _attn(q, k, v, sinks, *, batch, seq, tq):
    n = batch * seq
    nt = seq // tq
    nb = tq // BLOCK
    bps = seq // BLOCK
    cur = lambda b, t: (b * nt + t, 0)
    prev = lambda b, t: (b * bps + jnp.maximum(t * nb - 1, 0), 0)
    return pl.pallas_call(
        functools.partial(_attn_body, tq=tq),
        grid=(batch, nt),
        in_specs=[
            pl.BlockSpec(memory_space=pltpu.SMEM),
            pl.BlockSpec((tq, ATTN_WIDTH), cur),
            pl.BlockSpec((tq, KV_WIDTH), cur),
            pl.BlockSpec((BLOCK, KV_WIDTH), prev),
            pl.BlockSpec((tq, KV_WIDTH), cur),
            pl.BlockSpec((BLOCK, KV_WIDTH), prev),
        ],
        out_specs=pl.BlockSpec((tq, ATTN_WIDTH), cur),
        out_shape=jax.ShapeDtypeStruct((n, ATTN_WIDTH), BF16),
        scratch_shapes=[
            pltpu.VMEM((4, tq + BLOCK, LANES), BF16),
            pltpu.VMEM((4, tq + BLOCK, LANES), BF16),
        ],
        compiler_params=pltpu.CompilerParams(
            dimension_semantics=("parallel", "parallel"), vmem_limit_bytes=VMEM_LIMIT),
        name="attn",
    )(sinks, q, k, k, v, v)


def _hgrn_body(hq_ref, hk_ref, hl_ref, hi_ref, og_ref, gain_ref, o_ref, state_ref, *, tile):
    n_seq = hq_ref.shape[0]

    @pl.when(pl.program_id(0) == 0)
    def _():
        state_ref[...] = jnp.zeros_like(state_ref)

    ri = lax.broadcasted_iota(jnp.int32, (CHUNK, CHUNK), 0)
    ci = lax.broadcasted_iota(jnp.int32, (CHUNK, CHUNK), 1)
    causal = ri >= ci
    tri = jnp.where(causal, 1.0, 0.0).astype(BF16)
    gain = gain_ref[...]
    heads = [slice(h * HG_HEAD_DIM, (h + 1) * HG_HEAD_DIM) for h in range(HG_HEADS)]
    units = [(s, h) for s in range(n_seq) for h in range(HG_HEADS)]

    def chunk(c, carry):
        rows = pl.ds(pl.multiple_of(c * CHUNK, CHUNK), CHUNK)
        q_mid, k_mid, q_in, k_out, decay = [], [], [], [], []
        for s in range(n_seq):
            lf = hl_ref[s, rows, :]
            hi = lf.astype(BF16)
            lo = (lf - hi.astype(F32)).astype(BF16)
            b = _dot(tri, hi) + _dot(tri, lo)
            b_mid = b[CHUNK // 2:CHUNK // 2 + 1]
            b_last = b[CHUNK - 1:CHUNK]
            qm = hq_ref[s, rows, :].astype(F32) * jnp.exp2(b - b_mid)
            km = hk_ref[s, rows, :].astype(F32) * jnp.exp2(b_mid - b)
            q_mid.append(qm.astype(BF16))
            k_mid.append(km.astype(BF16))
            q_in.append((qm * jnp.exp2(b_mid)).astype(BF16))
            k_out.append((km * jnp.exp2(b_last - b_mid)).astype(BF16))
            decay.append(jnp.exp2(b_last))
        att = [_dot_nt(q_mid[s][:, heads[h]], k_mid[s][:, heads[h]]) for s, h in units]
        st = [state_ref[s, h] for s, h in units]
        o_inter = [_dot_nt(q_in[s][:, heads[h]], st[u].astype(BF16)) for u, (s, h) in enumerate(units)]
        upd = [_dot_tn(hi_ref[s, rows, heads[h]], k_out[s][:, heads[h]]) for s, h in units]
        o = [o_inter[u] + _dot(jnp.where(causal, att[u], 0.0).astype(BF16), hi_ref[s, rows, heads[h]])
             for u, (s, h) in enumerate(units)]
        for u, (s, h) in enumerate(units):
            state_ref[s, h] = st[u] * decay[s][:, heads[h]] + upd[u]
            y = _rms_rows(o[u], gain) * og_ref[s, rows, heads[h]].astype(F32)
            o_ref[s, rows, heads[h]] = y.astype(BF16)
        return carry

    lax.fori_loop(0, tile // CHUNK, chunk, 0, unroll=True)


def _hgrn(hq, hk, hl, hi, og, gain, *, batch, seq, tile):
    n = batch * seq
    cur = lambda t: (0, t, 0)
    ins = [a.reshape(batch, seq, HG_WIDTH) for a in (hq, hk, hl, hi, og)]
    out = pl.pallas_call(
        functools.partial(_hgrn_body, tile=tile),
        grid=(seq // tile,),
        in_specs=[pl.BlockSpec((batch, tile, HG_WIDTH), cur)] * 5 + [pl.BlockSpec((1, HG_HEAD_DIM), lambda t: (0, 0))],
        out_specs=pl.BlockSpec((batch, tile, HG_WIDTH), cur),
        out_shape=jax.ShapeDtypeStruct((batch, seq, HG_WIDTH), BF16),
        scratch_shapes=[pltpu.VMEM((batch, HG_HEADS, HG_HEAD_DIM, HG_HEAD_DIM), F32)],
        compiler_params=pltpu.CompilerParams(
            dimension_semantics=("arbitrary",), vmem_limit_bytes=VMEM_LIMIT),
        name="hgrn",
    )(*ins, gain)
    return out.reshape(n, HG_WIDTH)


def _merge_body(x_ref, ya_ref, yh_ref, ga_ref, gr_ref, wa_ref, wr_ref, wo_ref, *rest):
    n_casts = (len(rest) - 1) // 2
    cast_src, o_ref, cast_dst = rest[:n_casts], rest[n_casts], rest[n_casts + 1:]
    a = _dot(ya_ref[...], wa_ref[...])
    r = _dot(yh_ref[...], wr_ref[...])
    merged = (ga_ref[...].astype(F32) * a + gr_ref[...].astype(F32) * r).astype(BF16)
    o_ref[...] = x_ref[...] + _dot(merged, wo_ref[...])
    _run_casts(cast_src, cast_dst)


def _merge(x, ya, yh, ga, gr, w_a, w_r, w_o, casts=(), *, tm):
    n = x.shape[0]
    row = lambda i: (i, 0)
    whole = lambda i: (0, 0)
    cast_specs = _cast_specs(casts, n // tm)
    outs = pl.pallas_call(
        _merge_body,
        grid=(n // tm,),
        in_specs=[
            pl.BlockSpec((tm, D_MODEL), row),
            pl.BlockSpec((tm, ATTN_WIDTH), row),
            pl.BlockSpec((tm, HG_WIDTH), row),
            pl.BlockSpec((tm, D_MODEL), row),
            pl.BlockSpec((tm, D_MODEL), row),
            pl.BlockSpec((ATTN_WIDTH, D_MODEL), whole, pipeline_mode=pl.Buffered(1)),
            pl.BlockSpec((HG_WIDTH, D_MODEL), whole, pipeline_mode=pl.Buffered(1)),
            pl.BlockSpec((D_MODEL, D_MODEL), whole, pipeline_mode=pl.Buffered(1)),
        ] + cast_specs,
        out_specs=[pl.BlockSpec((tm, D_MODEL), row)] + cast_specs,
        out_shape=[jax.ShapeDtypeStruct((n, D_MODEL), F32)] + [jax.ShapeDtypeStruct(w.shape, BF16) for w in casts],
        compiler_params=pltpu.CompilerParams(
            dimension_semantics=("arbitrary",), vmem_limit_bytes=VMEM_LIMIT),
        name="merge",
    )(x, ya, yh, ga, gr, w_a, w_r, w_o, *casts)
    return outs[0], outs[1:]


def _tile(n, want):
    t = min(want, n)
    assert n % t == 0, (n, t)
    return t


def kernel(x, positions, lb_table, ffn1_norm, ffn1_w_gu, ffn1_w_down, mix_norm, w_in, q_norm, k_norm, sinks,
           hg_out_norm, w_attn_branch, w_hg_branch, w_out, ffn2_norm, ffn2_w_gu, ffn2_w_down):
    batch, seq = x.shape[0], x.shape[1]
    n = batch * seq
    depth = w_in.shape[0]
    assert seq % BLOCK == 0 and seq % CHUNK == 0

    half = HEAD_DIM // 2
    inv_freq = (ROPE_THETA ** (-jnp.arange(half, dtype=F32) * 2.0 / HEAD_DIM))[:, None]
    tm_proj = _tile(n, PROJ_A_TM)
    pos_rows = positions.astype(F32).reshape(n // tm_proj, 1, tm_proj)

    xf = x.reshape(n, D_MODEL)
    for l in range(depth):
        xf, (w_in_l,) = _ffn(xf, ffn1_norm[l][None, :], ffn1_w_gu[l].astype(BF16), ffn1_w_down[l].astype(BF16),
                             casts=(w_in[l],), tm=_tile(n, FFN_TM), tf=FFN_TF)
        gain = mix_norm[l][None, :]
        q_gain = jnp.tile(q_norm[l], 2)[None, :] * (HEAD_DIM ** -0.5 * LOG2E)
        q, k, v, hq, hk, hl, hi, h_mix = _proj_a(
            xf, gain, w_in_l, pos_rows, inv_freq, q_gain, jnp.tile(k_norm[l], 2)[None, :], lb_table,
            layer=l, tm=tm_proj)
        (og, ga, gr), (w_a, w_r, w_o) = _proj_b(
            h_mix, w_in_l, casts=(w_attn_branch[l], w_hg_branch[l], w_out[l]), tm=_tile(n, PROJ_B_TM))
        ya = _attn(q, k, v, sinks[l], batch=batch, seq=seq, tq=_tile(seq, ATTN_TQ))
        yh = _hgrn(hq, hk, hl, hi, og, hg_out_norm[l][None, :], batch=batch, seq=seq, tile=_tile(seq, HGRN_TILE))
        xf, (w_gu2, w_down2) = _merge(
            xf, ya, yh, ga, gr, w_a, w_r, w_o, casts=(ffn2_w_gu[l], ffn2_w_down[l]), tm=_tile(n, MERGE_TM))
        xf, _ = _ffn(xf, ffn2_norm[l][None, :], w_gu2, w_down2, tm=_tile(n, FFN_TM), tf=FFN_TF)
    return xf.reshape(batch, seq, D_MODEL)
```

```python
import functools

import jax
import jax.numpy as jnp
from jax import lax
from jax.experimental import pallas as pl
from jax.experimental.pallas import tpu as pltpu

D_MODEL = 2048
HEAD_DIM = 64
N_Q_HEADS = 16
N_KV_HEADS = 2
GQA_GROUP = N_Q_HEADS // N_KV_HEADS
ATTN_WIDTH = N_Q_HEADS * HEAD_DIM
KV_WIDTH = N_KV_HEADS * HEAD_DIM
WINDOW = 128
BLOCK = 128
ROPE_THETA = 10000.0
HG_HEAD_DIM = 128
HG_HEADS = 8
HG_WIDTH = HG_HEADS * HG_HEAD_DIM
CHUNK = 64
D_FF = 5632
EPS = 1e-6

LANES = 128
PAIRS_PER_GROUP = GQA_GROUP // 2
VMEM_LIMIT = 58 * 1024 * 1024
LOG2E = 1.4426950408889634

FFN_TM, FFN_TF = 1024, 512
PROJ_A_TM, PROJ_B_TM, MERGE_TM = 512, 1024, 512
ATTN_TQ, HGRN_TILE = 2048, 512

N_ATTN = ATTN_WIDTH + 2 * KV_WIDTH
COLS_A = N_ATTN + 3 * HG_WIDTH
COLS_B = HG_WIDTH + 2 * D_MODEL

F32 = jnp.float32
BF16 = jnp.bfloat16


def _dot(a, b):
    return jnp.dot(a, b, preferred_element_type=F32)


def _dot_nt(a, b):
    return lax.dot_general(a, b, (((1,), (1,)), ((), ())), preferred_element_type=F32)


def _dot_tn(a, b):
    return lax.dot_general(a, b, (((0,), (0,)), ((), ())), preferred_element_type=F32)


def _sigmoid(z):
    return 0.5 * jnp.tanh(0.5 * z) + 0.5


def _silu(z):
    half = 0.5 * z
    return half * jnp.tanh(half) + half


def _rms_rows(x, gain):
    ms = jnp.mean(x * x, axis=-1, keepdims=True)
    return x * lax.rsqrt(ms + EPS) * gain


BF16_SUBLANES = 16


def _cast_specs(casts, steps, step_of=lambda i: i):
    specs = []
    for w in casts:
        rows = w.shape[0]
        rb = next(r for r in range(BF16_SUBLANES, rows + 1, BF16_SUBLANES) if rows % r == 0 and rows // r <= steps)
        specs.append(pl.BlockSpec(
            (rb, w.shape[1]),
            functools.partial(lambda *idx, last: (jnp.minimum(step_of(*idx), last), 0), last=rows // rb - 1)))
    return specs


def _run_casts(src_refs, dst_refs):
    for src_ref, dst_ref in zip(src_refs, dst_refs):
        dst_ref[...] = src_ref[...].astype(BF16)


FFN_NORM_ROWS = 128
FFN_NORM_FIRST = 2


def _ffn_body(x_hbm, g_ref, wg_ref, wu_ref, wd_ref, *rest, tm, tf):
    n_casts = (len(rest) - 5) // 2
    cast_src, o_ref, cast_dst = rest[:n_casts], rest[n_casts], rest[n_casts + 1:2 * n_casts + 1]
    h0_ref, h1_ref, xs_ref, sem = rest[2 * n_casts + 1:]
    i = pl.program_id(0)
    j = pl.program_id(1)
    n_tiles = pl.num_programs(0)
    hbufs = (h0_ref, h1_ref)

    def x_copy(tile):
        rows = pl.ds(pl.multiple_of(tile * tm, tm), tm)
        return pltpu.make_async_copy(x_hbm.at[rows], xs_ref.at[:tm], sem)

    @pl.when((i == 0) & (j == 0))
    def _():
        xs_ref[tm:, :] = jnp.zeros((FFN_NORM_ROWS, D_MODEL), F32)
        x_copy(0).start()
        x_copy(0).wait()
        h0_ref[:tm, :] = _rms_rows(xs_ref[:tm, :], g_ref[...]).astype(BF16)

    @pl.when((j == FFN_NORM_FIRST) & (i + 1 < n_tiles))
    def _():
        x_copy(i + 1).wait()

    def chunk_step(parity, first):
        h = hbufs[parity][:tm, :]
        halves = [slice(c * tf // 2, (c + 1) * tf // 2) for c in range(2)]
        gate_up = [(_dot(h, wg_ref[:, sl]), _dot(h, wu_ref[:, sl])) for sl in halves]
        act = [(_silu(g) * (u * 0.5)).astype(BF16) for g, u in gate_up]
        down = _dot(act[0], wd_ref[halves[0], :]) + _dot(act[1], wd_ref[halves[1], :])
        if first:
            o_ref[...] = xs_ref[:tm, :] + down

            @pl.when(i + 1 < n_tiles)
            def _():
                x_copy(i + 1).start()
        else:
            o_ref[...] += down
            part = j - FFN_NORM_FIRST
            live = (part < tm // FFN_NORM_ROWS) & (part >= 0) & (i + 1 < n_tiles)
            rows = pl.ds(pl.multiple_of(jnp.where(live, part * FFN_NORM_ROWS, tm), FFN_NORM_ROWS), FFN_NORM_ROWS)
            hbufs[1 - parity][rows, :] = _rms_rows(xs_ref[rows, :], g_ref[...]).astype(BF16)
        _run_casts(cast_src, cast_dst)

    for parity in range(2):
        for first in (True, False):
            pl.when((i % 2 == parity) & ((j == 0) == first))(functools.partial(chunk_step, parity, first))


def _ffn(x, gain, w_gu, w_down, casts=(), *, tm, tf):
    n = x.shape[0]
    nf = D_FF // tf
    assert tm % FFN_NORM_ROWS == 0 and nf >= FFN_NORM_FIRST + tm // FFN_NORM_ROWS
    cast_specs = _cast_specs(casts, (n // tm) * nf, lambda i, j: i * nf + j)
    outs = pl.pallas_call(
        functools.partial(_ffn_body, tm=tm, tf=tf),
        grid=(n // tm, nf),
        in_specs=[
            pl.BlockSpec(memory_space=pl.ANY),
            pl.BlockSpec((1, D_MODEL), lambda i, j: (0, 0)),
            pl.BlockSpec((D_MODEL, tf), lambda i, j: (0, j)),
            pl.BlockSpec((D_MODEL, tf), lambda i, j: (0, j + nf)),
            pl.BlockSpec((tf, D_MODEL), lambda i, j: (j, 0)),
        ] + cast_specs,
        out_specs=[pl.BlockSpec((tm, D_MODEL), lambda i, j: (i, 0))] + cast_specs,
        out_shape=[jax.ShapeDtypeStruct((n, D_MODEL), F32)] + [jax.ShapeDtypeStruct(w.shape, BF16) for w in casts],
        scratch_shapes=[pltpu.VMEM((tm + FFN_NORM_ROWS, D_MODEL), BF16), pltpu.VMEM((tm + FFN_NORM_ROWS, D_MODEL), BF16),
                        pltpu.VMEM((tm + FFN_NORM_ROWS, D_MODEL), F32), pltpu.SemaphoreType.DMA(())],
        compiler_params=pltpu.CompilerParams(
            dimension_semantics=("arbitrary", "arbitrary"), vmem_limit_bytes=VMEM_LIMIT),
        name="ffn",
    )(x, gain, w_gu, w_gu, w_down, *casts)
    return outs[0], outs[1:]


def _proj_a_body(x_ref, g_ref, w_ref, pos_ref, invf_ref, qg_ref, kg_ref, lbt_ref,
                 q_ref, k_ref, v_ref, hq_ref, hk_ref, hl_ref, hi_ref, h_ref, *, layer):
    h = _rms_rows(x_ref[...], g_ref[...]).astype(BF16)
    h_ref[...] = h

    def cols(start, width):
        return _dot(h, w_ref[:, start:start + width])

    lane = lax.broadcasted_iota(jnp.int32, (1, LANES), 1)
    first_half = (lane % HEAD_DIM) < (HEAD_DIM // 2)
    gi = lax.broadcasted_iota(jnp.int32, (LANES, LANES), 0) // HEAD_DIM
    gj = lax.broadcasted_iota(jnp.int32, (LANES, LANES), 1) // HEAD_DIM
    gsum = jnp.where(gi == gj, 1.0, 0.0).astype(BF16)

    def head_norm(v, gain):
        sq = v * v
        hi = sq.astype(BF16)
        lo = (sq - hi.astype(F32)).astype(BF16)
        ss = _dot(hi, gsum) + _dot(lo, gsum)
        return v * lax.rsqrt(ss * (1.0 / HEAD_DIM) + EPS) * gain

    ang = invf_ref[...] * pos_ref[...]
    cos_t = jnp.cos(ang)
    sin_t = jnp.sin(ang)
    reps = LANES // HEAD_DIM
    cosf = jnp.concatenate([cos_t, cos_t] * reps, axis=0).T
    sin_signed = jnp.concatenate([-sin_t, sin_t] * reps, axis=0).T

    def rope(v):
        rot = jnp.where(first_half, pltpu.roll(v, LANES - HEAD_DIM // 2, 1), pltpu.roll(v, HEAD_DIM // 2, 1))
        return v * cosf + rot * sin_signed

    lbt = lbt_ref[...]
    e = jnp.exp(lbt - jnp.max(lbt, axis=0, keepdims=True))
    sm = e / jnp.sum(e, axis=0, keepdims=True)
    lb = jnp.sum(sm[:layer + 1], axis=0, keepdims=True)

    zq = cols(0, ATTN_WIDTH)
    zkv = cols(ATTN_WIDTH, 2 * KV_WIDTH)

    def attn_piece(c):
        if c < ATTN_WIDTH // LANES:
            sl = slice(c * LANES, (c + 1) * LANES)
            q_ref[:, sl] = rope(head_norm(zq[:, sl], qg_ref[...])).astype(BF16)
        else:
            k_ref[...] = rope(head_norm(zkv[:, :KV_WIDTH], kg_ref[...])).astype(BF16)
            v_ref[...] = zkv[:, KV_WIDTH:].astype(BF16)

    def hgrn_piece(kind, sl):
        z = cols(N_ATTN + kind * HG_WIDTH + sl.start, sl.stop - sl.start)
        if kind == 0:
            hq_ref[:, sl] = _silu(z).astype(BF16)
        elif kind == 1:
            f = lb[:, sl] + (1.0 - lb[:, sl]) * _sigmoid(z)
            hk_ref[:, sl] = (1.0 - f).astype(BF16)
            hl_ref[:, sl] = jnp.log2(f)
        else:
            hi_ref[:, sl] = z.astype(BF16)

    chunk = 2 * LANES
    pieces = [(kind, slice(c, c + chunk)) for kind in range(3) for c in range(0, HG_WIDTH, chunk)]
    n_attn_pieces = ATTN_WIDTH // LANES + 1
    for idx, (kind, sl) in enumerate(pieces):
        hgrn_piece(kind, sl)
        if idx < n_attn_pieces:
            attn_piece(idx)


def _proj_a(x, gain, w_a, pos_rows, inv_freq, q_gain, k_gain, lb_table, *, layer, tm):
    n = x.shape[0]
    row = lambda i: (i, 0)
    whole = lambda i: (0, 0)
    widths = (ATTN_WIDTH, KV_WIDTH, KV_WIDTH, HG_WIDTH, HG_WIDTH, HG_WIDTH, HG_WIDTH, D_MODEL)
    dtypes = (BF16, BF16, BF16, BF16, BF16, F32, BF16, BF16)
    return pl.pallas_call(
        functools.partial(_proj_a_body, layer=layer),
        grid=(n // tm,),
        in_specs=[
            pl.BlockSpec((tm, D_MODEL), row),
            pl.BlockSpec((1, D_MODEL), whole),
            pl.BlockSpec((D_MODEL, COLS_A), whole, pipeline_mode=pl.Buffered(1)),
            pl.BlockSpec((None, 1, tm), lambda i: (i, 0, 0)),
            pl.BlockSpec((HEAD_DIM // 2, 1), whole),
            pl.BlockSpec((1, LANES), whole),
            pl.BlockSpec((1, LANES), whole),
            pl.BlockSpec(lb_table.shape, whole),
        ],
        out_specs=[pl.BlockSpec((tm, w), row) for w in widths],
        out_shape=[jax.ShapeDtypeStruct((n, w), d) for w, d in zip(widths, dtypes)],
        compiler_params=pltpu.CompilerParams(
            dimension_semantics=("parallel",), vmem_limit_bytes=VMEM_LIMIT),
        name="proj_a",
    )(x, gain, w_a, pos_rows, inv_freq, q_gain, k_gain, lb_table)


def _proj_b_body(h_ref, w_hbm, *rest):
    n_casts = (len(rest) - 5) // 2
    cast_src, (og_ref, ga_ref, gr_ref) = rest[:n_casts], rest[n_casts:n_casts + 3]
    cast_dst, (w_ref, sem) = rest[n_casts + 3:2 * n_casts + 3], rest[2 * n_casts + 3:]

    @pl.when(pl.program_id(0) == 0)
    def _():
        copy = pltpu.make_async_copy(w_hbm.at[:, COLS_A:], w_ref, sem)
        copy.start()
        copy.wait()

    def cols(start, width):
        return _dot(h_ref[...], w_ref[:, start:start + width])

    og_ref[...] = _silu(cols(0, HG_WIDTH)).astype(BF16)
    for c in range(D_MODEL // HG_WIDTH):
        sl = slice(c * HG_WIDTH, (c + 1) * HG_WIDTH)
        ga_ref[:, sl] = _sigmoid(cols(HG_WIDTH + c * HG_WIDTH, HG_WIDTH)).astype(BF16)
        gr_ref[:, sl] = _sigmoid(cols(HG_WIDTH + D_MODEL + c * HG_WIDTH, HG_WIDTH)).astype(BF16)
    _run_casts(cast_src, cast_dst)


def _proj_b(h, w_in, casts=(), *, tm):
    n = h.shape[0]
    row = lambda i: (i, 0)
    widths = (HG_WIDTH, D_MODEL, D_MODEL)
    cast_specs = _cast_specs(casts, n // tm)
    outs = pl.pallas_call(
        _proj_b_body,
        grid=(n // tm,),
        in_specs=[
            pl.BlockSpec((tm, D_MODEL), row),
            pl.BlockSpec(memory_space=pl.ANY),
        ] + cast_specs,
        out_specs=[pl.BlockSpec((tm, w), row) for w in widths] + cast_specs,
        out_shape=([jax.ShapeDtypeStruct((n, w), BF16) for w in widths]
                   + [jax.ShapeDtypeStruct(w.shape, BF16) for w in casts]),
        scratch_shapes=[pltpu.VMEM((D_MODEL, COLS_B), BF16), pltpu.SemaphoreType.DMA(())],
        compiler_params=pltpu.CompilerParams(
            dimension_semantics=("arbitrary",), vmem_limit_bytes=VMEM_LIMIT),
        name="proj_b",
    )(h, w_in, *casts)
    return outs[:3], outs[3:]


def _attn_body(sinks_ref, q_ref, k_ref, kp_ref, v_ref, vp_ref, o_ref, ks_ref, vs_ref, *, tq):
    t = pl.program_id(1)
    nb = tq // BLOCK
    lane = lax.broadcasted_iota(jnp.int32, (1, LANES), 1)
    low_head = lane < HEAD_DIM

    def put_kv(dst_ref, rows, v, pad):
        swapped = pltpu.roll(v, HEAD_DIM, 1)
        dst_ref[0, rows, :] = jnp.where(low_head, v, pad).astype(BF16)
        dst_ref[1, rows, :] = jnp.where(low_head, pad, swapped).astype(BF16)
        dst_ref[2, rows, :] = jnp.where(low_head, swapped, pad).astype(BF16)
        dst_ref[3, rows, :] = jnp.where(low_head, pad, v).astype(BF16)

    put_kv(ks_ref, slice(0, BLOCK), kp_ref[...].astype(F32), 0.0)
    put_kv(ks_ref, slice(BLOCK, BLOCK + tq), k_ref[...].astype(F32), 0.0)
    put_kv(vs_ref, slice(0, BLOCK), vp_ref[...].astype(F32), 1.0)
    put_kv(vs_ref, slice(BLOCK, BLOCK + tq), v_ref[...].astype(F32), 1.0)

    rows_q = PAIRS_PER_GROUP * BLOCK
    qi = lax.broadcasted_iota(jnp.int32, (rows_q, BLOCK), 0) % BLOCK
    kj = lax.broadcasted_iota(jnp.int32, (rows_q, BLOCK), 1)
    band_prev = kj > qi + BLOCK - WINDOW
    band_cur = kj <= qi
    sink_fill = {}
    for g in range(N_KV_HEADS):
        for half in range(2):
            sink_rows = jnp.concatenate(
                [jnp.full((BLOCK, LANES), sinks_ref[g * GQA_GROUP + 2 * p + half] * LOG2E, F32)
                 for p in range(PAIRS_PER_GROUP)], axis=0)
            sink_fill[g, half] = jnp.where(kj == 0, sink_rows, -jnp.inf)
    first_key = lax.broadcasted_iota(jnp.int32, (2 * BLOCK, LANES), 0) == 0
    drop_value = (first_key & low_head, first_key & jnp.logical_not(low_head))

    def block(n, carry):
        qrows = pl.ds(pl.multiple_of(n * BLOCK, BLOCK), BLOCK)
        krows = pl.ds(pl.multiple_of(n * BLOCK, BLOCK), 2 * BLOCK)
        valid_prev = band_prev & (t * nb + n > 0)

        def scores(g, half):
            qst = jnp.concatenate(
                [q_ref[qrows, (g * PAIRS_PER_GROUP + p) * LANES:(g * PAIRS_PER_GROUP + p + 1) * LANES]
                 for p in range(PAIRS_PER_GROUP)], axis=0)
            return _dot_nt(qst, ks_ref[2 * g + half, krows, :])

        units = [(g, half) for g in range(N_KV_HEADS) for half in range(2)]
        s_next = scores(*units[0])
        o = []
        for idx, (g, half) in enumerate(units):
            s = s_next
            if idx + 1 < len(units):
                s_next = scores(*units[idx + 1])
            s_prev = jnp.where(valid_prev, s[:, :BLOCK], sink_fill[g, half])
            s_cur = jnp.where(band_cur, s[:, BLOCK:], -jnp.inf)
            m = jnp.max(jnp.maximum(s_prev, s_cur), axis=-1, keepdims=True)
            p_ = jnp.concatenate([jnp.exp2(s_prev - m), jnp.exp2(s_cur - m)], axis=1).astype(BF16)
            vwin = vs_ref[2 * g + half, krows, :]
            vwin = jnp.where(drop_value[half], jnp.zeros_like(vwin), vwin)
            o.append(_dot(p_, vwin))
            if half == 1:
                num = jnp.where(low_head, o[0], o[1])
                den = pltpu.roll(jnp.where(low_head, o[1], o[0]), HEAD_DIM, 1)
                outs = num / den
                o = []
                for p in range(PAIRS_PER_GROUP):
                    c = g * PAIRS_PER_GROUP + p
                    o_ref[qrows, c * LANES:(c + 1) * LANES] = outs[p * BLOCK:(p + 1) * BLOCK].astype(BF16)
        return carry

    lax.fori_loop(0, nb, block, 0, unroll=min(4, nb))


def _attn(q, k, v, sinks, *, batch, seq, tq):
    n = batch * seq
    nt = seq // tq
    nb = tq // BLOCK
    bps = seq // BLOCK
    cur = lambda b, t: (b * nt + t, 0)
    prev = lambda b, t: (b * bps + jnp.maximum(t * nb - 1, 0), 0)
    return pl.pallas_call(
        functools.partial(_attn_body, tq=tq),
        grid=(batch, nt),
        in_specs=[
            pl.BlockSpec(memory_space=pltpu.SMEM),
            pl.BlockSpec((tq, ATTN_WIDTH), cur),
            pl.BlockSpec((tq, KV_WIDTH), cur),
            pl.BlockSpec((BLOCK, KV_WIDTH), prev),
            pl.BlockSpec((tq, KV_WIDTH), cur),
            pl.BlockSpec((BLOCK, KV_WIDTH), prev),
        ],
        out_specs=pl.BlockSpec((tq, ATTN_WIDTH), cur),
        out_shape=jax.ShapeDtypeStruct((n, ATTN_WIDTH), BF16),
        scratch_shapes=[
            pltpu.VMEM((4, tq + BLOCK, LANES), BF16),
            pltpu.VMEM((4, tq + BLOCK, LANES), BF16),
        ],
        compiler_params=pltpu.CompilerParams(
            dimension_semantics=("parallel", "parallel"), vmem_limit_bytes=VMEM_LIMIT),
        name="attn",
    )(sinks, q, k, k, v, v)


def _hgrn_body(hq_ref, hk_ref, hl_ref, hi_ref, og_ref, gain_ref, o_ref, state_ref, *, tile):
    n_seq = hq_ref.shape[0]

    @pl.when(pl.program_id(0) == 0)
    def _():
        state_ref[...] = jnp.zeros_like(state_ref)

    ri = lax.broadcasted_iota(jnp.int32, (CHUNK, CHUNK), 0)
    ci = lax.broadcasted_iota(jnp.int32, (CHUNK, CHUNK), 1)
    causal = ri >= ci
    tri = jnp.where(causal, 1.0, 0.0).astype(BF16)
    gain = gain_ref[...]
    ones = jnp.ones((CHUNK, HG_HEAD_DIM), BF16)
    heads = [slice(h * HG_HEAD_DIM, (h + 1) * HG_HEAD_DIM) for h in range(HG_HEADS)]
    units = [(s, h) for s in range(n_seq) for h in range(HG_HEADS)]

    def chunk(c, carry):
        rows = pl.ds(pl.multiple_of(c * CHUNK, CHUNK), CHUNK)
        q_mid, k_mid_t, q_in, k_out, decay = [], [], [], [], []
        for s in range(n_seq):
            lf = hl_ref[s, rows, :]
            hi = lf.astype(BF16)
            lo = (lf - hi.astype(F32)).astype(BF16)
            b = _dot(tri, hi) + _dot(tri, lo)
            b_mid = b[CHUNK // 2:CHUNK // 2 + 1]
            b_last = b[CHUNK - 1:CHUNK]
            qm = hq_ref[s, rows, :].astype(F32) * jnp.exp2(b - b_mid)
            km = hk_ref[s, rows, :].astype(F32) * jnp.exp2(b_mid - b)
            q_mid.append(qm.astype(BF16))
            k_mid_t.append(km.T.astype(BF16))
            q_in.append((qm * jnp.exp2(b_mid)).astype(BF16))
            k_out.append((km * jnp.exp2(b_last - b_mid)).astype(BF16))
            decay.append(jnp.exp2(_dot_tn(hi, ones) + _dot_tn(lo, ones)))
        att = [_dot(q_mid[s][:, heads[h]], k_mid_t[s][heads[h], :]) for s, h in units]
        st = [state_ref[s, h] for s, h in units]
        o_inter = [_dot(q_in[s][:, heads[h]], st[u].astype(BF16)) for u, (s, h) in enumerate(units)]
        upd = [_dot_tn(k_out[s][:, heads[h]], hi_ref[s, rows, heads[h]]) for s, h in units]
        o = [o_inter[u] + _dot(jnp.where(causal, att[u], 0.0).astype(BF16), hi_ref[s, rows, heads[h]])
             for u, (s, h) in enumerate(units)]
        for u, (s, h) in enumerate(units):
            state_ref[s, h] = st[u] * decay[s][heads[h], :] + upd[u]
            y = _rms_rows(o[u], gain) * og_ref[s, rows, heads[h]].astype(F32)
            o_ref[s, rows, heads[h]] = y.astype(BF16)
        return carry

    lax.fori_loop(0, tile // CHUNK, chunk, 0, unroll=True)


def _hgrn(hq, hk, hl, hi, og, gain, *, batch, seq, tile):
    n = batch * seq
    cur = lambda t: (0, t, 0)
    ins = [a.reshape(batch, seq, HG_WIDTH) for a in (hq, hk, hl, hi, og)]
    out = pl.pallas_call(
        functools.partial(_hgrn_body, tile=tile),
        grid=(seq // tile,),
        in_specs=[pl.BlockSpec((batch, tile, HG_WIDTH), cur)] * 5 + [pl.BlockSpec((1, HG_HEAD_DIM), lambda t: (0, 0))],
        out_specs=pl.BlockSpec((batch, tile, HG_WIDTH), cur),
        out_shape=jax.ShapeDtypeStruct((batch, seq, HG_WIDTH), BF16),
        scratch_shapes=[pltpu.VMEM((batch, HG_HEADS, HG_HEAD_DIM, HG_HEAD_DIM), F32)],
        compiler_params=pltpu.CompilerParams(
            dimension_semantics=("arbitrary",), vmem_limit_bytes=VMEM_LIMIT),
        name="hgrn",
    )(*ins, gain)
    return out.reshape(n, HG_WIDTH)


def _merge_body(x_ref, ya_ref, yh_ref, ga_ref, gr_ref, wa_ref, wr_ref, wo_ref, *rest):
    n_casts = (len(rest) - 1) // 2
    cast_src, o_ref, cast_dst = rest[:n_casts], rest[n_casts], rest[n_casts + 1:]
    a = _dot(ya_ref[...], wa_ref[...])
    r = _dot(yh_ref[...], wr_ref[...])
    merged = (ga_ref[...].astype(F32) * a + gr_ref[...].astype(F32) * r).astype(BF16)
    o_ref[...] = x_ref[...] + _dot(merged, wo_ref[...])
    _run_casts(cast_src, cast_dst)


def _merge(x, ya, yh, ga, gr, w_a, w_r, w_o, casts=(), *, tm):
    n = x.shape[0]
    row = lambda i: (i, 0)
    whole = lambda i: (0, 0)
    cast_specs = _cast_specs(casts, n // tm)
    outs = pl.pallas_call(
        _merge_body,
        grid=(n // tm,),
        in_specs=[
            pl.BlockSpec((tm, D_MODEL), row),
            pl.BlockSpec((tm, ATTN_WIDTH), row),
            pl.BlockSpec((tm, HG_WIDTH), row),
            pl.BlockSpec((tm, D_MODEL), row),
            pl.BlockSpec((tm, D_MODEL), row),
            pl.BlockSpec((ATTN_WIDTH, D_MODEL), whole, pipeline_mode=pl.Buffered(1)),
            pl.BlockSpec((HG_WIDTH, D_MODEL), whole, pipeline_mode=pl.Buffered(1)),
            pl.BlockSpec((D_MODEL, D_MODEL), whole, pipeline_mode=pl.Buffered(1)),
        ] + cast_specs,
        out_specs=[pl.BlockSpec((tm, D_MODEL), row)] + cast_specs,
        out_shape=[jax.ShapeDtypeStruct((n, D_MODEL), F32)] + [jax.ShapeDtypeStruct(w.shape, BF16) for w in casts],
        compiler_params=pltpu.CompilerParams(
            dimension_semantics=("arbitrary",), vmem_limit_bytes=VMEM_LIMIT),
        name="merge",
    )(x, ya, yh, ga, gr, w_a, w_r, w_o, *casts)
    return outs[0], outs[1:]


def _tile(n, want):
    t = min(want, n)
    assert n % t == 0, (n, t)
    return t


def kernel(x, positions, lb_table, ffn1_norm, ffn1_w_gu, ffn1_w_down, mix_norm, w_in, q_norm, k_norm, sinks,
           hg_out_norm, w_attn_branch, w_hg_branch, w_out, ffn2_norm, ffn2_w_gu, ffn2_w_down):
    batch, seq = x.shape[0], x.shape[1]
    n = batch * seq
    depth = w_in.shape[0]
    assert seq % BLOCK == 0 and seq % CHUNK == 0

    half = HEAD_DIM // 2
    inv_freq = (ROPE_THETA ** (-jnp.arange(half, dtype=F32) * 2.0 / HEAD_DIM))[:, None]
    tm_proj = _tile(n, PROJ_A_TM)
    pos_rows = positions.astype(F32).reshape(n // tm_proj, 1, tm_proj)

    xf = x.reshape(n, D_MODEL)
    for l in range(depth):
        xf, (w_in_l,) = _ffn(xf, ffn1_norm[l][None, :], ffn1_w_gu[l].astype(BF16), ffn1_w_down[l].astype(BF16),
                             casts=(w_in[l],), tm=_tile(n, FFN_TM), tf=FFN_TF)
        gain = mix_norm[l][None, :]
        q_gain = jnp.tile(q_norm[l], 2)[None, :] * (HEAD_DIM ** -0.5 * LOG2E)
        q, k, v, hq, hk, hl, hi, h_mix = _proj_a(
            xf, gain, w_in_l, pos_rows, inv_freq, q_gain, jnp.tile(k_norm[l], 2)[None, :], lb_table,
            layer=l, tm=tm_proj)
        (og, ga, gr), (w_a, w_r, w_o) = _proj_b(
            h_mix, w_in_l, casts=(w_attn_branch[l], w_hg_branch[l], w_out[l]), tm=_tile(n, PROJ_B_TM))
        ya = _attn(q, k, v, sinks[l], batch=batch, seq=seq, tq=_tile(seq, ATTN_TQ))
        yh = _hgrn(hq, hk, hl, hi, og, hg_out_norm[l][None, :], batch=batch, seq=seq, tile=_tile(seq, HGRN_TILE))
        xf, (w_gu2, w_down2) = _merge(
            xf, ya, yh, ga, gr, w_a, w_r, w_o, casts=(ffn2_w_gu[l], ffn2_w_down[l]), tm=_tile(n, MERGE_TM))
        xf, _ = _ffn(xf, ffn2_norm[l][None, :], w_gu2, w_down2, tm=_tile(n, FFN_TM), tf=FFN_TF)
    return xf.reshape(batch, seq, D_MODEL)
```

```python
import functools

import jax
import jax.numpy as jnp
from jax import lax
from jax.experimental import pallas as pl
from jax.experimental.pallas import tpu as pltpu

D_MODEL = 2048
HEAD_DIM = 64
N_Q_HEADS = 16
N_KV_HEADS = 2
GQA_GROUP = N_Q_HEADS // N_KV_HEADS
ATTN_WIDTH = N_Q_HEADS * HEAD_DIM
KV_WIDTH = N_KV_HEADS * HEAD_DIM
WINDOW = 128
BLOCK = 128
ROPE_THETA = 10000.0
HG_HEAD_DIM = 128
HG_HEADS = 8
HG_WIDTH = HG_HEADS * HG_HEAD_DIM
CHUNK = 64
D_FF = 5632
EPS = 1e-6

LANES = 128
PAIRS_PER_GROUP = GQA_GROUP // 2
VMEM_LIMIT = 58 * 1024 * 1024
LOG2E = 1.4426950408889634

FFN_TM, FFN_TF = 1024, 512
PROJ_A_TM, PROJ_B_TM, MERGE_TM = 512, 1024, 512
ATTN_TQ, HGRN_TILE = 2048, 512

N_ATTN = ATTN_WIDTH + 2 * KV_WIDTH
COLS_A = N_ATTN + 3 * HG_WIDTH
COLS_B = HG_WIDTH + 2 * D_MODEL

F32 = jnp.float32
BF16 = jnp.bfloat16


def _dot(a, b):
    return jnp.dot(a, b, preferred_element_type=F32)


def _dot_nt(a, b):
    return lax.dot_general(a, b, (((1,), (1,)), ((), ())), preferred_element_type=F32)


def _dot_tn(a, b):
    return lax.dot_general(a, b, (((0,), (0,)), ((), ())), preferred_element_type=F32)


def _sigmoid(z):
    return 0.5 * jnp.tanh(0.5 * z) + 0.5


def _silu(z):
    half = 0.5 * z
    return half * jnp.tanh(half) + half


def _rms_rows(x, gain):
    ms = jnp.mean(x * x, axis=-1, keepdims=True)
    return x * lax.rsqrt(ms + EPS) * gain


BF16_SUBLANES = 16


def _cast_specs(casts, steps, step_of=lambda i: i):
    specs = []
    for w in casts:
        rows = w.shape[0]
        rb = next(r for r in range(BF16_SUBLANES, rows + 1, BF16_SUBLANES) if rows % r == 0 and rows // r <= steps)
        specs.append(pl.BlockSpec(
            (rb, w.shape[1]),
            functools.partial(lambda *idx, last: (jnp.minimum(step_of(*idx), last), 0), last=rows // rb - 1)))
    return specs


def _run_casts(src_refs, dst_refs):
    for src_ref, dst_ref in zip(src_refs, dst_refs):
        dst_ref[...] = src_ref[...].astype(BF16)


FFN_NORM_ROWS = 128
FFN_NORM_FIRST = 2


def _ffn_body(x_hbm, g_ref, wg_ref, wu_ref, wd_ref, *rest, tm, tf):
    n_casts = (len(rest) - 5) // 2
    cast_src, o_ref, cast_dst = rest[:n_casts], rest[n_casts], rest[n_casts + 1:2 * n_casts + 1]
    h0_ref, h1_ref, xs_ref, sem = rest[2 * n_casts + 1:]
    i = pl.program_id(0)
    j = pl.program_id(1)
    n_tiles = pl.num_programs(0)
    hbufs = (h0_ref, h1_ref)

    def x_copy(tile):
        rows = pl.ds(pl.multiple_of(tile * tm, tm), tm)
        return pltpu.make_async_copy(x_hbm.at[rows], xs_ref.at[:tm], sem)

    @pl.when((i == 0) & (j == 0))
    def _():
        xs_ref[tm:, :] = jnp.zeros((FFN_NORM_ROWS, D_MODEL), F32)
        x_copy(0).start()
        x_copy(0).wait()
        h0_ref[:tm, :] = _rms_rows(xs_ref[:tm, :], g_ref[...]).astype(BF16)

    @pl.when((j == FFN_NORM_FIRST) & (i + 1 < n_tiles))
    def _():
        x_copy(i + 1).wait()

    def chunk_step(parity, first):
        h = hbufs[parity][:tm, :]
        halves = [slice(c * tf // 2, (c + 1) * tf // 2) for c in range(2)]
        gate_up = [(_dot(h, wg_ref[:, sl]), _dot(h, wu_ref[:, sl])) for sl in halves]
        act = [(_silu(g) * (u * 0.5)).astype(BF16) for g, u in gate_up]
        down = _dot(act[0], wd_ref[halves[0], :]) + _dot(act[1], wd_ref[halves[1], :])
        if first:
            o_ref[...] = xs_ref[:tm, :] + down

            @pl.when(i + 1 < n_tiles)
            def _():
                x_copy(i + 1).start()
        else:
            o_ref[...] += down
            part = j - FFN_NORM_FIRST
            live = (part < tm // FFN_NORM_ROWS) & (part >= 0) & (i + 1 < n_tiles)
            rows = pl.ds(pl.multiple_of(jnp.where(live, part * FFN_NORM_ROWS, tm), FFN_NORM_ROWS), FFN_NORM_ROWS)
            hbufs[1 - parity][rows, :] = _rms_rows(xs_ref[rows, :], g_ref[...]).astype(BF16)
        _run_casts(cast_src, cast_dst)

    for parity in range(2):
        for first in (True, False):
            pl.when((i % 2 == parity) & ((j == 0) == first))(functools.partial(chunk_step, parity, first))


def _ffn(x, gain, w_gu, w_down, casts=(), *, tm, tf):
    n = x.shape[0]
    nf = D_FF // tf
    assert tm % FFN_NORM_ROWS == 0 and nf >= FFN_NORM_FIRST + tm // FFN_NORM_ROWS
    cast_specs = _cast_specs(casts, (n // tm) * nf, lambda i, j: i * nf + j)
    outs = pl.pallas_call(
        functools.partial(_ffn_body, tm=tm, tf=tf),
        grid=(n // tm, nf),
        in_specs=[
            pl.BlockSpec(memory_space=pl.ANY),
            pl.BlockSpec((1, D_MODEL), lambda i, j: (0, 0)),
            pl.BlockSpec((D_MODEL, tf), lambda i, j: (0, j)),
            pl.BlockSpec((D_MODEL, tf), lambda i, j: (0, j + nf)),
            pl.BlockSpec((tf, D_MODEL), lambda i, j: (j, 0)),
        ] + cast_specs,
        out_specs=[pl.BlockSpec((tm, D_MODEL), lambda i, j: (i, 0))] + cast_specs,
        out_shape=[jax.ShapeDtypeStruct((n, D_MODEL), F32)] + [jax.ShapeDtypeStruct(w.shape, BF16) for w in casts],
        scratch_shapes=[pltpu.VMEM((tm + FFN_NORM_ROWS, D_MODEL), BF16), pltpu.VMEM((tm + FFN_NORM_ROWS, D_MODEL), BF16),
                        pltpu.VMEM((tm + FFN_NORM_ROWS, D_MODEL), F32), pltpu.SemaphoreType.DMA(())],
        compiler_params=pltpu.CompilerParams(
            dimension_semantics=("arbitrary", "arbitrary"), vmem_limit_bytes=VMEM_LIMIT),
        name="ffn",
    )(x, gain, w_gu, w_gu, w_down, *casts)
    return outs[0], outs[1:]


def _proj_a_body(x_ref, g_ref, w_ref, pos_ref, invf_ref, qg_ref, kg_ref, lbt_ref,
                 q_ref, k_ref, v_ref, hq_ref, hk_ref, hl_ref, hi_ref, h_ref, *, layer):
    h = _rms_rows(x_ref[...], g_ref[...]).astype(BF16)
    h_ref[...] = h

    def cols(start, width):
        return _dot(h, w_ref[:, start:start + width])

    lane = lax.broadcasted_iota(jnp.int32, (1, LANES), 1)
    first_half = (lane % HEAD_DIM) < (HEAD_DIM // 2)
    gi = lax.broadcasted_iota(jnp.int32, (LANES, LANES), 0) // HEAD_DIM
    gj = lax.broadcasted_iota(jnp.int32, (LANES, LANES), 1) // HEAD_DIM
    gsum = jnp.where(gi == gj, 1.0, 0.0).astype(BF16)

    def head_norm(v, gain):
        sq = v * v
        hi = sq.astype(BF16)
        lo = (sq - hi.astype(F32)).astype(BF16)
        ss = _dot(hi, gsum) + _dot(lo, gsum)
        return v * lax.rsqrt(ss * (1.0 / HEAD_DIM) + EPS) * gain

    ang = invf_ref[...] * pos_ref[...]
    cos_t = jnp.cos(ang)
    sin_t = jnp.sin(ang)
    reps = LANES // HEAD_DIM
    cosf = jnp.concatenate([cos_t, cos_t] * reps, axis=0).T
    sin_signed = jnp.concatenate([-sin_t, sin_t] * reps, axis=0).T

    def rope(v):
        rot = jnp.where(first_half, pltpu.roll(v, LANES - HEAD_DIM // 2, 1), pltpu.roll(v, HEAD_DIM // 2, 1))
        return v * cosf + rot * sin_signed

    lbt = lbt_ref[...]
    e = jnp.exp(lbt - jnp.max(lbt, axis=0, keepdims=True))
    sm = e / jnp.sum(e, axis=0, keepdims=True)
    lb = jnp.sum(sm[:layer + 1], axis=0, keepdims=True)

    zq = cols(0, ATTN_WIDTH)
    zkv = cols(ATTN_WIDTH, 2 * KV_WIDTH)

    def attn_piece(c):
        if c < ATTN_WIDTH // LANES:
            sl = slice(c * LANES, (c + 1) * LANES)
            q_ref[:, sl] = rope(head_norm(zq[:, sl], qg_ref[...])).astype(BF16)
        else:
            k_ref[...] = rope(head_norm(zkv[:, :KV_WIDTH], kg_ref[...])).astype(BF16)
            v_ref[...] = zkv[:, KV_WIDTH:].astype(BF16)

    def hgrn_piece(kind, sl):
        z = cols(N_ATTN + kind * HG_WIDTH + sl.start, sl.stop - sl.start)
        if kind == 0:
            hq_ref[:, sl] = _silu(z).astype(BF16)
        elif kind == 1:
            f = lb[:, sl] + (1.0 - lb[:, sl]) * _sigmoid(z)
            hk_ref[:, sl] = (1.0 - f).astype(BF16)
            hl_ref[:, sl] = jnp.log2(f)
        else:
            hi_ref[:, sl] = z.astype(BF16)

    chunk = 2 * LANES
    pieces = [(kind, slice(c, c + chunk)) for kind in range(3) for c in range(0, HG_WIDTH, chunk)]
    n_attn_pieces = ATTN_WIDTH // LANES + 1
    for idx, (kind, sl) in enumerate(pieces):
        hgrn_piece(kind, sl)
        if idx < n_attn_pieces:
            attn_piece(idx)


def _proj_a(x, gain, w_a, pos_rows, inv_freq, q_gain, k_gain, lb_table, *, layer, tm):
    n = x.shape[0]
    row = lambda i: (i, 0)
    whole = lambda i: (0, 0)
    widths = (ATTN_WIDTH, KV_WIDTH, KV_WIDTH, HG_WIDTH, HG_WIDTH, HG_WIDTH, HG_WIDTH, D_MODEL)
    dtypes = (BF16, BF16, BF16, BF16, BF16, F32, BF16, BF16)
    return pl.pallas_call(
        functools.partial(_proj_a_body, layer=layer),
        grid=(n // tm,),
        in_specs=[
            pl.BlockSpec((tm, D_MODEL), row),
            pl.BlockSpec((1, D_MODEL), whole),
            pl.BlockSpec((D_MODEL, COLS_A), whole, pipeline_mode=pl.Buffered(1)),
            pl.BlockSpec((None, 1, tm), lambda i: (i, 0, 0)),
            pl.BlockSpec((HEAD_DIM // 2, 1), whole),
            pl.BlockSpec((1, LANES), whole),
            pl.BlockSpec((1, LANES), whole),
            pl.BlockSpec(lb_table.shape, whole),
        ],
        out_specs=[pl.BlockSpec((tm, w), row) for w in widths],
        out_shape=[jax.ShapeDtypeStruct((n, w), d) for w, d in zip(widths, dtypes)],
        compiler_params=pltpu.CompilerParams(
            dimension_semantics=("parallel",), vmem_limit_bytes=VMEM_LIMIT),
        name="proj_a",
    )(x, gain, w_a, pos_rows, inv_freq, q_gain, k_gain, lb_table)


def _proj_b_body(h_ref, w_hbm, *rest):
    n_casts = (len(rest) - 5) // 2
    cast_src, (og_ref, ga_ref, gr_ref) = rest[:n_casts], rest[n_casts:n_casts + 3]
    cast_dst, (w_ref, sem) = rest[n_casts + 3:2 * n_casts + 3], rest[2 * n_casts + 3:]

    @pl.when(pl.program_id(0) == 0)
    def _():
        copy = pltpu.make_async_copy(w_hbm.at[:, COLS_A:], w_ref, sem)
        copy.start()
        copy.wait()

    def cols(start, width):
        return _dot(h_ref[...], w_ref[:, start:start + width])

    og_ref[...] = _silu(cols(0, HG_WIDTH)).astype(BF16)
    for c in range(D_MODEL // HG_WIDTH):
        sl = slice(c * HG_WIDTH, (c + 1) * HG_WIDTH)
        ga_ref[:, sl] = _sigmoid(cols(HG_WIDTH + c * HG_WIDTH, HG_WIDTH)).astype(BF16)
        gr_ref[:, sl] = _sigmoid(cols(HG_WIDTH + D_MODEL + c * HG_WIDTH, HG_WIDTH)).astype(BF16)
    _run_casts(cast_src, cast_dst)


def _proj_b(h, w_in, casts=(), *, tm):
    n = h.shape[0]
    row = lambda i: (i, 0)
    widths = (HG_WIDTH, D_MODEL, D_MODEL)
    cast_specs = _cast_specs(casts, n // tm)
    outs = pl.pallas_call(
        _proj_b_body,
        grid=(n // tm,),
        in_specs=[
            pl.BlockSpec((tm, D_MODEL), row),
            pl.BlockSpec(memory_space=pl.ANY),
        ] + cast_specs,
        out_specs=[pl.BlockSpec((tm, w), row) for w in widths] + cast_specs,
        out_shape=([jax.ShapeDtypeStruct((n, w), BF16) for w in widths]
                   + [jax.ShapeDtypeStruct(w.shape, BF16) for w in casts]),
        scratch_shapes=[pltpu.VMEM((D_MODEL, COLS_B), BF16), pltpu.SemaphoreType.DMA(())],
        compiler_params=pltpu.CompilerParams(
            dimension_semantics=("arbitrary",), vmem_limit_bytes=VMEM_LIMIT),
        name="proj_b",
    )(h, w_in, *casts)
    return outs[:3], outs[3:]


def _attn_body(sinks_ref, q_ref, k_ref, kp_ref, v_ref, vp_ref, o_ref, ks_ref, vs_ref, *, tq):
    t = pl.program_id(1)
    nb = tq // BLOCK
    lane = lax.broadcasted_iota(jnp.int32, (1, LANES), 1)
    low_head = lane < HEAD_DIM

    def put_kv(dst_ref, rows, v, pad):
        swapped = pltpu.roll(v, HEAD_DIM, 1)
        dst_ref[0, rows, :] = jnp.where(low_head, v, pad).astype(BF16)
        dst_ref[1, rows, :] = jnp.where(low_head, pad, swapped).astype(BF16)
        dst_ref[2, rows, :] = jnp.where(low_head, swapped, pad).astype(BF16)
        dst_ref[3, rows, :] = jnp.where(low_head, pad, v).astype(BF16)

    put_kv(ks_ref, slice(0, BLOCK), kp_ref[...].astype(F32), 0.0)
    put_kv(ks_ref, slice(BLOCK, BLOCK + tq), k_ref[...].astype(F32), 0.0)
    put_kv(vs_ref, slice(0, BLOCK), vp_ref[...].astype(F32), 1.0)
    put_kv(vs_ref, slice(BLOCK, BLOCK + tq), v_ref[...].astype(F32), 1.0)

    rows_q = PAIRS_PER_GROUP * BLOCK
    qi = lax.broadcasted_iota(jnp.int32, (rows_q, BLOCK), 0) % BLOCK
    kj = lax.broadcasted_iota(jnp.int32, (rows_q, BLOCK), 1)
    band_prev = kj > qi + BLOCK - WINDOW
    band_cur = kj <= qi
    sink_fill = {}
    for g in range(N_KV_HEADS):
        for half in range(2):
            sink_rows = jnp.concatenate(
                [jnp.full((BLOCK, LANES), sinks_ref[g * GQA_GROUP + 2 * p + half] * LOG2E, F32)
                 for p in range(PAIRS_PER_GROUP)], axis=0)
            sink_fill[g, half] = jnp.where(kj == 0, sink_rows, -jnp.inf)
    first_key = lax.broadcasted_iota(jnp.int32, (2 * BLOCK, LANES), 0) == 0
    drop_value = (first_key & low_head, first_key & jnp.logical_not(low_head))

    def block(n, carry):
        qrows = pl.ds(pl.multiple_of(n * BLOCK, BLOCK), BLOCK)
        krows = pl.ds(pl.multiple_of(n * BLOCK, BLOCK), 2 * BLOCK)
        valid_prev = band_prev & (t * nb + n > 0)

        def scores(g, half):
            qst = jnp.concatenate(
                [q_ref[qrows, (g * PAIRS_PER_GROUP + p) * LANES:(g * PAIRS_PER_GROUP + p + 1) * LANES]
                 for p in range(PAIRS_PER_GROUP)], axis=0)
            return _dot_nt(qst, ks_ref[2 * g + half, krows, :])

        units = [(g, half) for g in range(N_KV_HEADS) for half in range(2)]
        s_next = scores(*units[0])
        o = []
        for idx, (g, half) in enumerate(units):
            s = s_next
            if idx + 1 < len(units):
                s_next = scores(*units[idx + 1])
            s_prev = jnp.where(valid_prev, s[:, :BLOCK], sink_fill[g, half])
            s_cur = jnp.where(band_cur, s[:, BLOCK:], -jnp.inf)
            m = jnp.max(jnp.maximum(s_prev, s_cur), axis=-1, keepdims=True)
            p_ = jnp.concatenate([jnp.exp2(s_prev - m), jnp.exp2(s_cur - m)], axis=1).astype(BF16)
            vwin = vs_ref[2 * g + half, krows, :]
            vwin = jnp.where(drop_value[half], jnp.zeros_like(vwin), vwin)
            o.append(_dot(p_, vwin))
            if half == 1:
                num = jnp.where(low_head, o[0], o[1])
                den = pltpu.roll(jnp.where(low_head, o[1], o[0]), HEAD_DIM, 1)
                outs = num / den
                o = []
                for p in range(PAIRS_PER_GROUP):
                    c = g * PAIRS_PER_GROUP + p
                    o_ref[qrows, c * LANES:(c + 1) * LANES] = outs[p * BLOCK:(p + 1) * BLOCK].astype(BF16)
        return carry

    lax.fori_loop(0, nb, block, 0, unroll=min(4, nb))


def _attn(q, k, v, sinks, *, batch, seq, tq):
    n = batch * seq
    nt = seq // tq
    nb = tq // BLOCK
    bps = seq // BLOCK
    cur = lambda b, t: (b * nt + t, 0)
    prev = lambda b, t: (b * bps + jnp.maximum(t * nb - 1, 0), 0)
    return pl.pallas_call(
        functools.partial(_attn_body, tq=tq),
        grid=(batch, nt),
        in_specs=[
            pl.BlockSpec(memory_space=pltpu.SMEM),
            pl.BlockSpec((tq, ATTN_WIDTH), cur),
            pl.BlockSpec((tq, KV_WIDTH), cur),
            pl.BlockSpec((BLOCK, KV_WIDTH), prev),
            pl.BlockSpec((tq, KV_WIDTH), cur),
            pl.BlockSpec((BLOCK, KV_WIDTH), prev),
        ],
        out_specs=pl.BlockSpec((tq, ATTN_WIDTH), cur),
        out_shape=jax.ShapeDtypeStruct((n, ATTN_WIDTH), BF16),
        scratch_shapes=[
            pltpu.VMEM((4, tq + BLOCK, LANES), BF16),
            pltpu.VMEM((4, tq + BLOCK, LANES), BF16),
        ],
        compiler_params=pltpu.CompilerParams(
            dimension_semantics=("parallel", "parallel"), vmem_limit_bytes=VMEM_LIMIT),
        name="attn",
    )(sinks, q, k, k, v, v)


CHUNK_GROUP = 4


def _hgrn_body(hq_ref, hk_ref, hl_ref, hi_ref, og_ref, gain_ref, o_ref, state_ref, *, tile):
    n_seq = hq_ref.shape[0]

    @pl.when(pl.program_id(0) == 0)
    def _():
        state_ref[...] = jnp.zeros_like(state_ref)

    ri = lax.broadcasted_iota(jnp.int32, (CHUNK, CHUNK), 0)
    ci = lax.broadcasted_iota(jnp.int32, (CHUNK, CHUNK), 1)
    causal = ri >= ci
    tri = jnp.where(causal, 1.0, 0.0).astype(BF16)
    gain = gain_ref[...]
    heads = [slice(h * HG_HEAD_DIM, (h + 1) * HG_HEAD_DIM) for h in range(HG_HEADS)]
    units = [(s, h) for s in range(n_seq) for h in range(HG_HEADS)]

    def chunk_group(cg, carry):
        chunk_rows, q_mid, k_mid, q_in, k_out, decay = [], [], [], [], [], []
        for ch in range(CHUNK_GROUP):
            rows = pl.ds(pl.multiple_of((cg * CHUNK_GROUP + ch) * CHUNK, CHUNK), CHUNK)
            chunk_rows.append(rows)
            for s in range(n_seq):
                lf = hl_ref[s, rows, :]
                hi = lf.astype(BF16)
                lo = (lf - hi.astype(F32)).astype(BF16)
                b = _dot(tri, hi) + _dot(tri, lo)
                b_mid = b[CHUNK // 2:CHUNK // 2 + 1]
                b_last = b[CHUNK - 1:CHUNK]
                qm = hq_ref[s, rows, :].astype(F32) * jnp.exp2(b - b_mid)
                km = hk_ref[s, rows, :].astype(F32) * jnp.exp2(b_mid - b)
                q_mid.append(qm.astype(BF16))
                k_mid.append(km.astype(BF16))
                q_in.append((qm * jnp.exp2(b_mid)).astype(BF16))
                k_out.append((km * jnp.exp2(b_last - b_mid)).astype(BF16))
                decay.append(jnp.exp2(b_last))
        group = [(ch, s, h) for ch in range(CHUNK_GROUP) for s in range(n_seq) for h in range(HG_HEADS)]
        att = {(ch, s, h): _dot_nt(q_mid[ch * n_seq + s][:, heads[h]], k_mid[ch * n_seq + s][:, heads[h]])
               for ch, s, h in group}
        upd = {(ch, s, h): _dot_tn(hi_ref[s, chunk_rows[ch], heads[h]], k_out[ch * n_seq + s][:, heads[h]])
               for ch, s, h in group}
        intra = {(ch, s, h): _dot(jnp.where(causal, att[ch, s, h], 0.0).astype(BF16),
                                  hi_ref[s, chunk_rows[ch], heads[h]]) for ch, s, h in group}
        for ch in range(CHUNK_GROUP):
            st = {(s, h): state_ref[s, h] for s, h in units}
            o_inter = {(s, h): _dot_nt(q_in[ch * n_seq + s][:, heads[h]], st[s, h].astype(BF16)) for s, h in units}
            for s, h in units:
                state_ref[s, h] = st[s, h] * decay[ch * n_seq + s][:, heads[h]] + upd[ch, s, h]
                y = _rms_rows(o_inter[s, h] + intra[ch, s, h], gain) * og_ref[s, chunk_rows[ch], heads[h]].astype(F32)
                o_ref[s, chunk_rows[ch], heads[h]] = y.astype(BF16)
        return carry

    lax.fori_loop(0, tile // (CHUNK * CHUNK_GROUP), chunk_group, 0, unroll=True)


def _hgrn(hq, hk, hl, hi, og, gain, *, batch, seq, tile):
    n = batch * seq
    cur = lambda t: (0, t, 0)
    ins = [a.reshape(batch, seq, HG_WIDTH) for a in (hq, hk, hl, hi, og)]
    out = pl.pallas_call(
        functools.partial(_hgrn_body, tile=tile),
        grid=(seq // tile,),
        in_specs=[pl.BlockSpec((batch, tile, HG_WIDTH), cur)] * 5 + [pl.BlockSpec((1, HG_HEAD_DIM), lambda t: (0, 0))],
        out_specs=pl.BlockSpec((batch, tile, HG_WIDTH), cur),
        out_shape=jax.ShapeDtypeStruct((batch, seq, HG_WIDTH), BF16),
        scratch_shapes=[pltpu.VMEM((batch, HG_HEADS, HG_HEAD_DIM, HG_HEAD_DIM), F32)],
        compiler_params=pltpu.CompilerParams(
            dimension_semantics=("arbitrary",), vmem_limit_bytes=VMEM_LIMIT),
        name="hgrn",
    )(*ins, gain)
    return out.reshape(n, HG_WIDTH)


def _merge_body(x_ref, ya_ref, yh_ref, ga_ref, gr_ref, wa_ref, wr_ref, wo_ref, *rest):
    n_casts = (len(rest) - 1) // 2
    cast_src, o_ref, cast_dst = rest[:n_casts], rest[n_casts], rest[n_casts + 1:]
    a = _dot(ya_ref[...], wa_ref[...])
    r = _dot(yh_ref[...], wr_ref[...])
    merged = (ga_ref[...].astype(F32) * a + gr_ref[...].astype(F32) * r).astype(BF16)
    o_ref[...] = x_ref[...] + _dot(merged, wo_ref[...])
    _run_casts(cast_src, cast_dst)


def _merge(x, ya, yh, ga, gr, w_a, w_r, w_o, casts=(), *, tm):
    n = x.shape[0]
    row = lambda i: (i, 0)
    whole = lambda i: (0, 0)
    cast_specs = _cast_specs(casts, n // tm)
    outs = pl.pallas_call(
        _merge_body,
        grid=(n // tm,),
        in_specs=[
            pl.BlockSpec((tm, D_MODEL), row),
            pl.BlockSpec((tm, ATTN_WIDTH), row),
            pl.BlockSpec((tm, HG_WIDTH), row),
            pl.BlockSpec((tm, D_MODEL), row),
            pl.BlockSpec((tm, D_MODEL), row),
            pl.BlockSpec((ATTN_WIDTH, D_MODEL), whole, pipeline_mode=pl.Buffered(1)),
            pl.BlockSpec((HG_WIDTH, D_MODEL), whole, pipeline_mode=pl.Buffered(1)),
            pl.BlockSpec((D_MODEL, D_MODEL), whole, pipeline_mode=pl.Buffered(1)),
        ] + cast_specs,
        out_specs=[pl.BlockSpec((tm, D_MODEL), row)] + cast_specs,
        out_shape=[jax.ShapeDtypeStruct((n, D_MODEL), F32)] + [jax.ShapeDtypeStruct(w.shape, BF16) for w in casts],
        compiler_params=pltpu.CompilerParams(
            dimension_semantics=("arbitrary",), vmem_limit_bytes=VMEM_LIMIT),
        name="merge",
    )(x, ya, yh, ga, gr, w_a, w_r, w_o, *casts)
    return outs[0], outs[1:]


def _tile(n, want):
    t = min(want, n)
    assert n % t == 0, (n, t)
    return t


def kernel(x, positions, lb_table, ffn1_norm, ffn1_w_gu, ffn1_w_down, mix_norm, w_in, q_norm, k_norm, sinks,
           hg_out_norm, w_attn_branch, w_hg_branch, w_out, ffn2_norm, ffn2_w_gu, ffn2_w_down):
    batch, seq = x.shape[0], x.shape[1]
    n = batch * seq
    depth = w_in.shape[0]
    assert seq % BLOCK == 0 and seq % CHUNK == 0

    half = HEAD_DIM // 2
    inv_freq = (ROPE_THETA ** (-jnp.arange(half, dtype=F32) * 2.0 / HEAD_DIM))[:, None]
    tm_proj = _tile(n, PROJ_A_TM)
    pos_rows = positions.astype(F32).reshape(n // tm_proj, 1, tm_proj)

    xf = x.reshape(n, D_MODEL)
    for l in range(depth):
        xf, (w_in_l,) = _ffn(xf, ffn1_norm[l][None, :], ffn1_w_gu[l].astype(BF16), ffn1_w_down[l].astype(BF16),
                             casts=(w_in[l],), tm=_tile(n, FFN_TM), tf=FFN_TF)
        gain = mix_norm[l][None, :]
        q_gain = jnp.tile(q_norm[l], 2)[None, :] * (HEAD_DIM ** -0.5 * LOG2E)
        q, k, v, hq, hk, hl, hi, h_mix = _proj_a(
            xf, gain, w_in_l, pos_rows, inv_freq, q_gain, jnp.tile(k_norm[l], 2)[None, :], lb_table,
            layer=l, tm=tm_proj)
        (og, ga, gr), (w_a, w_r, w_o) = _proj_b(
            h_mix, w_in_l, casts=(w_attn_branch[l], w_hg_branch[l], w_out[l]), tm=_tile(n, PROJ_B_TM))
        ya = _attn(q, k, v, sinks[l], batch=batch, seq=seq, tq=_tile(seq, ATTN_TQ))
        yh = _hgrn(hq, hk, hl, hi, og, hg_out_norm[l][None, :], batch=batch, seq=seq, tile=_tile(seq, HGRN_TILE))
        xf, (w_gu2, w_down2) = _merge(
            xf, ya, yh, ga, gr, w_a, w_r, w_o, casts=(ffn2_w_gu[l], ffn2_w_down[l]), tm=_tile(n, MERGE_TM))
        xf, _ = _ffn(xf, ffn2_norm[l][None, :], w_gu2, w_down2, tm=_tile(n, FFN_TM), tf=FFN_TF)
    return xf.reshape(batch, seq, D_MODEL)
```

```python
import functools

import jax
import jax.numpy as jnp
from jax import lax
from jax.experimental import pallas as pl
from jax.experimental.pallas import tpu as pltpu

D_MODEL = 2048
HEAD_DIM = 64
N_Q_HEADS = 16
N_KV_HEADS = 2
GQA_GROUP = N_Q_HEADS // N_KV_HEADS
ATTN_WIDTH = N_Q_HEADS * HEAD_DIM
KV_WIDTH = N_KV_HEADS * HEAD_DIM
WINDOW = 128
BLOCK = 128
ROPE_THETA = 10000.0
HG_HEAD_DIM = 128
HG_HEADS = 8
HG_WIDTH = HG_HEADS * HG_HEAD_DIM
CHUNK = 64
D_FF = 5632
EPS = 1e-6

LANES = 128
PAIRS_PER_GROUP = GQA_GROUP // 2
VMEM_LIMIT = 58 * 1024 * 1024
LOG2E = 1.4426950408889634

FFN_TM, FFN_TF = 1024, 512
PROJ_A_TM, PROJ_B_TM, MERGE_TM = 512, 1024, 512
ATTN_TQ, HGRN_TILE = 2048, 512

N_ATTN = ATTN_WIDTH + 2 * KV_WIDTH
COLS_A = N_ATTN + 3 * HG_WIDTH
COLS_B = HG_WIDTH + 2 * D_MODEL

F32 = jnp.float32
BF16 = jnp.bfloat16


def _dot(a, b):
    return jnp.dot(a, b, preferred_element_type=F32)


def _dot_nt(a, b):
    return lax.dot_general(a, b, (((1,), (1,)), ((), ())), preferred_element_type=F32)


def _dot_tn(a, b):
    return lax.dot_general(a, b, (((0,), (0,)), ((), ())), preferred_element_type=F32)


def _sigmoid(z):
    return 0.5 * jnp.tanh(0.5 * z) + 0.5


def _silu(z):
    half = 0.5 * z
    return half * jnp.tanh(half) + half


def _rms_rows(x, gain):
    ms = jnp.mean(x * x, axis=-1, keepdims=True)
    return x * lax.rsqrt(ms + EPS) * gain


BF16_SUBLANES = 16


def _cast_specs(casts, steps, step_of=lambda i: i):
    specs = []
    for w in casts:
        rows = w.shape[0]
        rb = next(r for r in range(BF16_SUBLANES, rows + 1, BF16_SUBLANES) if rows % r == 0 and rows // r <= steps)
        specs.append(pl.BlockSpec(
            (rb, w.shape[1]),
            functools.partial(lambda *idx, last: (jnp.minimum(step_of(*idx), last), 0), last=rows // rb - 1)))
    return specs


def _run_casts(src_refs, dst_refs):
    for src_ref, dst_ref in zip(src_refs, dst_refs):
        dst_ref[...] = src_ref[...].astype(BF16)


FFN_NORM_ROWS = 128
FFN_NORM_FIRST = 2


def _ffn_body(x_hbm, g_ref, wg_ref, wu_ref, wd_ref, *rest, tm, tf):
    n_casts = (len(rest) - 5) // 2
    cast_src, o_ref, cast_dst = rest[:n_casts], rest[n_casts], rest[n_casts + 1:2 * n_casts + 1]
    h0_ref, h1_ref, xs_ref, sem = rest[2 * n_casts + 1:]
    i = pl.program_id(0)
    j = pl.program_id(1)
    n_tiles = pl.num_programs(0)
    hbufs = (h0_ref, h1_ref)

    def x_copy(tile):
        rows = pl.ds(pl.multiple_of(tile * tm, tm), tm)
        return pltpu.make_async_copy(x_hbm.at[rows], xs_ref.at[:tm], sem)

    @pl.when((i == 0) & (j == 0))
    def _():
        xs_ref[tm:, :] = jnp.zeros((FFN_NORM_ROWS, D_MODEL), F32)
        x_copy(0).start()
        x_copy(0).wait()
        h0_ref[:tm, :] = _rms_rows(xs_ref[:tm, :], g_ref[...]).astype(BF16)

    @pl.when((j == FFN_NORM_FIRST) & (i + 1 < n_tiles))
    def _():
        x_copy(i + 1).wait()

    def chunk_step(parity, first):
        h = hbufs[parity][:tm, :]
        halves = [slice(c * tf // 2, (c + 1) * tf // 2) for c in range(2)]
        gate_up = [(_dot(h, wg_ref[:, sl]), _dot(h, wu_ref[:, sl])) for sl in halves]
        act = [(_silu(g) * (u * 0.5)).astype(BF16) for g, u in gate_up]
        down = _dot(act[0], wd_ref[halves[0], :]) + _dot(act[1], wd_ref[halves[1], :])
        if first:
            o_ref[...] = xs_ref[:tm, :] + down

            @pl.when(i + 1 < n_tiles)
            def _():
                x_copy(i + 1).start()
        else:
            o_ref[...] += down
            part = j - FFN_NORM_FIRST
            live = (part < tm // FFN_NORM_ROWS) & (part >= 0) & (i + 1 < n_tiles)
            rows = pl.ds(pl.multiple_of(jnp.where(live, part * FFN_NORM_ROWS, tm), FFN_NORM_ROWS), FFN_NORM_ROWS)
            hbufs[1 - parity][rows, :] = _rms_rows(xs_ref[rows, :], g_ref[...]).astype(BF16)
        _run_casts(cast_src, cast_dst)

    for parity in range(2):
        for first in (True, False):
            pl.when((i % 2 == parity) & ((j == 0) == first))(functools.partial(chunk_step, parity, first))


def _ffn(x, gain, w_gu, w_down, casts=(), *, tm, tf):
    n = x.shape[0]
    nf = D_FF // tf
    assert tm % FFN_NORM_ROWS == 0 and nf >= FFN_NORM_FIRST + tm // FFN_NORM_ROWS
    cast_specs = _cast_specs(casts, (n // tm) * nf, lambda i, j: i * nf + j)
    outs = pl.pallas_call(
        functools.partial(_ffn_body, tm=tm, tf=tf),
        grid=(n // tm, nf),
        in_specs=[
            pl.BlockSpec(memory_space=pl.ANY),
            pl.BlockSpec((1, D_MODEL), lambda i, j: (0, 0)),
            pl.BlockSpec((D_MODEL, tf), lambda i, j: (0, j)),
            pl.BlockSpec((D_MODEL, tf), lambda i, j: (0, j + nf)),
            pl.BlockSpec((tf, D_MODEL), lambda i, j: (j, 0)),
        ] + cast_specs,
        out_specs=[pl.BlockSpec((tm, D_MODEL), lambda i, j: (i, 0))] + cast_specs,
        out_shape=[jax.ShapeDtypeStruct((n, D_MODEL), F32)] + [jax.ShapeDtypeStruct(w.shape, BF16) for w in casts],
        scratch_shapes=[pltpu.VMEM((tm + FFN_NORM_ROWS, D_MODEL), BF16), pltpu.VMEM((tm + FFN_NORM_ROWS, D_MODEL), BF16),
                        pltpu.VMEM((tm + FFN_NORM_ROWS, D_MODEL), F32), pltpu.SemaphoreType.DMA(())],
        compiler_params=pltpu.CompilerParams(
            dimension_semantics=("arbitrary", "arbitrary"), vmem_limit_bytes=VMEM_LIMIT),
        name="ffn",
    )(x, gain, w_gu, w_gu, w_down, *casts)
    return outs[0], outs[1:]


def _proj_a_body(x_ref, g_ref, w_ref, pos_ref, invf_ref, qg_ref, kg_ref, lbt_ref,
                 q_ref, k_ref, v_ref, hq_ref, hk_ref, hl_ref, hi_ref, h_ref, *, layer):
    h = _rms_rows(x_ref[...], g_ref[...]).astype(BF16)
    h_ref[...] = h

    def cols(start, width):
        return _dot(h, w_ref[:, start:start + width])

    lane = lax.broadcasted_iota(jnp.int32, (1, LANES), 1)
    first_half = (lane % HEAD_DIM) < (HEAD_DIM // 2)
    gi = lax.broadcasted_iota(jnp.int32, (LANES, LANES), 0) // HEAD_DIM
    gj = lax.broadcasted_iota(jnp.int32, (LANES, LANES), 1) // HEAD_DIM
    gsum = jnp.where(gi == gj, 1.0, 0.0).astype(BF16)

    def head_norm(v, gain):
        sq = v * v
        hi = sq.astype(BF16)
        lo = (sq - hi.astype(F32)).astype(BF16)
        ss = _dot(hi, gsum) + _dot(lo, gsum)
        return v * lax.rsqrt(ss * (1.0 / HEAD_DIM) + EPS) * gain

    ang = invf_ref[...] * pos_ref[...]
    cos_t = jnp.cos(ang)
    sin_t = jnp.sin(ang)
    reps = LANES // HEAD_DIM
    cosf = jnp.concatenate([cos_t, cos_t] * reps, axis=0).T
    sin_signed = jnp.concatenate([-sin_t, sin_t] * reps, axis=0).T

    def rope(v):
        rot = jnp.where(first_half, pltpu.roll(v, LANES - HEAD_DIM // 2, 1), pltpu.roll(v, HEAD_DIM // 2, 1))
        return v * cosf + rot * sin_signed

    lbt = lbt_ref[...]
    e = jnp.exp(lbt - jnp.max(lbt, axis=0, keepdims=True))
    sm = e / jnp.sum(e, axis=0, keepdims=True)
    lb = jnp.sum(sm[:layer + 1], axis=0, keepdims=True)

    zq = cols(0, ATTN_WIDTH)
    zkv = cols(ATTN_WIDTH, 2 * KV_WIDTH)

    def attn_piece(c):
        if c < ATTN_WIDTH // LANES:
            sl = slice(c * LANES, (c + 1) * LANES)
            q_ref[:, sl] = rope(head_norm(zq[:, sl], qg_ref[...])).astype(BF16)
        else:
            k_ref[...] = rope(head_norm(zkv[:, :KV_WIDTH], kg_ref[...])).astype(BF16)
            v_ref[...] = zkv[:, KV_WIDTH:].astype(BF16)

    def hgrn_dot(kind, sl):
        return cols(N_ATTN + kind * HG_WIDTH + sl.start, sl.stop - sl.start)

    def hgrn_piece(kind, sl, z):
        if kind == 0:
            hq_ref[:, sl] = _silu(z).astype(BF16)
        elif kind == 1:
            f = lb[:, sl] + (1.0 - lb[:, sl]) * _sigmoid(z)
            hk_ref[:, sl] = (1.0 - f).astype(BF16)
            hl_ref[:, sl] = jnp.log2(f)
        else:
            hi_ref[:, sl] = z.astype(BF16)

    chunk = 2 * LANES
    pieces = [(kind, slice(c, c + chunk)) for kind in range(3) for c in range(0, HG_WIDTH, chunk)]
    n_attn_pieces = ATTN_WIDTH // LANES + 1
    z_next = hgrn_dot(*pieces[0])
    for idx, (kind, sl) in enumerate(pieces):
        z = z_next
        if idx + 1 < len(pieces):
            z_next = hgrn_dot(*pieces[idx + 1])
        hgrn_piece(kind, sl, z)
        if idx < n_attn_pieces:
            attn_piece(idx)


def _proj_a(x, gain, w_a, pos_rows, inv_freq, q_gain, k_gain, lb_table, *, layer, tm):
    n = x.shape[0]
    row = lambda i: (i, 0)
    whole = lambda i: (0, 0)
    widths = (ATTN_WIDTH, KV_WIDTH, KV_WIDTH, HG_WIDTH, HG_WIDTH, HG_WIDTH, HG_WIDTH, D_MODEL)
    dtypes = (BF16, BF16, BF16, BF16, BF16, F32, BF16, BF16)
    return pl.pallas_call(
        functools.partial(_proj_a_body, layer=layer),
        grid=(n // tm,),
        in_specs=[
            pl.BlockSpec((tm, D_MODEL), row),
            pl.BlockSpec((1, D_MODEL), whole),
            pl.BlockSpec((D_MODEL, COLS_A), whole, pipeline_mode=pl.Buffered(1)),
            pl.BlockSpec((None, 1, tm), lambda i: (i, 0, 0)),
            pl.BlockSpec((HEAD_DIM // 2, 1), whole),
            pl.BlockSpec((1, LANES), whole),
            pl.BlockSpec((1, LANES), whole),
            pl.BlockSpec(lb_table.shape, whole),
        ],
        out_specs=[pl.BlockSpec((tm, w), row) for w in widths],
        out_shape=[jax.ShapeDtypeStruct((n, w), d) for w, d in zip(widths, dtypes)],
        compiler_params=pltpu.CompilerParams(
            dimension_semantics=("parallel",), vmem_limit_bytes=VMEM_LIMIT),
        name="proj_a",
    )(x, gain, w_a, pos_rows, inv_freq, q_gain, k_gain, lb_table)


def _proj_b_body(h_ref, w_hbm, *rest):
    n_casts = (len(rest) - 5) // 2
    cast_src, (og_ref, ga_ref, gr_ref) = rest[:n_casts], rest[n_casts:n_casts + 3]
    cast_dst, (w_ref, sem) = rest[n_casts + 3:2 * n_casts + 3], rest[2 * n_casts + 3:]

    @pl.when(pl.program_id(0) == 0)
    def _():
        copy = pltpu.make_async_copy(w_hbm.at[:, COLS_A:], w_ref, sem)
        copy.start()
        copy.wait()

    def cols(start, width):
        return _dot(h_ref[...], w_ref[:, start:start + width])

    og_ref[...] = _silu(cols(0, HG_WIDTH)).astype(BF16)
    for c in range(D_MODEL // HG_WIDTH):
        sl = slice(c * HG_WIDTH, (c + 1) * HG_WIDTH)
        ga_ref[:, sl] = _sigmoid(cols(HG_WIDTH + c * HG_WIDTH, HG_WIDTH)).astype(BF16)
        gr_ref[:, sl] = _sigmoid(cols(HG_WIDTH + D_MODEL + c * HG_WIDTH, HG_WIDTH)).astype(BF16)
    _run_casts(cast_src, cast_dst)


def _proj_b(h, w_in, casts=(), *, tm):
    n = h.shape[0]
    row = lambda i: (i, 0)
    widths = (HG_WIDTH, D_MODEL, D_MODEL)
    cast_specs = _cast_specs(casts, n // tm)
    outs = pl.pallas_call(
        _proj_b_body,
        grid=(n // tm,),
        in_specs=[
            pl.BlockSpec((tm, D_MODEL), row),
            pl.BlockSpec(memory_space=pl.ANY),
        ] + cast_specs,
        out_specs=[pl.BlockSpec((tm, w), row) for w in widths] + cast_specs,
        out_shape=([jax.ShapeDtypeStruct((n, w), BF16) for w in widths]
                   + [jax.ShapeDtypeStruct(w.shape, BF16) for w in casts]),
        scratch_shapes=[pltpu.VMEM((D_MODEL, COLS_B), BF16), pltpu.SemaphoreType.DMA(())],
        compiler_params=pltpu.CompilerParams(
            dimension_semantics=("arbitrary",), vmem_limit_bytes=VMEM_LIMIT),
        name="proj_b",
    )(h, w_in, *casts)
    return outs[:3], outs[3:]


def _attn_body(sinks_ref, q_ref, k_ref, kp_ref, v_ref, vp_ref, o_ref, ks_ref, vs_ref, *, tq):
    t = pl.program_id(1)
    nb = tq // BLOCK
    lane = lax.broadcasted_iota(jnp.int32, (1, LANES), 1)
    low_head = lane < HEAD_DIM

    def put_kv(dst_ref, rows, v, pad):
        swapped = pltpu.roll(v, HEAD_DIM, 1)
        dst_ref[0, rows, :] = jnp.where(low_head, v, pad).astype(BF16)
        dst_ref[1, rows, :] = jnp.where(low_head, pad, swapped).astype(BF16)
        dst_ref[2, rows, :] = jnp.where(low_head, swapped, pad).astype(BF16)
        dst_ref[3, rows, :] = jnp.where(low_head, pad, v).astype(BF16)

    put_kv(ks_ref, slice(0, BLOCK), kp_ref[...].astype(F32), 0.0)
    put_kv(ks_ref, slice(BLOCK, BLOCK + tq), k_ref[...].astype(F32), 0.0)
    put_kv(vs_ref, slice(0, BLOCK), vp_ref[...].astype(F32), 1.0)
    put_kv(vs_ref, slice(BLOCK, BLOCK + tq), v_ref[...].astype(F32), 1.0)

    rows_q = PAIRS_PER_GROUP * BLOCK
    qi = lax.broadcasted_iota(jnp.int32, (rows_q, BLOCK), 0) % BLOCK
    kj = lax.broadcasted_iota(jnp.int32, (rows_q, BLOCK), 1)
    band_prev = kj > qi + BLOCK - WINDOW
    band_cur = kj <= qi
    sink_fill = {}
    for g in range(N_KV_HEADS):
        for half in range(2):
            sink_rows = jnp.concatenate(
                [jnp.full((BLOCK, LANES), sinks_ref[g * GQA_GROUP + 2 * p + half] * LOG2E, F32)
                 for p in range(PAIRS_PER_GROUP)], axis=0)
            sink_fill[g, half] = jnp.where(kj == 0, sink_rows, -jnp.inf)
    first_key = lax.broadcasted_iota(jnp.int32, (2 * BLOCK, LANES), 0) == 0
    drop_value = (first_key & low_head, first_key & jnp.logical_not(low_head))

    def block(n, carry):
        qrows = pl.ds(pl.multiple_of(n * BLOCK, BLOCK), BLOCK)
        krows = pl.ds(pl.multiple_of(n * BLOCK, BLOCK), 2 * BLOCK)
        valid_prev = band_prev & (t * nb + n > 0)

        def scores(g, half):
            qst = jnp.concatenate(
                [q_ref[qrows, (g * PAIRS_PER_GROUP + p) * LANES:(g * PAIRS_PER_GROUP + p + 1) * LANES]
                 for p in range(PAIRS_PER_GROUP)], axis=0)
            return _dot_nt(qst, ks_ref[2 * g + half, krows, :])

        units = [(g, half) for g in range(N_KV_HEADS) for half in range(2)]
        s_next = scores(*units[0])
        o = []
        for idx, (g, half) in enumerate(units):
            s = s_next
            if idx + 1 < len(units):
                s_next = scores(*units[idx + 1])
            s_prev = jnp.where(valid_prev, s[:, :BLOCK], sink_fill[g, half])
            s_cur = jnp.where(band_cur, s[:, BLOCK:], -jnp.inf)
            m = jnp.max(jnp.maximum(s_prev, s_cur), axis=-1, keepdims=True)
            p_ = jnp.concatenate([jnp.exp2(s_prev - m), jnp.exp2(s_cur - m)], axis=1).astype(BF16)
            vwin = vs_ref[2 * g + half, krows, :]
            vwin = jnp.where(drop_value[half], jnp.zeros_like(vwin), vwin)
            o.append(_dot(p_, vwin))
            if half == 1:
                num = jnp.where(low_head, o[0], o[1])
                den = pltpu.roll(jnp.where(low_head, o[1], o[0]), HEAD_DIM, 1)
                outs = num / den
                o = []
                for p in range(PAIRS_PER_GROUP):
                    c = g * PAIRS_PER_GROUP + p
                    o_ref[qrows, c * LANES:(c + 1) * LANES] = outs[p * BLOCK:(p + 1) * BLOCK].astype(BF16)
        return carry

    lax.fori_loop(0, nb, block, 0, unroll=min(4, nb))


def _attn(q, k, v, sinks, *, batch, seq, tq):
    n = batch * seq
    nt = seq // tq
    nb = tq // BLOCK
    bps = seq // BLOCK
    cur = lambda b, t: (b * nt + t, 0)
    prev = lambda b, t: (b * bps + jnp.maximum(t * nb - 1, 0), 0)
    return pl.pallas_call(
        functools.partial(_attn_body, tq=tq),
        grid=(batch, nt),
        in_specs=[
            pl.BlockSpec(memory_space=pltpu.SMEM),
            pl.BlockSpec((tq, ATTN_WIDTH), cur),
            pl.BlockSpec((tq, KV_WIDTH), cur),
            pl.BlockSpec((BLOCK, KV_WIDTH), prev),
            pl.BlockSpec((tq, KV_WIDTH), cur),
            pl.BlockSpec((BLOCK, KV_WIDTH), prev),
        ],
        out_specs=pl.BlockSpec((tq, ATTN_WIDTH), cur),
        out_shape=jax.ShapeDtypeStruct((n, ATTN_WIDTH), BF16),
        scratch_shapes=[
            pltpu.VMEM((4, tq + BLOCK, LANES), BF16),
            pltpu.VMEM((4, tq + BLOCK, LANES), BF16),
        ],
        compiler_params=pltpu.CompilerParams(
            dimension_semantics=("parallel", "parallel"), vmem_limit_bytes=VMEM_LIMIT),
        name="attn",
    )(sinks, q, k, k, v, v)


CHUNK_GROUP = 4


def _hgrn_body(hq_ref, hk_ref, hl_ref, hi_ref, og_ref, gain_ref, o_ref, state_ref, *, tile):
    n_seq = hq_ref.shape[0]

    @pl.when(pl.program_id(0) == 0)
    def _():
        state_ref[...] = jnp.zeros_like(state_ref)

    ri = lax.broadcasted_iota(jnp.int32, (CHUNK, CHUNK), 0)
    ci = lax.broadcasted_iota(jnp.int32, (CHUNK, CHUNK), 1)
    causal = ri >= ci
    tri = jnp.where(causal, 1.0, 0.0).astype(BF16)
    gain = gain_ref[...]
    heads = [slice(h * HG_HEAD_DIM, (h + 1) * HG_HEAD_DIM) for h in range(HG_HEADS)]
    units = [(s, h) for s in range(n_seq) for h in range(HG_HEADS)]

    def chunk_group(cg, carry):
        chunk_rows, q_mid, k_mid, q_in, k_out, decay = [], [], [], [], [], []
        for ch in range(CHUNK_GROUP):
            rows = pl.ds(pl.multiple_of((cg * CHUNK_GROUP + ch) * CHUNK, CHUNK), CHUNK)
            chunk_rows.append(rows)
            for s in range(n_seq):
                lf = hl_ref[s, rows, :]
                hi = lf.astype(BF16)
                lo = (lf - hi.astype(F32)).astype(BF16)
                b = _dot(tri, hi) + _dot(tri, lo)
                b_mid = b[CHUNK // 2:CHUNK // 2 + 1]
                b_last = b[CHUNK - 1:CHUNK]
                qm = hq_ref[s, rows, :].astype(F32) * jnp.exp2(b - b_mid)
                km = hk_ref[s, rows, :].astype(F32) * jnp.exp2(b_mid - b)
                q_mid.append(qm.astype(BF16))
                k_mid.append(km.astype(BF16))
                q_in.append((qm * jnp.exp2(b_mid)).astype(BF16))
                k_out.append((km * jnp.exp2(b_last - b_mid)).astype(BF16))
                decay.append(jnp.exp2(b_last))
        group = [(ch, s, h) for ch in range(CHUNK_GROUP) for s in range(n_seq) for h in range(HG_HEADS)]
        att = {(ch, s, h): _dot_nt(q_mid[ch * n_seq + s][:, heads[h]], k_mid[ch * n_seq + s][:, heads[h]])
               for ch, s, h in group}
        upd = {(ch, s, h): _dot_tn(hi_ref[s, chunk_rows[ch], heads[h]], k_out[ch * n_seq + s][:, heads[h]])
               for ch, s, h in group}
        intra = {(ch, s, h): _dot(jnp.where(causal, att[ch, s, h], 0.0).astype(BF16),
                                  hi_ref[s, chunk_rows[ch], heads[h]]) for ch, s, h in group}
        for ch in range(CHUNK_GROUP):
            st = {(s, h): state_ref[s, h] for s, h in units}
            o_inter = {(s, h): _dot_nt(q_in[ch * n_seq + s][:, heads[h]], st[s, h].astype(BF16)) for s, h in units}
            for s, h in units:
                state_ref[s, h] = st[s, h] * decay[ch * n_seq + s][:, heads[h]] + upd[ch, s, h]
                y = _rms_rows(o_inter[s, h] + intra[ch, s, h], gain) * og_ref[s, chunk_rows[ch], heads[h]].astype(F32)
                o_ref[s, chunk_rows[ch], heads[h]] = y.astype(BF16)
        return carry

    lax.fori_loop(0, tile // (CHUNK * CHUNK_GROUP), chunk_group, 0, unroll=True)


def _hgrn(hq, hk, hl, hi, og, gain, *, batch, seq, tile):
    n = batch * seq
    cur = lambda t: (0, t, 0)
    ins = [a.reshape(batch, seq, HG_WIDTH) for a in (hq, hk, hl, hi, og)]
    out = pl.pallas_call(
        functools.partial(_hgrn_body, tile=tile),
        grid=(seq // tile,),
        in_specs=[pl.BlockSpec((batch, tile, HG_WIDTH), cur)] * 5 + [pl.BlockSpec((1, HG_HEAD_DIM), lambda t: (0, 0))],
        out_specs=pl.BlockSpec((batch, tile, HG_WIDTH), cur),
        out_shape=jax.ShapeDtypeStruct((batch, seq, HG_WIDTH), BF16),
        scratch_shapes=[pltpu.VMEM((batch, HG_HEADS, HG_HEAD_DIM, HG_HEAD_DIM), F32)],
        compiler_params=pltpu.CompilerParams(
            dimension_semantics=("arbitrary",), vmem_limit_bytes=VMEM_LIMIT),
        name="hgrn",
    )(*ins, gain)
    return out.reshape(n, HG_WIDTH)


def _merge_body(x_ref, ya_ref, yh_ref, ga_ref, gr_ref, wa_ref, wr_ref, wo_ref, *rest):
    n_casts = (len(rest) - 1) // 2
    cast_src, o_ref, cast_dst = rest[:n_casts], rest[n_casts], rest[n_casts + 1:]
    a = _dot(ya_ref[...], wa_ref[...])
    r = _dot(yh_ref[...], wr_ref[...])
    merged = (ga_ref[...].astype(F32) * a + gr_ref[...].astype(F32) * r).astype(BF16)
    o_ref[...] = x_ref[...] + _dot(merged, wo_ref[...])
    _run_casts(cast_src, cast_dst)


def _merge(x, ya, yh, ga, gr, w_a, w_r, w_o, casts=(), *, tm):
    n = x.shape[0]
    row = lambda i: (i, 0)
    whole = lambda i: (0, 0)
    cast_specs = _cast_specs(casts, n // tm)
    outs = pl.pallas_call(
        _merge_body,
        grid=(n // tm,),
        in_specs=[
            pl.BlockSpec((tm, D_MODEL), row),
            pl.BlockSpec((tm, ATTN_WIDTH), row),
            pl.BlockSpec((tm, HG_WIDTH), row),
            pl.BlockSpec((tm, D_MODEL), row),
            pl.BlockSpec((tm, D_MODEL), row),
            pl.BlockSpec((ATTN_WIDTH, D_MODEL), whole, pipeline_mode=pl.Buffered(1)),
            pl.BlockSpec((HG_WIDTH, D_MODEL), whole, pipeline_mode=pl.Buffered(1)),
            pl.BlockSpec((D_MODEL, D_MODEL), whole, pipeline_mode=pl.Buffered(1)),
        ] + cast_specs,
        out_specs=[pl.BlockSpec((tm, D_MODEL), row)] + cast_specs,
        out_shape=[jax.ShapeDtypeStruct((n, D_MODEL), F32)] + [jax.ShapeDtypeStruct(w.shape, BF16) for w in casts],
        compiler_params=pltpu.CompilerParams(
            dimension_semantics=("arbitrary",), vmem_limit_bytes=VMEM_LIMIT),
        name="merge",
    )(x, ya, yh, ga, gr, w_a, w_r, w_o, *casts)
    return outs[0], outs[1:]


def _tile(n, want):
    t = min(want, n)
    assert n % t == 0, (n, t)
    return t


def kernel(x, positions, lb_table, ffn1_norm, ffn1_w_gu, ffn1_w_down, mix_norm, w_in, q_norm, k_norm, sinks,
           hg_out_norm, w_attn_branch, w_hg_branch, w_out, ffn2_norm, ffn2_w_gu, ffn2_w_down):
    batch, seq = x.shape[0], x.shape[1]
    n = batch * seq
    depth = w_in.shape[0]
    assert seq % BLOCK == 0 and seq % CHUNK == 0

    half = HEAD_DIM // 2
    inv_freq = (ROPE_THETA ** (-jnp.arange(half, dtype=F32) * 2.0 / HEAD_DIM))[:, None]
    tm_proj = _tile(n, PROJ_A_TM)
    pos_rows = positions.astype(F32).reshape(n // tm_proj, 1, tm_proj)

    xf = x.reshape(n, D_MODEL)
    for l in range(depth):
        xf, (w_in_l,) = _ffn(xf, ffn1_norm[l][None, :], ffn1_w_gu[l].astype(BF16), ffn1_w_down[l].astype(BF16),
                             casts=(w_in[l],), tm=_tile(n, FFN_TM), tf=FFN_TF)
        gain = mix_norm[l][None, :]
        q_gain = jnp.tile(q_norm[l], 2)[None, :] * (HEAD_DIM ** -0.5 * LOG2E)
        q, k, v, hq, hk, hl, hi, h_mix = _proj_a(
            xf, gain, w_in_l, pos_rows, inv_freq, q_gain, jnp.tile(k_norm[l], 2)[None, :], lb_table,
            layer=l, tm=tm_proj)
        (og, ga, gr), (w_a, w_r, w_o) = _proj_b(
            h_mix, w_in_l, casts=(w_attn_branch[l], w_hg_branch[l], w_out[l]), tm=_tile(n, PROJ_B_TM))
        ya = _attn(q, k, v, sinks[l], batch=batch, seq=seq, tq=_tile(seq, ATTN_TQ))
        yh = _hgrn(hq, hk, hl, hi, og, hg_out_norm[l][None, :], batch=batch, seq=seq, tile=_tile(seq, HGRN_TILE))
        xf, (w_gu2, w_down2) = _merge(
            xf, ya, yh, ga, gr, w_a, w_r, w_o, casts=(ffn2_w_gu[l], ffn2_w_down[l]), tm=_tile(n, MERGE_TM))
        xf, _ = _ffn(xf, ffn2_norm[l][None, :], w_gu2, w_down2, tm=_tile(n, FFN_TM), tf=FFN_TF)
    return xf.reshape(batch, seq, D_MODEL)
```

```python
import functools

import jax
import jax.numpy as jnp
from jax import lax
from jax.experimental import pallas as pl
from jax.experimental.pallas import tpu as pltpu

D_MODEL = 2048
HEAD_DIM = 64
N_Q_HEADS = 16
N_KV_HEADS = 2
GQA_GROUP = N_Q_HEADS // N_KV_HEADS
ATTN_WIDTH = N_Q_HEADS * HEAD_DIM
KV_WIDTH = N_KV_HEADS * HEAD_DIM
WINDOW = 128
BLOCK = 128
ROPE_THETA = 10000.0
HG_HEAD_DIM = 128
HG_HEADS = 8
HG_WIDTH = HG_HEADS * HG_HEAD_DIM
CHUNK = 64
D_FF = 5632
EPS = 1e-6

LANES = 128
PAIRS_PER_GROUP = GQA_GROUP // 2
VMEM_LIMIT = 58 * 1024 * 1024
LOG2E = 1.4426950408889634

FFN_TM, FFN_TF = 1024, 512
PROJ_A_TM, PROJ_B_TM, MERGE_TM = 512, 1024, 512
ATTN_TQ, HGRN_TILE = 2048, 512

N_ATTN = ATTN_WIDTH + 2 * KV_WIDTH
COLS_A = N_ATTN + 3 * HG_WIDTH
COLS_B = HG_WIDTH + 2 * D_MODEL

F32 = jnp.float32
BF16 = jnp.bfloat16


def _dot(a, b):
    return jnp.dot(a, b, preferred_element_type=F32)


def _dot_nt(a, b):
    return lax.dot_general(a, b, (((1,), (1,)), ((), ())), preferred_element_type=F32)


def _dot_tn(a, b):
    return lax.dot_general(a, b, (((0,), (0,)), ((), ())), preferred_element_type=F32)


def _sigmoid(z):
    return 0.5 * jnp.tanh(0.5 * z) + 0.5


def _silu(z):
    half = 0.5 * z
    return half * jnp.tanh(half) + half


def _rms_rows(x, gain):
    ms = jnp.mean(x * x, axis=-1, keepdims=True)
    return x * lax.rsqrt(ms + EPS) * gain


BF16_SUBLANES = 16


def _cast_specs(casts, steps, step_of=lambda i: i):
    specs = []
    for w in casts:
        rows = w.shape[0]
        rb = next(r for r in range(BF16_SUBLANES, rows + 1, BF16_SUBLANES) if rows % r == 0 and rows // r <= steps)
        specs.append(pl.BlockSpec(
            (rb, w.shape[1]),
            functools.partial(lambda *idx, last: (jnp.minimum(step_of(*idx), last), 0), last=rows // rb - 1)))
    return specs


def _run_casts(src_refs, dst_refs):
    for src_ref, dst_ref in zip(src_refs, dst_refs):
        dst_ref[...] = src_ref[...].astype(BF16)


FFN_NORM_ROWS = 128
FFN_NORM_FIRST = 2


def _ffn_body(x_hbm, g_ref, wg_ref, wu_ref, wd_ref, *rest, tm, tf):
    n_casts = (len(rest) - 5) // 2
    cast_src, o_ref, cast_dst = rest[:n_casts], rest[n_casts], rest[n_casts + 1:2 * n_casts + 1]
    h0_ref, h1_ref, xs_ref, sem = rest[2 * n_casts + 1:]
    i = pl.program_id(0)
    j = pl.program_id(1)
    n_tiles = pl.num_programs(0)
    hbufs = (h0_ref, h1_ref)

    def x_copy(tile):
        rows = pl.ds(pl.multiple_of(tile * tm, tm), tm)
        return pltpu.make_async_copy(x_hbm.at[rows], xs_ref.at[:tm], sem)

    @pl.when((i == 0) & (j == 0))
    def _():
        xs_ref[tm:, :] = jnp.zeros((FFN_NORM_ROWS, D_MODEL), F32)
        x_copy(0).start()
        x_copy(0).wait()
        h0_ref[:tm, :] = _rms_rows(xs_ref[:tm, :], g_ref[...]).astype(BF16)

    @pl.when((j == FFN_NORM_FIRST) & (i + 1 < n_tiles))
    def _():
        x_copy(i + 1).wait()

    def chunk_step(parity, first):
        h = hbufs[parity][:tm, :]
        halves = [slice(c * tf // 2, (c + 1) * tf // 2) for c in range(2)]
        gate_up = [(_dot(h, wg_ref[:, sl]), _dot(h, wu_ref[:, sl])) for sl in halves]
        act = [(_silu(g) * (u * 0.5)).astype(BF16) for g, u in gate_up]
        down = _dot(act[0], wd_ref[halves[0], :]) + _dot(act[1], wd_ref[halves[1], :])
        if first:
            o_ref[...] = xs_ref[:tm, :] + down

            @pl.when(i + 1 < n_tiles)
            def _():
                x_copy(i + 1).start()
        else:
            o_ref[...] += down
            part = j - FFN_NORM_FIRST
            live = (part < tm // FFN_NORM_ROWS) & (part >= 0) & (i + 1 < n_tiles)
            rows = pl.ds(pl.multiple_of(jnp.where(live, part * FFN_NORM_ROWS, tm), FFN_NORM_ROWS), FFN_NORM_ROWS)
            hbufs[1 - parity][rows, :] = _rms_rows(xs_ref[rows, :], g_ref[...]).astype(BF16)
        _run_casts(cast_src, cast_dst)

    for parity in range(2):
        for first in (True, False):
            pl.when((i % 2 == parity) & ((j == 0) == first))(functools.partial(chunk_step, parity, first))


def _ffn(x, gain, w_gu, w_down, casts=(), *, tm, tf):
    n = x.shape[0]
    nf = D_FF // tf
    assert tm % FFN_NORM_ROWS == 0 and nf >= FFN_NORM_FIRST + tm // FFN_NORM_ROWS
    cast_specs = _cast_specs(casts, (n // tm) * nf, lambda i, j: i * nf + j)
    outs = pl.pallas_call(
        functools.partial(_ffn_body, tm=tm, tf=tf),
        grid=(n // tm, nf),
        in_specs=[
            pl.BlockSpec(memory_space=pl.ANY),
            pl.BlockSpec((1, D_MODEL), lambda i, j: (0, 0)),
            pl.BlockSpec((D_MODEL, tf), lambda i, j: (0, j)),
            pl.BlockSpec((D_MODEL, tf), lambda i, j: (0, j + nf)),
            pl.BlockSpec((tf, D_MODEL), lambda i, j: (j, 0)),
        ] + cast_specs,
        out_specs=[pl.BlockSpec((tm, D_MODEL), lambda i, j: (i, 0))] + cast_specs,
        out_shape=[jax.ShapeDtypeStruct((n, D_MODEL), F32)] + [jax.ShapeDtypeStruct(w.shape, BF16) for w in casts],
        scratch_shapes=[pltpu.VMEM((tm + FFN_NORM_ROWS, D_MODEL), BF16), pltpu.VMEM((tm + FFN_NORM_ROWS, D_MODEL), BF16),
                        pltpu.VMEM((tm + FFN_NORM_ROWS, D_MODEL), F32), pltpu.SemaphoreType.DMA(())],
        compiler_params=pltpu.CompilerParams(
            dimension_semantics=("arbitrary", "arbitrary"), vmem_limit_bytes=VMEM_LIMIT),
        name="ffn",
    )(x, gain, w_gu, w_gu, w_down, *casts)
    return outs[0], outs[1:]


def _proj_a_body(x_ref, g_ref, w_ref, pos_ref, invf_ref, qg_ref, kg_ref, lbt_ref,
                 q_ref, k_ref, v_ref, hq_ref, hk_ref, hl_ref, hi_ref, h_ref, *, layer):
    h = _rms_rows(x_ref[...], g_ref[...]).astype(BF16)
    h_ref[...] = h

    def cols(start, width):
        return _dot(h, w_ref[:, start:start + width])

    lane = lax.broadcasted_iota(jnp.int32, (1, LANES), 1)
    first_half = (lane % HEAD_DIM) < (HEAD_DIM // 2)
    gi = lax.broadcasted_iota(jnp.int32, (LANES, LANES), 0) // HEAD_DIM
    gj = lax.broadcasted_iota(jnp.int32, (LANES, LANES), 1) // HEAD_DIM
    gsum = jnp.where(gi == gj, 1.0, 0.0).astype(BF16)

    def head_norm(v, gain):
        sq = v * v
        hi = sq.astype(BF16)
        lo = (sq - hi.astype(F32)).astype(BF16)
        ss = _dot(hi, gsum) + _dot(lo, gsum)
        return v * lax.rsqrt(ss * (1.0 / HEAD_DIM) + EPS) * gain

    ang = invf_ref[...] * pos_ref[...]
    cos_t = jnp.cos(ang)
    sin_t = jnp.sin(ang)
    reps = LANES // HEAD_DIM
    cosf = jnp.concatenate([cos_t, cos_t] * reps, axis=0).T
    sin_signed = jnp.concatenate([-sin_t, sin_t] * reps, axis=0).T

    def rope(v):
        rot = jnp.where(first_half, pltpu.roll(v, LANES - HEAD_DIM // 2, 1), pltpu.roll(v, HEAD_DIM // 2, 1))
        return v * cosf + rot * sin_signed

    lbt = lbt_ref[...]
    e = jnp.exp(lbt - jnp.max(lbt, axis=0, keepdims=True))
    sm = e / jnp.sum(e, axis=0, keepdims=True)
    lb = jnp.sum(sm[:layer + 1], axis=0, keepdims=True)

    zq = cols(0, ATTN_WIDTH)
    zkv = cols(ATTN_WIDTH, 2 * KV_WIDTH)

    def attn_piece(c):
        if c < ATTN_WIDTH // LANES:
            sl = slice(c * LANES, (c + 1) * LANES)
            q_ref[:, sl] = rope(head_norm(zq[:, sl], qg_ref[...])).astype(BF16)
        else:
            k_ref[...] = rope(head_norm(zkv[:, :KV_WIDTH], kg_ref[...])).astype(BF16)
            v_ref[...] = zkv[:, KV_WIDTH:].astype(BF16)

    def hgrn_dot(kind, sl):
        return cols(N_ATTN + kind * HG_WIDTH + sl.start, sl.stop - sl.start)

    def hgrn_piece(kind, sl, z):
        if kind == 0:
            hq_ref[:, sl] = _silu(z).astype(BF16)
        elif kind == 1:
            f = lb[:, sl] + (1.0 - lb[:, sl]) * _sigmoid(z)
            hk_ref[:, sl] = (1.0 - f).astype(BF16)
            hl_ref[:, sl] = jnp.log2(f)
        else:
            hi_ref[:, sl] = z.astype(BF16)

    chunk = 2 * LANES
    pieces = [(kind, slice(c, c + chunk)) for kind in range(3) for c in range(0, HG_WIDTH, chunk)]
    n_attn_pieces = ATTN_WIDTH // LANES + 1
    z_next = hgrn_dot(*pieces[0])
    for idx, (kind, sl) in enumerate(pieces):
        z = z_next
        if idx + 1 < len(pieces):
            z_next = hgrn_dot(*pieces[idx + 1])
        hgrn_piece(kind, sl, z)
        if idx < n_attn_pieces:
            attn_piece(idx)


def _proj_a(x, gain, w_a, pos_rows, inv_freq, q_gain, k_gain, lb_table, *, layer, tm):
    n = x.shape[0]
    row = lambda i: (i, 0)
    whole = lambda i: (0, 0)
    widths = (ATTN_WIDTH, KV_WIDTH, KV_WIDTH, HG_WIDTH, HG_WIDTH, HG_WIDTH, HG_WIDTH, D_MODEL)
    dtypes = (BF16, BF16, BF16, BF16, BF16, F32, BF16, BF16)
    return pl.pallas_call(
        functools.partial(_proj_a_body, layer=layer),
        grid=(n // tm,),
        in_specs=[
            pl.BlockSpec((tm, D_MODEL), row),
            pl.BlockSpec((1, D_MODEL), whole),
            pl.BlockSpec((D_MODEL, COLS_A), whole, pipeline_mode=pl.Buffered(1)),
            pl.BlockSpec((None, 1, tm), lambda i: (i, 0, 0)),
            pl.BlockSpec((HEAD_DIM // 2, 1), whole),
            pl.BlockSpec((1, LANES), whole),
            pl.BlockSpec((1, LANES), whole),
            pl.BlockSpec(lb_table.shape, whole),
        ],
        out_specs=[pl.BlockSpec((tm, w), row) for w in widths],
        out_shape=[jax.ShapeDtypeStruct((n, w), d) for w, d in zip(widths, dtypes)],
        compiler_params=pltpu.CompilerParams(
            dimension_semantics=("parallel",), vmem_limit_bytes=VMEM_LIMIT),
        name="proj_a",
    )(x, gain, w_a, pos_rows, inv_freq, q_gain, k_gain, lb_table)


def _proj_b_body(h_ref, w_hbm, *rest):
    n_casts = (len(rest) - 5) // 2
    cast_src, (og_ref, ga_ref, gr_ref) = rest[:n_casts], rest[n_casts:n_casts + 3]
    cast_dst, (w_ref, sem) = rest[n_casts + 3:2 * n_casts + 3], rest[2 * n_casts + 3:]

    @pl.when(pl.program_id(0) == 0)
    def _():
        copy = pltpu.make_async_copy(w_hbm.at[:, COLS_A:], w_ref, sem)
        copy.start()
        copy.wait()

    def cols(start, width):
        return _dot(h_ref[...], w_ref[:, start:start + width])

    og_ref[...] = _silu(cols(0, HG_WIDTH)).astype(BF16)
    for c in range(D_MODEL // HG_WIDTH):
        sl = slice(c * HG_WIDTH, (c + 1) * HG_WIDTH)
        ga_ref[:, sl] = _sigmoid(cols(HG_WIDTH + c * HG_WIDTH, HG_WIDTH)).astype(BF16)
        gr_ref[:, sl] = _sigmoid(cols(HG_WIDTH + D_MODEL + c * HG_WIDTH, HG_WIDTH)).astype(BF16)
    _run_casts(cast_src, cast_dst)


def _proj_b(h, w_in, casts=(), *, tm):
    n = h.shape[0]
    row = lambda i: (i, 0)
    widths = (HG_WIDTH, D_MODEL, D_MODEL)
    cast_specs = _cast_specs(casts, n // tm)
    outs = pl.pallas_call(
        _proj_b_body,
        grid=(n // tm,),
        in_specs=[
            pl.BlockSpec((tm, D_MODEL), row),
            pl.BlockSpec(memory_space=pl.ANY),
        ] + cast_specs,
        out_specs=[pl.BlockSpec((tm, w), row) for w in widths] + cast_specs,
        out_shape=([jax.ShapeDtypeStruct((n, w), BF16) for w in widths]
                   + [jax.ShapeDtypeStruct(w.shape, BF16) for w in casts]),
        scratch_shapes=[pltpu.VMEM((D_MODEL, COLS_B), BF16), pltpu.SemaphoreType.DMA(())],
        compiler_params=pltpu.CompilerParams(
            dimension_semantics=("arbitrary",), vmem_limit_bytes=VMEM_LIMIT),
        name="proj_b",
    )(h, w_in, *casts)
    return outs[:3], outs[3:]


def _attn_body(sinks_ref, q_ref, k_ref, kp_ref, v_ref, vp_ref, o_ref, ks_ref, vs_ref, *, tq):
    t = pl.program_id(1)
    nb = tq // BLOCK
    lane = lax.broadcasted_iota(jnp.int32, (1, LANES), 1)
    low_head = lane < HEAD_DIM

    def put_kv(dst_ref, rows, v, pad):
        swapped = pltpu.roll(v, HEAD_DIM, 1)
        dst_ref[0, rows, :] = jnp.where(low_head, v, pad).astype(BF16)
        dst_ref[1, rows, :] = jnp.where(low_head, pad, swapped).astype(BF16)
        dst_ref[2, rows, :] = jnp.where(low_head, swapped, pad).astype(BF16)
        dst_ref[3, rows, :] = jnp.where(low_head, pad, v).astype(BF16)

    put_kv(ks_ref, slice(0, BLOCK), kp_ref[...].astype(F32), 0.0)
    put_kv(ks_ref, slice(BLOCK, BLOCK + tq), k_ref[...].astype(F32), 0.0)
    put_kv(vs_ref, slice(0, BLOCK), vp_ref[...].astype(F32), 1.0)
    put_kv(vs_ref, slice(BLOCK, BLOCK + tq), v_ref[...].astype(F32), 1.0)

    rows_q = PAIRS_PER_GROUP * BLOCK
    qi = lax.broadcasted_iota(jnp.int32, (rows_q, BLOCK), 0) % BLOCK
    kj = lax.broadcasted_iota(jnp.int32, (rows_q, BLOCK), 1)
    band_prev = kj > qi + BLOCK - WINDOW
    band_cur = kj <= qi
    sink_fill = {}
    for g in range(N_KV_HEADS):
        for half in range(2):
            sink_rows = jnp.concatenate(
                [jnp.full((BLOCK, LANES), sinks_ref[g * GQA_GROUP + 2 * p + half] * LOG2E, F32)
                 for p in range(PAIRS_PER_GROUP)], axis=0)
            sink_fill[g, half] = jnp.where(kj == 0, sink_rows, -jnp.inf)
    first_key = lax.broadcasted_iota(jnp.int32, (2 * BLOCK, LANES), 0) == 0
    drop_value = (first_key & low_head, first_key & jnp.logical_not(low_head))

    def block(n, carry):
        qrows = pl.ds(pl.multiple_of(n * BLOCK, BLOCK), BLOCK)
        krows = pl.ds(pl.multiple_of(n * BLOCK, BLOCK), 2 * BLOCK)
        valid_prev = band_prev & (t * nb + n > 0)

        def scores(g, half):
            qst = jnp.concatenate(
                [q_ref[qrows, (g * PAIRS_PER_GROUP + p) * LANES:(g * PAIRS_PER_GROUP + p + 1) * LANES]
                 for p in range(PAIRS_PER_GROUP)], axis=0)
            return _dot_nt(qst, ks_ref[2 * g + half, krows, :])

        units = [(g, half) for g in range(N_KV_HEADS) for half in range(2)]
        s_next = scores(*units[0])
        o = []
        for idx, (g, half) in enumerate(units):
            s = s_next
            if idx + 1 < len(units):
                s_next = scores(*units[idx + 1])
            s_prev = jnp.where(valid_prev, s[:, :BLOCK], sink_fill[g, half])
            s_cur = jnp.where(band_cur, s[:, BLOCK:], -jnp.inf)
            m = jnp.max(jnp.maximum(s_prev, s_cur), axis=-1, keepdims=True)
            p_ = jnp.concatenate([jnp.exp2(s_prev - m), jnp.exp2(s_cur - m)], axis=1).astype(BF16)
            vwin = vs_ref[2 * g + half, krows, :]
            vwin = jnp.where(drop_value[half], jnp.zeros_like(vwin), vwin)
            o.append(_dot(p_, vwin))
            if half == 1:
                num = jnp.where(low_head, o[0], o[1])
                den = pltpu.roll(jnp.where(low_head, o[1], o[0]), HEAD_DIM, 1)
                outs = num / den
                o = []
                for p in range(PAIRS_PER_GROUP):
                    c = g * PAIRS_PER_GROUP + p
                    o_ref[qrows, c * LANES:(c + 1) * LANES] = outs[p * BLOCK:(p + 1) * BLOCK].astype(BF16)
        return carry

    lax.fori_loop(0, nb, block, 0, unroll=min(4, nb))


def _attn(q, k, v, sinks, *, batch, seq, tq):
    n = batch * seq
    nt = seq // tq
    nb = tq // BLOCK
    bps = seq // BLOCK
    cur = lambda b, t: (b * nt + t, 0)
    prev = lambda b, t: (b * bps + jnp.maximum(t * nb - 1, 0), 0)
    return pl.pallas_call(
        functools.partial(_attn_body, tq=tq),
        grid=(batch, nt),
        in_specs=[
            pl.BlockSpec(memory_space=pltpu.SMEM),
            pl.BlockSpec((tq, ATTN_WIDTH), cur),
            pl.BlockSpec((tq, KV_WIDTH), cur),
            pl.BlockSpec((BLOCK, KV_WIDTH), prev),
            pl.BlockSpec((tq, KV_WIDTH), cur),
            pl.BlockSpec((BLOCK, KV_WIDTH), prev),
        ],
        out_specs=pl.BlockSpec((tq, ATTN_WIDTH), cur),
        out_shape=jax.ShapeDtypeStruct((n, ATTN_WIDTH), BF16),
        scratch_shapes=[
            pltpu.VMEM((4, tq + BLOCK, LANES), BF16),
            pltpu.VMEM((4, tq + BLOCK, LANES), BF16),
        ],
        compiler_params=pltpu.CompilerParams(
            dimension_semantics=("parallel", "parallel"), vmem_limit_bytes=VMEM_LIMIT),
        name="attn",
    )(sinks, q, k, k, v, v)


CHUNK_GROUP = 4


def _hgrn_body(hq_ref, hk_ref, hl_ref, hi_ref, og_ref, gain_ref, o_ref, state_ref, *, tile):
    n_seq = hq_ref.shape[0]

    @pl.when(pl.program_id(0) == 0)
    def _():
        state_ref[...] = jnp.zeros_like(state_ref)

    ri = lax.broadcasted_iota(jnp.int32, (CHUNK, CHUNK), 0)
    ci = lax.broadcasted_iota(jnp.int32, (CHUNK, CHUNK), 1)
    causal = ri >= ci
    tri = jnp.where(causal, 1.0, 0.0).astype(BF16)
    gain = gain_ref[...]
    heads = [slice(h * HG_HEAD_DIM, (h + 1) * HG_HEAD_DIM) for h in range(HG_HEADS)]
    units = [(s, h) for s in range(n_seq) for h in range(HG_HEADS)]

    def independent(cg):
        chunk_rows, q_mid, k_mid, q_in, k_out, decay = [], [], [], [], [], []
        for ch in range(CHUNK_GROUP):
            rows = pl.ds((cg * CHUNK_GROUP + ch) * CHUNK, CHUNK)
            chunk_rows.append(rows)
            for s in range(n_seq):
                lf = hl_ref[s, rows, :]
                hi = lf.astype(BF16)
                lo = (lf - hi.astype(F32)).astype(BF16)
                b = _dot(tri, hi) + _dot(tri, lo)
                b_mid = b[CHUNK // 2:CHUNK // 2 + 1]
                b_last = b[CHUNK - 1:CHUNK]
                qm = hq_ref[s, rows, :].astype(F32) * jnp.exp2(b - b_mid)
                km = hk_ref[s, rows, :].astype(F32) * jnp.exp2(b_mid - b)
                q_mid.append(qm.astype(BF16))
                k_mid.append(km.astype(BF16))
                q_in.append((qm * jnp.exp2(b_mid)).astype(BF16))
                k_out.append((km * jnp.exp2(b_last - b_mid)).astype(BF16))
                decay.append(jnp.exp2(b_last))
        group = [(ch, s, h) for ch in range(CHUNK_GROUP) for s in range(n_seq) for h in range(HG_HEADS)]
        att = {(ch, s, h): _dot_nt(q_mid[ch * n_seq + s][:, heads[h]], k_mid[ch * n_seq + s][:, heads[h]])
               for ch, s, h in group}
        upd = {(ch, s, h): _dot_tn(hi_ref[s, chunk_rows[ch], heads[h]], k_out[ch * n_seq + s][:, heads[h]])
               for ch, s, h in group}
        intra = {(ch, s, h): _dot(jnp.where(causal, att[ch, s, h], 0.0).astype(BF16),
                                  hi_ref[s, chunk_rows[ch], heads[h]]) for ch, s, h in group}
        return chunk_rows, q_in, decay, upd, intra

    def state_chain(chunk_rows, q_in, decay, upd, intra):
        for ch in range(CHUNK_GROUP):
            st = {(s, h): state_ref[s, h] for s, h in units}
            o_inter = {(s, h): _dot_nt(q_in[ch * n_seq + s][:, heads[h]], st[s, h].astype(BF16)) for s, h in units}
            for s, h in units:
                state_ref[s, h] = st[s, h] * decay[ch * n_seq + s][:, heads[h]] + upd[ch, s, h]
                y = _rms_rows(o_inter[s, h] + intra[ch, s, h], gain) * og_ref[s, chunk_rows[ch], heads[h]].astype(F32)
                o_ref[s, chunk_rows[ch], heads[h]] = y.astype(BF16)

    n_groups = tile // (CHUNK * CHUNK_GROUP)
    ready = independent(0)
    for cg in range(n_groups):
        current = ready
        if cg + 1 < n_groups:
            ready = independent(cg + 1)
        state_chain(*current)


def _hgrn(hq, hk, hl, hi, og, gain, *, batch, seq, tile):
    n = batch * seq
    cur = lambda t: (0, t, 0)
    ins = [a.reshape(batch, seq, HG_WIDTH) for a in (hq, hk, hl, hi, og)]
    out = pl.pallas_call(
        functools.partial(_hgrn_body, tile=tile),
        grid=(seq // tile,),
        in_specs=[pl.BlockSpec((batch, tile, HG_WIDTH), cur)] * 5 + [pl.BlockSpec((1, HG_HEAD_DIM), lambda t: (0, 0))],
        out_specs=pl.BlockSpec((batch, tile, HG_WIDTH), cur),
        out_shape=jax.ShapeDtypeStruct((batch, seq, HG_WIDTH), BF16),
        scratch_shapes=[pltpu.VMEM((batch, HG_HEADS, HG_HEAD_DIM, HG_HEAD_DIM), F32)],
        compiler_params=pltpu.CompilerParams(
            dimension_semantics=("arbitrary",), vmem_limit_bytes=VMEM_LIMIT),
        name="hgrn",
    )(*ins, gain)
    return out.reshape(n, HG_WIDTH)


def _merge_body(x_ref, ya_ref, yh_ref, ga_ref, gr_ref, wa_ref, wr_ref, wo_ref, *rest):
    n_casts = (len(rest) - 1) // 2
    cast_src, o_ref, cast_dst = rest[:n_casts], rest[n_casts], rest[n_casts + 1:]
    a = _dot(ya_ref[...], wa_ref[...])
    r = _dot(yh_ref[...], wr_ref[...])
    merged = (ga_ref[...].astype(F32) * a + gr_ref[...].astype(F32) * r).astype(BF16)
    o_ref[...] = x_ref[...] + _dot(merged, wo_ref[...])
    _run_casts(cast_src, cast_dst)


def _merge(x, ya, yh, ga, gr, w_a, w_r, w_o, casts=(), *, tm):
    n = x.shape[0]
    row = lambda i: (i, 0)
    whole = lambda i: (0, 0)
    cast_specs = _cast_specs(casts, n // tm)
    outs = pl.pallas_call(
        _merge_body,
        grid=(n // tm,),
        in_specs=[
            pl.BlockSpec((tm, D_MODEL), row),
            pl.BlockSpec((tm, ATTN_WIDTH), row),
            pl.BlockSpec((tm, HG_WIDTH), row),
            pl.BlockSpec((tm, D_MODEL), row),
            pl.BlockSpec((tm, D_MODEL), row),
            pl.BlockSpec((ATTN_WIDTH, D_MODEL), whole, pipeline_mode=pl.Buffered(1)),
            pl.BlockSpec((HG_WIDTH, D_MODEL), whole, pipeline_mode=pl.Buffered(1)),
            pl.BlockSpec((D_MODEL, D_MODEL), whole, pipeline_mode=pl.Buffered(1)),
        ] + cast_specs,
        out_specs=[pl.BlockSpec((tm, D_MODEL), row)] + cast_specs,
        out_shape=[jax.ShapeDtypeStruct((n, D_MODEL), F32)] + [jax.ShapeDtypeStruct(w.shape, BF16) for w in casts],
        compiler_params=pltpu.CompilerParams(
            dimension_semantics=("arbitrary",), vmem_limit_bytes=VMEM_LIMIT),
        name="merge",
    )(x, ya, yh, ga, gr, w_a, w_r, w_o, *casts)
    return outs[0], outs[1:]


def _tile(n, want):
    t = min(want, n)
    assert n % t == 0, (n, t)
    return t


def kernel(x, positions, lb_table, ffn1_norm, ffn1_w_gu, ffn1_w_down, mix_norm, w_in, q_norm, k_norm, sinks,
           hg_out_norm, w_attn_branch, w_hg_branch, w_out, ffn2_norm, ffn2_w_gu, ffn2_w_down):
    batch, seq = x.shape[0], x.shape[1]
    n = batch * seq
    depth = w_in.shape[0]
    assert seq % BLOCK == 0 and seq % CHUNK == 0

    half = HEAD_DIM // 2
    inv_freq = (ROPE_THETA ** (-jnp.arange(half, dtype=F32) * 2.0 / HEAD_DIM))[:, None]
    tm_proj = _tile(n, PROJ_A_TM)
    pos_rows = positions.astype(F32).reshape(n // tm_proj, 1, tm_proj)

    xf = x.reshape(n, D_MODEL)
    for l in range(depth):
        xf, (w_in_l,) = _ffn(xf, ffn1_norm[l][None, :], ffn1_w_gu[l].astype(BF16), ffn1_w_down[l].astype(BF16),
                             casts=(w_in[l],), tm=_tile(n, FFN_TM), tf=FFN_TF)
        gain = mix_norm[l][None, :]
        q_gain = jnp.tile(q_norm[l], 2)[None, :] * (HEAD_DIM ** -0.5 * LOG2E)
        q, k, v, hq, hk, hl, hi, h_mix = _proj_a(
            xf, gain, w_in_l, pos_rows, inv_freq, q_gain, jnp.tile(k_norm[l], 2)[None, :], lb_table,
            layer=l, tm=tm_proj)
        (og, ga, gr), (w_a, w_r, w_o) = _proj_b(
            h_mix, w_in_l, casts=(w_attn_branch[l], w_hg_branch[l], w_out[l]), tm=_tile(n, PROJ_B_TM))
        ya = _attn(q, k, v, sinks[l], batch=batch, seq=seq, tq=_tile(seq, ATTN_TQ))
        yh = _hgrn(hq, hk, hl, hi, og, hg_out_norm[l][None, :], batch=batch, seq=seq, tile=_tile(seq, HGRN_TILE))
        xf, (w_gu2, w_down2) = _merge(
            xf, ya, yh, ga, gr, w_a, w_r, w_o, casts=(ffn2_w_gu[l], ffn2_w_down[l]), tm=_tile(n, MERGE_TM))
        xf, _ = _ffn(xf, ffn2_norm[l][None, :], w_gu2, w_down2, tm=_tile(n, FFN_TM), tf=FFN_TF)
    return xf.reshape(batch, seq, D_MODEL)
```

```python
import functools

import jax
import jax.numpy as jnp
from jax import lax
from jax.experimental import pallas as pl
from jax.experimental.pallas import tpu as pltpu

D_MODEL = 2048
HEAD_DIM = 64
N_Q_HEADS = 16
N_KV_HEADS = 2
GQA_GROUP = N_Q_HEADS // N_KV_HEADS
ATTN_WIDTH = N_Q_HEADS * HEAD_DIM
KV_WIDTH = N_KV_HEADS * HEAD_DIM
WINDOW = 128
BLOCK = 128
ROPE_THETA = 10000.0
HG_HEAD_DIM = 128
HG_HEADS = 8
HG_WIDTH = HG_HEADS * HG_HEAD_DIM
CHUNK = 64
D_FF = 5632
EPS = 1e-6

LANES = 128
PAIRS_PER_GROUP = GQA_GROUP // 2
VMEM_LIMIT = 58 * 1024 * 1024
LOG2E = 1.4426950408889634

FFN_TM, FFN_TF = 1024, 512
PROJ_A_TM, PROJ_B_TM, MERGE_TM = 512, 1024, 512
ATTN_TQ, HGRN_TILE = 2048, 512

N_ATTN = ATTN_WIDTH + 2 * KV_WIDTH
COLS_A = N_ATTN + 3 * HG_WIDTH
COLS_B = HG_WIDTH + 2 * D_MODEL

F32 = jnp.float32
BF16 = jnp.bfloat16


def _dot(a, b):
    return jnp.dot(a, b, preferred_element_type=F32)


def _dot_nt(a, b):
    return lax.dot_general(a, b, (((1,), (1,)), ((), ())), preferred_element_type=F32)


def _dot_tn(a, b):
    return lax.dot_general(a, b, (((0,), (0,)), ((), ())), preferred_element_type=F32)


def _sigmoid(z):
    return 0.5 * jnp.tanh(0.5 * z) + 0.5


def _silu(z):
    half = 0.5 * z
    return half * jnp.tanh(half) + half


def _rms_rows(x, gain):
    ms = jnp.mean(x * x, axis=-1, keepdims=True)
    return x * lax.rsqrt(ms + EPS) * gain


BF16_SUBLANES = 16


def _cast_specs(casts, steps, step_of=lambda i: i):
    specs = []
    for w in casts:
        rows = w.shape[0]
        rb = next(r for r in range(BF16_SUBLANES, rows + 1, BF16_SUBLANES) if rows % r == 0 and rows // r <= steps)
        specs.append(pl.BlockSpec(
            (rb, w.shape[1]),
            functools.partial(lambda *idx, last: (jnp.minimum(step_of(*idx), last), 0), last=rows // rb - 1)))
    return specs


def _run_casts(src_refs, dst_refs):
    for src_ref, dst_ref in zip(src_refs, dst_refs):
        dst_ref[...] = src_ref[...].astype(BF16)


FFN_NORM_ROWS = 128
FFN_NORM_FIRST = 2


def _ffn_body(x_hbm, g_ref, wg_ref, wu_ref, wd_ref, *rest, tm, tf):
    n_casts = (len(rest) - 5) // 2
    cast_src, o_ref, cast_dst = rest[:n_casts], rest[n_casts], rest[n_casts + 1:2 * n_casts + 1]
    h0_ref, h1_ref, xs_ref, sem = rest[2 * n_casts + 1:]
    i = pl.program_id(0)
    j = pl.program_id(1)
    n_tiles = pl.num_programs(0)
    hbufs = (h0_ref, h1_ref)

    def x_copy(tile):
        rows = pl.ds(pl.multiple_of(tile * tm, tm), tm)
        return pltpu.make_async_copy(x_hbm.at[rows], xs_ref.at[:tm], sem)

    @pl.when((i == 0) & (j == 0))
    def _():
        xs_ref[tm:, :] = jnp.zeros((FFN_NORM_ROWS, D_MODEL), F32)
        x_copy(0).start()
        x_copy(0).wait()
        h0_ref[:tm, :] = _rms_rows(xs_ref[:tm, :], g_ref[...]).astype(BF16)

    @pl.when((j == FFN_NORM_FIRST) & (i + 1 < n_tiles))
    def _():
        x_copy(i + 1).wait()

    def chunk_step(parity, first):
        h = hbufs[parity][:tm, :]
        halves = [slice(c * tf // 2, (c + 1) * tf // 2) for c in range(2)]
        gate_up = [(_dot(h, wg_ref[:, sl]), _dot(h, wu_ref[:, sl])) for sl in halves]
        act = [(_silu(g) * (u * 0.5)).astype(BF16) for g, u in gate_up]
        down = _dot(act[0], wd_ref[halves[0], :]) + _dot(act[1], wd_ref[halves[1], :])
        if first:
            o_ref[...] = xs_ref[:tm, :] + down

            @pl.when(i + 1 < n_tiles)
            def _():
                x_copy(i + 1).start()
        else:
            o_ref[...] += down
            part = j - FFN_NORM_FIRST
            live = (part < tm // FFN_NORM_ROWS) & (part >= 0) & (i + 1 < n_tiles)
            rows = pl.ds(pl.multiple_of(jnp.where(live, part * FFN_NORM_ROWS, tm), FFN_NORM_ROWS), FFN_NORM_ROWS)
            hbufs[1 - parity][rows, :] = _rms_rows(xs_ref[rows, :], g_ref[...]).astype(BF16)
        _run_casts(cast_src, cast_dst)

    for parity in range(2):
        for first in (True, False):
            pl.when((i % 2 == parity) & ((j == 0) == first))(functools.partial(chunk_step, parity, first))


def _ffn(x, gain, w_gu, w_down, casts=(), *, tm, tf):
    n = x.shape[0]
    nf = D_FF // tf
    assert tm % FFN_NORM_ROWS == 0 and nf >= FFN_NORM_FIRST + tm // FFN_NORM_ROWS
    cast_specs = _cast_specs(casts, (n // tm) * nf, lambda i, j: i * nf + j)
    outs = pl.pallas_call(
        functools.partial(_ffn_body, tm=tm, tf=tf),
        grid=(n // tm, nf),
        in_specs=[
            pl.BlockSpec(memory_space=pl.ANY),
            pl.BlockSpec((1, D_MODEL), lambda i, j: (0, 0)),
            pl.BlockSpec((D_MODEL, tf), lambda i, j: (0, j)),
            pl.BlockSpec((D_MODEL, tf), lambda i, j: (0, j + nf)),
            pl.BlockSpec((tf, D_MODEL), lambda i, j: (j, 0)),
        ] + cast_specs,
        out_specs=[pl.BlockSpec((tm, D_MODEL), lambda i, j: (i, 0))] + cast_specs,
        out_shape=[jax.ShapeDtypeStruct((n, D_MODEL), F32)] + [jax.ShapeDtypeStruct(w.shape, BF16) for w in casts],
        scratch_shapes=[pltpu.VMEM((tm + FFN_NORM_ROWS, D_MODEL), BF16), pltpu.VMEM((tm + FFN_NORM_ROWS, D_MODEL), BF16),
                        pltpu.VMEM((tm + FFN_NORM_ROWS, D_MODEL), F32), pltpu.SemaphoreType.DMA(())],
        compiler_params=pltpu.CompilerParams(
            dimension_semantics=("arbitrary", "arbitrary"), vmem_limit_bytes=VMEM_LIMIT),
        name="ffn",
    )(x, gain, w_gu, w_gu, w_down, *casts)
    return outs[0], outs[1:]


def _proj_a_body(x_ref, g_ref, w_ref, pos_ref, invf_ref, qg_ref, kg_ref, lbt_ref,
                 q_ref, k_ref, v_ref, hq_ref, hk_ref, hl_ref, hi_ref, h_ref, *, layer):
    h = _rms_rows(x_ref[...], g_ref[...]).astype(BF16)
    h_ref[...] = h

    def cols(start, width):
        return _dot(h, w_ref[:, start:start + width])

    lane = lax.broadcasted_iota(jnp.int32, (1, LANES), 1)
    first_half = (lane % HEAD_DIM) < (HEAD_DIM // 2)
    gi = lax.broadcasted_iota(jnp.int32, (LANES, LANES), 0) // HEAD_DIM
    gj = lax.broadcasted_iota(jnp.int32, (LANES, LANES), 1) // HEAD_DIM
    gsum = jnp.where(gi == gj, 1.0, 0.0).astype(BF16)

    def head_norm(v, gain):
        sq = v * v
        hi = sq.astype(BF16)
        lo = (sq - hi.astype(F32)).astype(BF16)
        ss = _dot(hi, gsum) + _dot(lo, gsum)
        return v * lax.rsqrt(ss * (1.0 / HEAD_DIM) + EPS) * gain

    ang = invf_ref[...] * pos_ref[...]
    cos_t = jnp.cos(ang)
    sin_t = jnp.sin(ang)
    reps = LANES // HEAD_DIM
    cosf = jnp.concatenate([cos_t, cos_t] * reps, axis=0).T
    sin_signed = jnp.concatenate([-sin_t, sin_t] * reps, axis=0).T

    def rope(v):
        rot = jnp.where(first_half, pltpu.roll(v, LANES - HEAD_DIM // 2, 1), pltpu.roll(v, HEAD_DIM // 2, 1))
        return v * cosf + rot * sin_signed

    lbt = lbt_ref[...]
    e = jnp.exp(lbt - jnp.max(lbt, axis=0, keepdims=True))
    sm = e / jnp.sum(e, axis=0, keepdims=True)
    lb = jnp.sum(sm[:layer + 1], axis=0, keepdims=True)

    zq = cols(0, ATTN_WIDTH)
    zkv = cols(ATTN_WIDTH, 2 * KV_WIDTH)

    def attn_piece(c):
        if c < ATTN_WIDTH // LANES:
            sl = slice(c * LANES, (c + 1) * LANES)
            q_ref[:, sl] = rope(head_norm(zq[:, sl], qg_ref[...])).astype(BF16)
        else:
            k_ref[...] = rope(head_norm(zkv[:, :KV_WIDTH], kg_ref[...])).astype(BF16)
            v_ref[...] = zkv[:, KV_WIDTH:].astype(BF16)

    def hgrn_dot(kind, sl):
        return cols(N_ATTN + kind * HG_WIDTH + sl.start, sl.stop - sl.start)

    def hgrn_piece(kind, sl, z):
        if kind == 0:
            hq_ref[:, sl] = _silu(z).astype(BF16)
        elif kind == 1:
            f = lb[:, sl] + (1.0 - lb[:, sl]) * _sigmoid(z)
            hk_ref[:, sl] = (1.0 - f).astype(BF16)
            hl_ref[:, sl] = jnp.log2(f)
        else:
            hi_ref[:, sl] = z.astype(BF16)

    chunk = 2 * LANES
    pieces = [(kind, slice(c, c + chunk)) for kind in range(3) for c in range(0, HG_WIDTH, chunk)]
    n_attn_pieces = ATTN_WIDTH // LANES + 1
    z_next = hgrn_dot(*pieces[0])
    for idx, (kind, sl) in enumerate(pieces):
        z = z_next
        if idx + 1 < len(pieces):
            z_next = hgrn_dot(*pieces[idx + 1])
        hgrn_piece(kind, sl, z)
        if idx < n_attn_pieces:
            attn_piece(idx)


def _proj_a(x, gain, w_a, pos_rows, inv_freq, q_gain, k_gain, lb_table, *, layer, tm):
    n = x.shape[0]
    row = lambda i: (i, 0)
    whole = lambda i: (0, 0)
    widths = (ATTN_WIDTH, KV_WIDTH, KV_WIDTH, HG_WIDTH, HG_WIDTH, HG_WIDTH, HG_WIDTH, D_MODEL)
    dtypes = (BF16, BF16, BF16, BF16, BF16, F32, BF16, BF16)
    return pl.pallas_call(
        functools.partial(_proj_a_body, layer=layer),
        grid=(n // tm,),
        in_specs=[
            pl.BlockSpec((tm, D_MODEL), row),
            pl.BlockSpec((1, D_MODEL), whole),
            pl.BlockSpec((D_MODEL, COLS_A), whole, pipeline_mode=pl.Buffered(1)),
            pl.BlockSpec((None, 1, tm), lambda i: (i, 0, 0)),
            pl.BlockSpec((HEAD_DIM // 2, 1), whole),
            pl.BlockSpec((1, LANES), whole),
            pl.BlockSpec((1, LANES), whole),
            pl.BlockSpec(lb_table.shape, whole),
        ],
        out_specs=[pl.BlockSpec((tm, w), row) for w in widths],
        out_shape=[jax.ShapeDtypeStruct((n, w), d) for w, d in zip(widths, dtypes)],
        compiler_params=pltpu.CompilerParams(
            dimension_semantics=("parallel",), vmem_limit_bytes=VMEM_LIMIT),
        name="proj_a",
    )(x, gain, w_a, pos_rows, inv_freq, q_gain, k_gain, lb_table)


def _proj_b_body(h_ref, w_hbm, *rest):
    n_casts = (len(rest) - 5) // 2
    cast_src, (og_ref, ga_ref, gr_ref) = rest[:n_casts], rest[n_casts:n_casts + 3]
    cast_dst, (w_ref, sem) = rest[n_casts + 3:2 * n_casts + 3], rest[2 * n_casts + 3:]

    @pl.when(pl.program_id(0) == 0)
    def _():
        copy = pltpu.make_async_copy(w_hbm.at[:, COLS_A:], w_ref, sem)
        copy.start()
        copy.wait()

    def cols(start, width):
        return _dot(h_ref[...], w_ref[:, start:start + width])

    og_ref[...] = _silu(cols(0, HG_WIDTH)).astype(BF16)
    for c in range(D_MODEL // HG_WIDTH):
        sl = slice(c * HG_WIDTH, (c + 1) * HG_WIDTH)
        ga_ref[:, sl] = _sigmoid(cols(HG_WIDTH + c * HG_WIDTH, HG_WIDTH)).astype(BF16)
        gr_ref[:, sl] = _sigmoid(cols(HG_WIDTH + D_MODEL + c * HG_WIDTH, HG_WIDTH)).astype(BF16)
    _run_casts(cast_src, cast_dst)


def _proj_b(h, w_in, casts=(), *, tm):
    n = h.shape[0]
    row = lambda i: (i, 0)
    widths = (HG_WIDTH, D_MODEL, D_MODEL)
    cast_specs = _cast_specs(casts, n // tm)
    outs = pl.pallas_call(
        _proj_b_body,
        grid=(n // tm,),
        in_specs=[
            pl.BlockSpec((tm, D_MODEL), row),
            pl.BlockSpec(memory_space=pl.ANY),
        ] + cast_specs,
        out_specs=[pl.BlockSpec((tm, w), row) for w in widths] + cast_specs,
        out_shape=([jax.ShapeDtypeStruct((n, w), BF16) for w in widths]
                   + [jax.ShapeDtypeStruct(w.shape, BF16) for w in casts]),
        scratch_shapes=[pltpu.VMEM((D_MODEL, COLS_B), BF16), pltpu.SemaphoreType.DMA(())],
        compiler_params=pltpu.CompilerParams(
            dimension_semantics=("arbitrary",), vmem_limit_bytes=VMEM_LIMIT),
        name="proj_b",
    )(h, w_in, *casts)
    return outs[:3], outs[3:]


def _attn_body(sinks_ref, q_ref, k_ref, kp_ref, v_ref, vp_ref, o_ref, ks_ref, vs_ref, *, tq):
    t = pl.program_id(1)
    nb = tq // BLOCK
    lane = lax.broadcasted_iota(jnp.int32, (1, LANES), 1)
    low_head = lane < HEAD_DIM

    def put_kv(dst_ref, rows, v, pad):
        swapped = pltpu.roll(v, HEAD_DIM, 1)
        dst_ref[0, rows, :] = jnp.where(low_head, v, pad).astype(BF16)
        dst_ref[1, rows, :] = jnp.where(low_head, pad, swapped).astype(BF16)
        dst_ref[2, rows, :] = jnp.where(low_head, swapped, pad).astype(BF16)
        dst_ref[3, rows, :] = jnp.where(low_head, pad, v).astype(BF16)

    put_kv(ks_ref, slice(0, BLOCK), kp_ref[...].astype(F32), 0.0)
    put_kv(ks_ref, slice(BLOCK, BLOCK + tq), k_ref[...].astype(F32), 0.0)
    put_kv(vs_ref, slice(0, BLOCK), vp_ref[...].astype(F32), 1.0)
    put_kv(vs_ref, slice(BLOCK, BLOCK + tq), v_ref[...].astype(F32), 1.0)

    rows_q = PAIRS_PER_GROUP * BLOCK

    def block(n, carry):
        qi = lax.broadcasted_iota(jnp.int32, (rows_q, BLOCK), 0) % BLOCK
        kj = lax.broadcasted_iota(jnp.int32, (rows_q, BLOCK), 1)
        band_prev = kj > qi + BLOCK - WINDOW
        band_cur = kj <= qi
        sink_fill = {}
        for g in range(N_KV_HEADS):
            for half in range(2):
                sink_rows = jnp.concatenate(
                    [jnp.full((BLOCK, LANES), sinks_ref[g * GQA_GROUP + 2 * p + half] * LOG2E, F32)
                     for p in range(PAIRS_PER_GROUP)], axis=0)
                sink_fill[g, half] = jnp.where(kj == 0, sink_rows, -jnp.inf)
        first_key = lax.broadcasted_iota(jnp.int32, (2 * BLOCK, LANES), 0) == 0
        drop_value = (first_key & low_head, first_key & jnp.logical_not(low_head))

        qrows = pl.ds(pl.multiple_of(n * BLOCK, BLOCK), BLOCK)
        krows = pl.ds(pl.multiple_of(n * BLOCK, BLOCK), 2 * BLOCK)
        valid_prev = band_prev & (t * nb + n > 0)

        def scores(g, half):
            qst = jnp.concatenate(
                [q_ref[qrows, (g * PAIRS_PER_GROUP + p) * LANES:(g * PAIRS_PER_GROUP + p + 1) * LANES]
                 for p in range(PAIRS_PER_GROUP)], axis=0)
            return _dot_nt(qst, ks_ref[2 * g + half, krows, :])

        units = [(g, half) for g in range(N_KV_HEADS) for half in range(2)]
        s_next = scores(*units[0])
        o = []
        for idx, (g, half) in enumerate(units):
            s = s_next
            if idx + 1 < len(units):
                s_next = scores(*units[idx + 1])
            s_prev = jnp.where(valid_prev, s[:, :BLOCK], sink_fill[g, half])
            s_cur = jnp.where(band_cur, s[:, BLOCK:], -jnp.inf)
            m = jnp.max(jnp.maximum(s_prev, s_cur), axis=-1, keepdims=True)
            p_ = jnp.concatenate([jnp.exp2(s_prev - m), jnp.exp2(s_cur - m)], axis=1).astype(BF16)
            vwin = vs_ref[2 * g + half, krows, :]
            vwin = jnp.where(drop_value[half], jnp.zeros_like(vwin), vwin)
            o.append(_dot(p_, vwin))
            if half == 1:
                num = jnp.where(low_head, o[0], o[1])
                den = pltpu.roll(jnp.where(low_head, o[1], o[0]), HEAD_DIM, 1)
                outs = num / den
                o = []
                for p in range(PAIRS_PER_GROUP):
                    c = g * PAIRS_PER_GROUP + p
                    o_ref[qrows, c * LANES:(c + 1) * LANES] = outs[p * BLOCK:(p + 1) * BLOCK].astype(BF16)
        return carry

    lax.fori_loop(0, nb, block, 0, unroll=min(4, nb))


def _attn(q, k, v, sinks, *, batch, seq, tq):
    n = batch * seq
    nt = seq // tq
    nb = tq // BLOCK
    bps = seq // BLOCK
    cur = lambda b, t: (b * nt + t, 0)
    prev = lambda b, t: (b * bps + jnp.maximum(t * nb - 1, 0), 0)
    return pl.pallas_call(
        functools.partial(_attn_body, tq=tq),
        grid=(batch, nt),
        in_specs=[
            pl.BlockSpec(memory_space=pltpu.SMEM),
            pl.BlockSpec((tq, ATTN_WIDTH), cur),
            pl.BlockSpec((tq, KV_WIDTH), cur),
            pl.BlockSpec((BLOCK, KV_WIDTH), prev),
            pl.BlockSpec((tq, KV_WIDTH), cur),
            pl.BlockSpec((BLOCK, KV_WIDTH), prev),
        ],
        out_specs=pl.BlockSpec((tq, ATTN_WIDTH), cur),
        out_shape=jax.ShapeDtypeStruct((n, ATTN_WIDTH), BF16),
        scratch_shapes=[
            pltpu.VMEM((4, tq + BLOCK, LANES), BF16),
            pltpu.VMEM((4, tq + BLOCK, LANES), BF16),
        ],
        compiler_params=pltpu.CompilerParams(
            dimension_semantics=("parallel", "parallel"), vmem_limit_bytes=VMEM_LIMIT),
        name="attn",
    )(sinks, q, k, k, v, v)


CHUNK_GROUP = 4


def _hgrn_body(hq_ref, hk_ref, hl_ref, hi_ref, og_ref, gain_ref, o_ref, state_ref, *, tile):
    n_seq = hq_ref.shape[0]

    @pl.when(pl.program_id(0) == 0)
    def _():
        state_ref[...] = jnp.zeros_like(state_ref)

    ri = lax.broadcasted_iota(jnp.int32, (CHUNK, CHUNK), 0)
    ci = lax.broadcasted_iota(jnp.int32, (CHUNK, CHUNK), 1)
    causal = ri >= ci
    tri = jnp.where(causal, 1.0, 0.0).astype(BF16)
    gain = gain_ref[...]
    heads = [slice(h * HG_HEAD_DIM, (h + 1) * HG_HEAD_DIM) for h in range(HG_HEADS)]
    units = [(s, h) for s in range(n_seq) for h in range(HG_HEADS)]

    def independent(cg):
        chunk_rows, q_mid, k_mid, q_in, k_out, decay = [], [], [], [], [], []
        for ch in range(CHUNK_GROUP):
            rows = pl.ds((cg * CHUNK_GROUP + ch) * CHUNK, CHUNK)
            chunk_rows.append(rows)
            for s in range(n_seq):
                lf = hl_ref[s, rows, :]
                hi = lf.astype(BF16)
                lo = (lf - hi.astype(F32)).astype(BF16)
                b = _dot(tri, hi) + _dot(tri, lo)
                b_mid = b[CHUNK // 2:CHUNK // 2 + 1]
                b_last = b[CHUNK - 1:CHUNK]
                qm = hq_ref[s, rows, :].astype(F32) * jnp.exp2(b - b_mid)
                km = hk_ref[s, rows, :].astype(F32) * jnp.exp2(b_mid - b)
                q_mid.append(qm.astype(BF16))
                k_mid.append(km.astype(BF16))
                q_in.append((qm * jnp.exp2(b_mid)).astype(BF16))
                k_out.append((km * jnp.exp2(b_last - b_mid)).astype(BF16))
                decay.append(jnp.exp2(b_last))
        group = [(ch, s, h) for ch in range(CHUNK_GROUP) for s in range(n_seq) for h in range(HG_HEADS)]
        att = {(ch, s, h): _dot_nt(q_mid[ch * n_seq + s][:, heads[h]], k_mid[ch * n_seq + s][:, heads[h]])
               for ch, s, h in group}
        upd = {(ch, s, h): _dot_tn(hi_ref[s, chunk_rows[ch], heads[h]], k_out[ch * n_seq + s][:, heads[h]])
               for ch, s, h in group}
        intra = {(ch, s, h): _dot(jnp.where(causal, att[ch, s, h], 0.0).astype(BF16),
                                  hi_ref[s, chunk_rows[ch], heads[h]]) for ch, s, h in group}
        return chunk_rows, q_in, decay, upd, intra

    def state_chain(chunk_rows, q_in, decay, upd, intra):
        for ch in range(CHUNK_GROUP):
            st = {(s, h): state_ref[s, h] for s, h in units}
            o_inter = {(s, h): _dot_nt(q_in[ch * n_seq + s][:, heads[h]], st[s, h].astype(BF16)) for s, h in units}
            for s, h in units:
                state_ref[s, h] = st[s, h] * decay[ch * n_seq + s][:, heads[h]] + upd[ch, s, h]
                y = _rms_rows(o_inter[s, h] + intra[ch, s, h], gain) * og_ref[s, chunk_rows[ch], heads[h]].astype(F32)
                o_ref[s, chunk_rows[ch], heads[h]] = y.astype(BF16)

    n_groups = tile // (CHUNK * CHUNK_GROUP)
    ready = independent(0)
    for cg in range(n_groups):
        current = ready
        if cg + 1 < n_groups:
            ready = independent(cg + 1)
        state_chain(*current)


def _hgrn(hq, hk, hl, hi, og, gain, *, batch, seq, tile):
    n = batch * seq
    cur = lambda t: (0, t, 0)
    ins = [a.reshape(batch, seq, HG_WIDTH) for a in (hq, hk, hl, hi, og)]
    out = pl.pallas_call(
        functools.partial(_hgrn_body, tile=tile),
        grid=(seq // tile,),
        in_specs=[pl.BlockSpec((batch, tile, HG_WIDTH), cur)] * 5 + [pl.BlockSpec((1, HG_HEAD_DIM), lambda t: (0, 0))],
        out_specs=pl.BlockSpec((batch, tile, HG_WIDTH), cur),
        out_shape=jax.ShapeDtypeStruct((batch, seq, HG_WIDTH), BF16),
        scratch_shapes=[pltpu.VMEM((batch, HG_HEADS, HG_HEAD_DIM, HG_HEAD_DIM), F32)],
        compiler_params=pltpu.CompilerParams(
            dimension_semantics=("arbitrary",), vmem_limit_bytes=VMEM_LIMIT),
        name="hgrn",
    )(*ins, gain)
    return out.reshape(n, HG_WIDTH)


def _merge_body(x_ref, ya_ref, yh_ref, ga_ref, gr_ref, wa_ref, wr_ref, wo_ref, *rest):
    n_casts = (len(rest) - 1) // 2
    cast_src, o_ref, cast_dst = rest[:n_casts], rest[n_casts], rest[n_casts + 1:]
    a = _dot(ya_ref[...], wa_ref[...])
    r = _dot(yh_ref[...], wr_ref[...])
    merged = (ga_ref[...].astype(F32) * a + gr_ref[...].astype(F32) * r).astype(BF16)
    o_ref[...] = x_ref[...] + _dot(merged, wo_ref[...])
    _run_casts(cast_src, cast_dst)


def _merge(x, ya, yh, ga, gr, w_a, w_r, w_o, casts=(), *, tm):
    n = x.shape[0]
    row = lambda i: (i, 0)
    whole = lambda i: (0, 0)
    cast_specs = _cast_specs(casts, n // tm)
    outs = pl.pallas_call(
        _merge_body,
        grid=(n // tm,),
        in_specs=[
            pl.BlockSpec((tm, D_MODEL), row),
            pl.BlockSpec((tm, ATTN_WIDTH), row),
            pl.BlockSpec((tm, HG_WIDTH), row),
            pl.BlockSpec((tm, D_MODEL), row),
            pl.BlockSpec((tm, D_MODEL), row),
            pl.BlockSpec((ATTN_WIDTH, D_MODEL), whole, pipeline_mode=pl.Buffered(1)),
            pl.BlockSpec((HG_WIDTH, D_MODEL), whole, pipeline_mode=pl.Buffered(1)),
            pl.BlockSpec((D_MODEL, D_MODEL), whole, pipeline_mode=pl.Buffered(1)),
        ] + cast_specs,
        out_specs=[pl.BlockSpec((tm, D_MODEL), row)] + cast_specs,
        out_shape=[jax.ShapeDtypeStruct((n, D_MODEL), F32)] + [jax.ShapeDtypeStruct(w.shape, BF16) for w in casts],
        compiler_params=pltpu.CompilerParams(
            dimension_semantics=("arbitrary",), vmem_limit_bytes=VMEM_LIMIT),
        name="merge",
    )(x, ya, yh, ga, gr, w_a, w_r, w_o, *casts)
    return outs[0], outs[1:]


def _tile(n, want):
    t = min(want, n)
    assert n % t == 0, (n, t)
    return t


def kernel(x, positions, lb_table, ffn1_norm, ffn1_w_gu, ffn1_w_down, mix_norm, w_in, q_norm, k_norm, sinks,
           hg_out_norm, w_attn_branch, w_hg_branch, w_out, ffn2_norm, ffn2_w_gu, ffn2_w_down):
    batch, seq = x.shape[0], x.shape[1]
    n = batch * seq
    depth = w_in.shape[0]
    assert seq % BLOCK == 0 and seq % CHUNK == 0

    half = HEAD_DIM // 2
    inv_freq = (ROPE_THETA ** (-jnp.arange(half, dtype=F32) * 2.0 / HEAD_DIM))[:, None]
    tm_proj = _tile(n, PROJ_A_TM)
    pos_rows = positions.astype(F32).reshape(n // tm_proj, 1, tm_proj)

    xf = x.reshape(n, D_MODEL)
    for l in range(depth):
        xf, (w_in_l,) = _ffn(xf, ffn1_norm[l][None, :], ffn1_w_gu[l].astype(BF16), ffn1_w_down[l].astype(BF16),
                             casts=(w_in[l],), tm=_tile(n, FFN_TM), tf=FFN_TF)
        gain = mix_norm[l][None, :]
        q_gain = jnp.tile(q_norm[l], 2)[None, :] * (HEAD_DIM ** -0.5 * LOG2E)
        q, k, v, hq, hk, hl, hi, h_mix = _proj_a(
            xf, gain, w_in_l, pos_rows, inv_freq, q_gain, jnp.tile(k_norm[l], 2)[None, :], lb_table,
            layer=l, tm=tm_proj)
        (og, ga, gr), (w_a, w_r, w_o) = _proj_b(
            h_mix, w_in_l, casts=(w_attn_branch[l], w_hg_branch[l], w_out[l]), tm=_tile(n, PROJ_B_TM))
        ya = _attn(q, k, v, sinks[l], batch=batch, seq=seq, tq=_tile(seq, ATTN_TQ))
        yh = _hgrn(hq, hk, hl, hi, og, hg_out_norm[l][None, :], batch=batch, seq=seq, tile=_tile(seq, HGRN_TILE))
        xf, (w_gu2, w_down2) = _merge(
            xf, ya, yh, ga, gr, w_a, w_r, w_o, casts=(ffn2_w_gu[l], ffn2_w_down[l]), tm=_tile(n, MERGE_TM))
        xf, _ = _ffn(xf, ffn2_norm[l][None, :], w_gu2, w_down2, tm=_tile(n, FFN_TM), tf=FFN_TF)
    return xf.reshape(batch, seq, D_MODEL)
```
